```python
import math
import jax
import jax.numpy as jnp
from jax import lax
import numpy as np

D_MODEL = 2048
BATCH = 2
SEQ = 16384
DEPTH = 1
DEC_BATCH = 16
DEC_SEQ = 2048
PAST_LEN = 128

D_S5 = D_MODEL // 2
S5_GROUP = 16
S5_GROUPS = D_S5 // S5_GROUP
S5_STATE = 64
DT_MIN = 1e-3
DT_MAX = 1e-1
D_RWKV = D_MODEL // 2
RWKV_HEAD = 64
RWKV_HEADS = D_RWKV // RWKV_HEAD
LORA_RANK = 64
D_RW_IN = 3 * D_RWKV + 4 * LORA_RANK
N_IN = 2 * D_S5 + D_RW_IN + D_RWKV + 2 * D_MODEL
IN_SPLITS = (D_S5, 2 * D_S5, 2 * D_S5 + D_RW_IN, 2 * D_S5 + D_RW_IN + D_RWKV)
RW_SPLITS = (D_RWKV, 2 * D_RWKV, 3 * D_RWKV, 3 * D_RWKV + LORA_RANK, 3 * D_RWKV + 2 * LORA_RANK, 3 * D_RWKV + 3 * LORA_RANK)
RMS_EPS = 1e-6
GN_EPS = 64e-5
L2_EPS = 1e-12

kernel_name = 'bidir_s5_rwkv7_gated_hybrid'


def rms_norm(x, g):
    x32 = x.astype(jnp.float32)
    return x32 * lax.rsqrt(jnp.mean(jnp.square(x32), axis=-1, keepdims=True) + RMS_EPS) * g.astype(jnp.float32)


def centred_shift_mix(z, mu):
    zero = jnp.zeros_like(z[:, :1])
    prev = jnp.concatenate([zero, z[:, :-1]], axis=1)
    nxt = jnp.concatenate([z[:, 1:], zero], axis=1)
    return z + (0.5 * (prev + nxt) - z) * mu


def _ssm_combine(e_i, e_j):
    a_i, b_i = e_i
    a_j, b_j = e_j
    return a_j * a_i, a_j * b_i + b_j


def s5_direction(u, lam_re, lam_im, log_dt, b_re, b_im, c_re, c_im, reverse):
    f32 = jnp.float32
    lam = lax.complex(lam_re.astype(f32), lam_im.astype(f32))
    dt = jnp.exp(log_dt.astype(f32))[:, None]
    lam_bar = jnp.exp(lam * dt)
    b_bar = ((lam_bar - 1.0) / lam)[..., None] * lax.complex(b_re.astype(f32), b_im.astype(f32))
    c = lax.complex(c_re.astype(f32), c_im.astype(f32))

    def one_sequence(u_seq):
        bu = jnp.einsum('gph,lgh->lgp', b_bar, u_seq.astype(jnp.complex64))
        a = jnp.broadcast_to(lam_bar, bu.shape)
        _, states = lax.associative_scan(_ssm_combine, (a, bu), axis=0, reverse=reverse)
        return jnp.einsum('ghp,lgp->lgh', c, states).real

    return lax.map(one_sequence, u)


def s5_branch(u_flat, gate, lam_re, lam_im, log_dt, b_re, b_im, c_re, c_im, d, w_glu, w_out):
    bsz, seq, _ = u_flat.shape
    u = u_flat.astype(jnp.float32).reshape(bsz, seq, S5_GROUPS, S5_GROUP)
    y = d.astype(jnp.float32).reshape(S5_GROUPS, S5_GROUP) * u
    for dirn, rev in ((0, False), (1, True)):
        y = y + s5_direction(u, lam_re[dirn], lam_im[dirn], log_dt[dirn], b_re[dirn], b_im[dirn],
                             c_re[dirn], c_im[dirn], rev)
    z = jax.nn.gelu(y.reshape(bsz, seq, D_S5))
    z = z * jax.nn.sigmoid(z @ w_glu)
    z = z * jax.nn.silu(gate.astype(jnp.float32))
    return z @ w_out


def wkv7_scan(r, w, kk, a, k, v, reverse):
    bsz = r.shape[0]
    xs = tuple(jnp.moveaxis(t, 1, 0) for t in (r, w, kk, a, k, v))

    def step(s, inp):
        r_t, w_t, kk_t, a_t, k_t, v_t = inp
        sa = jnp.einsum('bhvk,bhk->bhv', s, -kk_t)
        s = (s * w_t[:, :, None, :] + sa[..., None] * (kk_t * a_t)[:, :, None, :]
             + v_t[..., None] * k_t[:, :, None, :])
        return s, jnp.einsum('bhvk,bhk->bhv', s, r_t)

    s0 = jnp.zeros((bsz, RWKV_HEADS, RWKV_HEAD, RWKV_HEAD), jnp.float32)
    _, ys = lax.scan(step, s0, xs, reverse=reverse)
    return jnp.moveaxis(ys, 0, 1)


def rwkv7_branch(rw_in, gate, mu, w0, w_up, a0, a_up, k_k, k_a, r_k, ln_w, ln_b, w_out):
    bsz, seq, _ = rw_in.shape
    z = centred_shift_mix(rw_in.astype(jnp.float32), mu)
    r, k, v, xw_f, xw_b, xa_f, xa_b = jnp.split(z, RW_SPLITS, axis=-1)

    def heads(t):
        return t.reshape(bsz, seq, RWKV_HEADS, RWKV_HEAD)

    kk = heads(k * k_k)
    kk = kk * lax.rsqrt(jnp.sum(kk * kk, axis=-1, keepdims=True) + L2_EPS)
    rh, vh = heads(r), heads(v)
    bonus = jnp.sum(rh * heads(k) * r_k, axis=-1, keepdims=True) * vh
    y = jnp.zeros_like(vh)
    for dirn, (xw, xa, rev) in enumerate(((xw_f, xa_f, False), (xw_b, xa_b, True))):
        logw = -jax.nn.softplus(-(w0[dirn] + jnp.tanh(xw) @ w_up[dirn])) - 0.5
        decay = jnp.exp(-jnp.exp(logw))
        a = jax.nn.sigmoid(a0[dirn] + xa @ a_up[dirn])
        kd = k * (1.0 + (a - 1.0) * k_a)
        y = y + wkv7_scan(rh, heads(decay), kk, heads(a), heads(kd), vh, rev)
    mean = jnp.mean(y, axis=-1, keepdims=True)
    var = jnp.mean(jnp.square(y - mean), axis=-1, keepdims=True)
    y = (y - mean) * lax.rsqrt(var + GN_EPS)
    y = y.reshape(bsz, seq, D_RWKV) * ln_w + ln_b + bonus.reshape(bsz, seq, D_RWKV)
    y = y * jax.nn.silu(gate.astype(jnp.float32))
    return y @ w_out


def hybrid_layer(x, pre_norm_g, post_norm_g, w_in, s5_lam_re, s5_lam_im, s5_log_dt, s5_b_re, s5_b_im,
                 s5_c_re, s5_c_im, s5_d, s5_w_glu, s5_w_out, rw_mu, rw_w0, rw_w_up, rw_a0, rw_a_up,
                 rw_k_k, rw_k_a, rw_r_k, rw_ln_w, rw_ln_b, rw_w_out, w_o):
    hn = rms_norm(x, pre_norm_g).astype(x.dtype)
    proj = hn @ w_in
    s5_u, s5_gate, rw_in, rw_gate, merge_logits = jnp.split(proj, IN_SPLITS, axis=-1)
    p_s5 = s5_branch(s5_u, s5_gate, s5_lam_re, s5_lam_im, s5_log_dt, s5_b_re, s5_b_im,
                     s5_c_re, s5_c_im, s5_d, s5_w_glu, s5_w_out)
    p_rw = rwkv7_branch(rw_in, rw_gate, rw_mu, rw_w0, rw_w_up, rw_a0, rw_a_up, rw_k_k, rw_k_a,
                        rw_r_k, rw_ln_w, rw_ln_b, rw_w_out)
    g_s5, g_rw = jnp.split(merge_logits.astype(jnp.float32), 2, axis=-1)
    h = jax.nn.sigmoid(g_s5) * p_s5 + jax.nn.sigmoid(g_rw) * p_rw
    out = h.astype(x.dtype) @ w_o
    return (x.astype(jnp.float32) + rms_norm(out, post_norm_g)).astype(x.dtype)


def encoder(x, params):
    for layer_idx in range(DEPTH):
        x = hybrid_layer(x, *[p[layer_idx] for p in params])
    return x


def setup_inputs(seed: int = 0) -> dict:
    key = jax.random.key(seed)
    ks = jax.random.split(key, 27)
    f32 = jnp.float32

    def nrm(k, shape, scale):
        return scale * jax.random.normal(k, shape, f32)

    s5_lam_shape = (DEPTH, 2, S5_GROUPS, S5_STATE)
    n_idx = jnp.arange(S5_STATE, dtype=f32)
    return {
        'x_prompt': nrm(ks[0], (BATCH, SEQ, D_MODEL), 1.0),
        'x_sample': nrm(ks[1], (DEC_BATCH, DEC_SEQ, D_MODEL), 1.0),
        'pre_norm_g': 1.0 + nrm(ks[2], (DEPTH, D_MODEL), 0.02),
        'post_norm_g': 1.0 + nrm(ks[3], (DEPTH, D_MODEL), 0.02),
        'w_in': nrm(ks[4], (DEPTH, D_MODEL, N_IN), D_MODEL ** -0.5),
        's5_lam_re': -0.5 + nrm(ks[5], s5_lam_shape, 0.01),
        's5_lam_im': math.pi * n_idx + nrm(ks[6], s5_lam_shape, 0.01),
        's5_log_dt': jax.random.uniform(ks[7], (DEPTH, 2, S5_GROUPS), f32, math.log(DT_MIN), math.log(DT_MAX)),
        's5_b_re': nrm(ks[8], (DEPTH, 2, S5_GROUPS, S5_STATE, S5_GROUP), (2 * S5_GROUP) ** -0.5),
        's5_b_im': nrm(ks[9], (DEPTH, 2, S5_GROUPS, S5_STATE, S5_GROUP), (2 * S5_GROUP) ** -0.5),
        's5_c_re': nrm(ks[10], (DEPTH, 2, S5_GROUPS, S5_GROUP, S5_STATE), S5_STATE ** -0.5),
        's5_c_im': nrm(ks[11], (DEPTH, 2, S5_GROUPS, S5_GROUP, S5_STATE), S5_STATE ** -0.5),
        's5_d': nrm(ks[12], (DEPTH, D_S5), 1.0),
        's5_w_glu': nrm(ks[13], (DEPTH, D_S5, D_S5), D_S5 ** -0.5),
        's5_w_out': nrm(ks[14], (DEPTH, D_S5, D_MODEL), D_S5 ** -0.5),
        'rw_mu': jax.random.uniform(ks[15], (DEPTH, D_RW_IN), f32),
        'rw_w0': jax.random.uniform(ks[16], (DEPTH, 2, D_RWKV), f32, -6.0, -1.0),
        'rw_w_up': nrm(ks[17], (DEPTH, 2, LORA_RANK, D_RWKV), 0.5 * LORA_RANK ** -0.5),
        'rw_a0': nrm(ks[18], (DEPTH, 2, D_RWKV), 0.5),
        'rw_a_up': nrm(ks[19], (DEPTH, 2, LORA_RANK, D_RWKV), 0.5 * LORA_RANK ** -0.5),
        'rw_k_k': 0.85 + nrm(ks[20], (DEPTH, D_RWKV), 0.05),
        'rw_k_a': 1.0 + nrm(ks[21], (DEPTH, D_RWKV), 0.05),
        'rw_r_k': nrm(ks[22], (DEPTH, RWKV_HEADS, RWKV_HEAD), 0.1),
        'rw_ln_w': 1.0 + nrm(ks[23], (DEPTH, D_RWKV), 0.02),
        'rw_ln_b': nrm(ks[24], (DEPTH, D_RWKV), 0.02),
        'rw_w_out': nrm(ks[25], (DEPTH, D_RWKV, D_MODEL), D_RWKV ** -0.5),
        'w_o': nrm(ks[26], (DEPTH, D_MODEL, D_MODEL), D_MODEL ** -0.5),
    }


def reference(x_prompt, x_sample, pre_norm_g, post_norm_g, w_in, s5_lam_re, s5_lam_im, s5_log_dt,
              s5_b_re, s5_b_im, s5_c_re, s5_c_im, s5_d, s5_w_glu, s5_w_out, rw_mu, rw_w0, rw_w_up,
              rw_a0, rw_a_up, rw_k_k, rw_k_a, rw_r_k, rw_ln_w, rw_ln_b, rw_w_out, w_o):
    params = (pre_norm_g, post_norm_g, w_in, s5_lam_re, s5_lam_im, s5_log_dt, s5_b_re, s5_b_im,
              s5_c_re, s5_c_im, s5_d, s5_w_glu, s5_w_out, rw_mu, rw_w0, rw_w_up, rw_a0, rw_a_up,
              rw_k_k, rw_k_a, rw_r_k, rw_ln_w, rw_ln_b, rw_w_out, w_o)
    y_prompt = encoder(x_prompt, params)
    y_sample = encoder(x_sample, params)
    return (y_prompt, y_sample)
```

```python
import functools
import math

import jax
import jax.numpy as jnp
from jax import lax
from jax.experimental import pallas as pl
from jax.experimental.pallas import tpu as pltpu

F32 = jnp.float32
BF16 = jnp.bfloat16

D_MODEL = 2048
D_S5 = 1024
S5_GROUP = 16
S5_GROUPS = 64
S5_STATE = 64
D_RWKV = 1024
RWKV_HEAD = 64
RWKV_HEADS = 16
LORA_RANK = 64
D_RW_IN = 3 * D_RWKV + 4 * LORA_RANK
RMS_EPS = 1e-6
GN_EPS = 64e-5
L2_EPS = 1e-12

S5_CHUNK = 16
S5_PAIR = 2 * S5_CHUNK * S5_GROUP
WKV_CHUNK = 64
HEAD_PAIR = 2 * RWKV_HEAD
NEUMANN_STEPS = 5

VMEM_LIMIT = 48 * 1024 * 1024


def _cparams(sem):
    return pltpu.CompilerParams(dimension_semantics=sem, vmem_limit_bytes=VMEM_LIMIT)


def _dot(a, b):
    return jnp.dot(a, b, preferred_element_type=F32)


def _dot_nt(a, b):
    return lax.dot_general(a, b, (((1,), (1,)), ((), ())), preferred_element_type=F32)


def _dot_tn(a, b):
    return lax.dot_general(a, b, (((0,), (0,)), ((), ())), preferred_element_type=F32)


def _dot_split(a, b_bf16):
    hi = a.astype(BF16)
    lo = (a - hi.astype(F32)).astype(BF16)
    return _dot(hi, b_bf16) + _dot(lo, b_bf16)


def _sigmoid(x):
    return 1.0 / (1.0 + jnp.exp(-x))


def _silu(x):
    return x * _sigmoid(x)


def _gelu_tanh(x):
    c = math.sqrt(2.0 / math.pi)
    return 0.5 * x * (1.0 + jnp.tanh(c * (x + 0.044715 * (x * x * x))))


def _rms_cast_kernel(x_ref, g_ref, o_ref):
    x = x_ref[...]
    ms = jnp.mean(x * x, axis=-1, keepdims=True)
    o_ref[...] = (x * lax.rsqrt(ms + RMS_EPS) * g_ref[...]).astype(BF16)


def _rms_cast(x, g, tm):
    n, d = x.shape
    return pl.pallas_call(
        _rms_cast_kernel,
        out_shape=jax.ShapeDtypeStruct((n, d), BF16),
        grid=(n // tm,),
        in_specs=[pl.BlockSpec((tm, d), lambda i: (i, 0)),
                  pl.BlockSpec((1, d), lambda i: (0, 0))],
        out_specs=pl.BlockSpec((tm, d), lambda i: (i, 0)),
        compiler_params=_cparams(("parallel",)),
        name="rms_cast",
    )(x, g)


def _mm_kernel(a_ref, w_ref, o_ref):
    o_ref[...] = _dot(a_ref[...], w_ref[...]).astype(o_ref.dtype)


def _matmul(a, w, tm, tn, out_dtype=F32):
    n, k = a.shape
    m = w.shape[1]
    return pl.pallas_call(
        _mm_kernel,
        out_shape=jax.ShapeDtypeStruct((n, m), out_dtype),
        grid=(n // tm, m // tn),
        in_specs=[pl.BlockSpec((tm, k), lambda i, j: (i, 0)),
                  pl.BlockSpec((k, tn), lambda i, j: (0, j))],
        out_specs=pl.BlockSpec((tm, tn), lambda i, j: (i, j)),
        compiler_params=_cparams(("parallel", "arbitrary")),
        name="matmul",
    )(a, w)


def _s5_weights(lam_re, lam_im, log_dt, b_re, b_im, c_re, c_im, d):
    t = S5_CHUNK
    g, p, h = S5_GROUPS, S5_STATE, S5_GROUP
    hp = lax.Precision.HIGHEST
    lam = lax.complex(lam_re.astype(F32), lam_im.astype(F32))
    dt = jnp.exp(log_dt.astype(F32))[..., None]
    lam_dt = lam * dt
    lam_bar = jnp.exp(lam_dt)
    bbar = ((lam_bar - 1.0) / lam)[..., None] * lax.complex(b_re.astype(F32), b_im.astype(F32))
    c = lax.complex(c_re.astype(F32), c_im.astype(F32))
    steps = jnp.arange(t + 1, dtype=F32)
    pw = jnp.exp(lam_dt[None] * steps[:, None, None, None])

    kern = jnp.einsum('dghp,ldgp,dgpj->dglhj', c, pw[:t], bbar, precision=hp).real
    tt = jnp.arange(t)
    lag = tt[None, :] - tt[:, None]
    kf = kern[0][:, jnp.abs(lag)]
    kb = kern[1][:, jnp.abs(lag)]
    k0 = kern[0][:, 0] + kern[1][:, 0] + jnp.eye(h, dtype=F32) * d.astype(F32).reshape(g, h)[:, :, None]
    lag5 = lag[None, :, :, None, None]
    m = jnp.where(lag5 > 0, kf, jnp.where(lag5 < 0, kb, k0[:, None, None]))
    m = m.transpose(0, 1, 4, 2, 3).reshape(g, t * h, t * h)

    pf = pw[t - 1 - tt, 0][..., None] * bbar[0][None]
    pb = pw[tt, 1][..., None] * bbar[1][None]

    def _p_mat(x):
        return x.transpose(1, 0, 3, 2).reshape(g, t * h, p)

    p_parts = [_p_mat(pf.real), _p_mat(pf.imag), _p_mat(pb.real), _p_mat(pb.imag)]

    cf = c[0][None] * pw[tt + 1, 0][:, :, None, :]
    cb = c[1][None] * pw[t - tt, 1][:, :, None, :]

    def _q_mat(x):
        return x.transpose(1, 3, 0, 2).reshape(g, p, t * h)

    q_parts = [_q_mat(cf.real), _q_mat(-cf.imag), _q_mat(cb.real), _q_mat(-cb.imag)]

    gp = g // 2
    eye2 = jnp.eye(2, dtype=F32)

    def _pair(x, r, cdim):
        x = x.reshape(gp, 2, r, cdim)
        return jnp.einsum('gqrc,qs->gqrsc', x, eye2).reshape(gp, 2 * r, 2 * cdim)

    w1 = jnp.concatenate([_pair(m, t * h, t * h)] + [_pair(x, t * h, p) for x in p_parts], axis=2)
    qmat = jnp.concatenate([_pair(x, p, t * h) for x in q_parts], axis=1)
    at = pw[t]
    a_rows = jnp.stack([at[0].real, at[0].imag, at[1].real, at[1].imag]).reshape(4, g * p)
    return w1.astype(BF16), qmat.astype(BF16), a_rows


def _s5_chunk_kernel(x_ref, w_ref, y_ref, pfr_ref, pfi_ref, pbr_ref, pbi_ref):
    res = _dot(x_ref[...], w_ref[0])
    y_ref[...] = res[:, :S5_PAIR]
    for k, ref in enumerate((pfr_ref, pfi_ref, pbr_ref, pbi_ref)):
        ref[...] = res[:, S5_PAIR + 128 * k:S5_PAIR + 128 * (k + 1)]


def _s5_chunk(x2, w1, rb):
    r = x2.shape[0]
    gp = S5_GROUPS // 2
    st = jax.ShapeDtypeStruct((r, gp * 128), F32)
    pspec = pl.BlockSpec((rb, 128), lambda g, i: (i, g))
    return pl.pallas_call(
        _s5_chunk_kernel,
        out_shape=(jax.ShapeDtypeStruct((r, gp * S5_PAIR), F32), st, st, st, st),
        grid=(gp, r // rb),
        in_specs=[pl.BlockSpec((rb, S5_PAIR), lambda g, i: (i, g)),
                  pl.BlockSpec((1, S5_PAIR, 2 * S5_PAIR), lambda g, i: (g, 0, 0))],
        out_specs=(pl.BlockSpec((rb, S5_PAIR), lambda g, i: (i, g)), pspec, pspec, pspec, pspec),
        compiler_params=_cparams(("parallel", "arbitrary")),
        name="s5_chunk",
    )(x2, w1)


def _s5_scan_kernel(a_ref, pfr_ref, pfi_ref, pbr_ref, pbi_ref,
                    hfr_ref, hfi_ref, hbr_ref, hbi_ref, carry_ref, *, rows):
    @pl.when(pl.program_id(2) == 0)
    def _():
        carry_ref[...] = jnp.zeros_like(carry_ref)

    afr, afi = a_ref[0:1, :], a_ref[1:2, :]
    abr, abi = a_ref[2:3, :], a_ref[3:4, :]

    def tile(i, carry):
        hfr, hfi, hbr, hbi = carry
        r0 = pl.multiple_of(i * 8, 8)
        pfr, pfi = pfr_ref[pl.ds(r0, 8), :], pfi_ref[pl.ds(r0, 8), :]
        rb0 = pl.multiple_of(rows - 8 - i * 8, 8)
        pbr, pbi = pbr_ref[pl.ds(rb0, 8), :], pbi_ref[pl.ds(rb0, 8), :]
        of_r, of_i, ob_r, ob_i = [], [], [None] * 8, [None] * 8
        for s in range(8):
            of_r.append(hfr)
            of_i.append(hfi)
            hfr, hfi = (afr * hfr - afi * hfi + pfr[s:s + 1, :],
                        afr * hfi + afi * hfr + pfi[s:s + 1, :])
            sb = 7 - s
            ob_r[sb] = hbr
            ob_i[sb] = hbi
            hbr, hbi = (abr * hbr - abi * hbi + pbr[sb:sb + 1, :],
                        abr * hbi + abi * hbr + pbi[sb:sb + 1, :])
        hfr_ref[pl.ds(r0, 8), :] = jnp.concatenate(of_r, axis=0)
        hfi_ref[pl.ds(r0, 8), :] = jnp.concatenate(of_i, axis=0)
        hbr_ref[pl.ds(rb0, 8), :] = jnp.concatenate(ob_r, axis=0)
        hbi_ref[pl.ds(rb0, 8), :] = jnp.concatenate(ob_i, axis=0)
        return hfr, hfi, hbr, hbi

    init = tuple(carry_ref[k:k + 1, :] for k in range(4))
    out = lax.fori_loop(0, rows // 8, tile, init)
    for k in range(4):
        carry_ref[k:k + 1, :] = out[k]


def _s5_scan(a_rows, pfr, pfi, pbr, pbi, nseq, cb, lt):
    r, w = pfr.shape
    nblk = r // nseq // cb
    fspec = pl.BlockSpec((cb, lt), lambda b, j, i: (b * nblk + i, j))
    bspec = pl.BlockSpec((cb, lt), lambda b, j, i: (b * nblk + nblk - 1 - i, j))
    st = jax.ShapeDtypeStruct((r, w), F32)
    return pl.pallas_call(
        functools.partial(_s5_scan_kernel, rows=cb),
        out_shape=(st, st, st, st),
        grid=(nseq, w // lt, nblk),
        in_specs=[pl.BlockSpec((4, lt), lambda b, j, i: (0, j)), fspec, fspec, bspec, bspec],
        out_specs=(fspec, fspec, bspec, bspec),
        scratch_shapes=[pltpu.VMEM((4, lt), F32)],
        compiler_params=_cparams(("parallel", "parallel", "arbitrary")),
        name="s5_scan",
    )(a_rows, pfr, pfi, pbr, pbi)


def _s5_state_out_kernel(y0_ref, hfr_ref, hfi_ref, hbr_ref, hbi_ref, q_ref, y_ref):
    h = jnp.concatenate([hfr_ref[...], hfi_ref[...], hbr_ref[...], hbi_ref[...]], axis=1)
    y_ref[...] = y0_ref[...] + _dot(h.astype(BF16), q_ref[0])


def _s5_state_out(y0, hfr, hfi, hbr, hbi, qmat, rb):
    r = y0.shape[0]
    gp = S5_GROUPS // 2
    hspec = pl.BlockSpec((rb, 128), lambda g, i: (i, g))
    yspec = pl.BlockSpec((rb, S5_PAIR), lambda g, i: (i, g))
    return pl.pallas_call(
        _s5_state_out_kernel,
        out_shape=jax.ShapeDtypeStruct(y0.shape, F32),
        grid=(gp, r // rb),
        in_specs=[yspec, hspec, hspec, hspec, hspec,
                  pl.BlockSpec((1, S5_PAIR, S5_PAIR), lambda g, i: (g, 0, 0))],
        out_specs=yspec,
        compiler_params=_cparams(("parallel", "arbitrary")),
        name="s5_state_out",
    )(y0, hfr, hfi, hbr, hbi, qmat)


def _s5_post_kernel(y_ref, gate_ref, mg_ref, wglu_ref, wout_ref, o_ref):
    z = _gelu_tanh(y_ref[...])
    z = z * _sigmoid(_dot(z.astype(BF16), wglu_ref[...]))
    z = z * _silu(gate_ref[...])
    p = _dot(z.astype(BF16), wout_ref[...])
    o_ref[...] = _sigmoid(mg_ref[...]) * p


def _s5_post(y, gate, merge, w_glu, w_out, tm):
    n = y.shape[0]
    return pl.pallas_call(
        _s5_post_kernel,
        out_shape=jax.ShapeDtypeStruct((n, D_MODEL), F32),
        grid=(n // tm,),
        in_specs=[pl.BlockSpec((tm, D_S5), lambda i: (i, 0)),
                  pl.BlockSpec((tm, D_S5), lambda i: (i, 0)),
                  pl.BlockSpec((tm, D_MODEL), lambda i: (i, 0)),
                  pl.BlockSpec((D_S5, D_S5), lambda i: (0, 0)),
                  pl.BlockSpec((D_S5, D_MODEL), lambda i: (0, 0))],
        out_specs=pl.BlockSpec((tm, D_MODEL), lambda i: (i, 0)),
        compiler_params=_cparams(("parallel",)),
        name="s5_post",
    )(y, gate, merge, w_glu, w_out)


def _head_indicator():
    lane = lax.broadcasted_iota(jnp.int32, (D_RWKV, 128), 0) // RWKV_HEAD
    col = lax.broadcasted_iota(jnp.int32, (D_RWKV, 128), 1)
    return (lane == col).astype(BF16)


def _head_sum_bcast(x, e, et):
    return _dot_split(_dot_split(x, e), et)


def _rw_pre_kernel(x_ref, prev_ref, next_ref, mu_ref, wup_ref, aup_ref, w0_ref, a0_ref,
                   kk_ref_p, ka_ref_p, rk_ref_p, e_ref, et_ref,
                   r_ref, kk_ref, v_ref, kdf_ref, kdb_ref, bf_ref, bb_ref, lwf_ref, lwb_ref,
                   bonus_ref, *, tm, seq):
    i = pl.program_id(0)
    x = x_ref[...]
    first = (i * tm) % seq == 0
    last = ((i + 1) * tm) % seq == 0
    prev_row = jnp.where(first, 0.0, prev_ref[7:8, :])
    next_row = jnp.where(last, 0.0, next_ref[0:1, :])
    row = lax.broadcasted_iota(jnp.int32, x.shape, 0)
    prev = jnp.where(row == 0, prev_row, pltpu.roll(x, 1, 0))
    nxt = jnp.where(row == tm - 1, next_row, pltpu.roll(x, tm - 1, 0))
    z = x + (0.5 * (prev + nxt) - x) * mu_ref[...]

    r = z[:, :D_RWKV]
    k = z[:, D_RWKV:2 * D_RWKV]
    v = z[:, 2 * D_RWKV:3 * D_RWKV]
    xw = z[:, 3 * D_RWKV:3 * D_RWKV + 2 * LORA_RANK]
    xa = z[:, 3 * D_RWKV + 2 * LORA_RANK:]
    e, et = e_ref[...], et_ref[...]

    kk = k * kk_ref_p[...]
    kk = kk * lax.rsqrt(_head_sum_bcast(kk * kk, e, et) + L2_EPS)
    bonus_ref[...] = _head_sum_bcast(r * k * rk_ref_p[...], e, et) * v
    r_ref[...] = r
    kk_ref[...] = kk
    v_ref[...] = v

    wpre = _dot(jnp.tanh(xw).astype(BF16), wup_ref[...]) + w0_ref[...]
    apre = _dot(xa.astype(BF16), aup_ref[...]) + a0_ref[...]
    lw = -math.exp(-0.5) * _sigmoid(wpre)
    a = _sigmoid(apre)
    ka = ka_ref_p[...]
    lwf_ref[...] = lw[:, :D_RWKV]
    lwb_ref[...] = lw[:, D_RWKV:]
    af, ab = a[:, :D_RWKV], a[:, D_RWKV:]
    kdf_ref[...] = k * (1.0 + (af - 1.0) * ka)
    kdb_ref[...] = k * (1.0 + (ab - 1.0) * ka)
    bf_ref[...] = kk * af
    bb_ref[...] = kk * ab


def _rw_pre(rw_in, mu, wup, aup, w0, a0, k_k, k_a, r_k, e, et, tm, seq):
    n, w = rw_in.shape
    nb8 = n // 8
    t8 = tm // 8
    row = lambda i: (i, 0)
    const = lambda i: (0, 0)
    ospec = pl.BlockSpec((tm, D_RWKV), row)
    st = jax.ShapeDtypeStruct((n, D_RWKV), F32)
    return pl.pallas_call(
        functools.partial(_rw_pre_kernel, tm=tm, seq=seq),
        out_shape=(st,) * 10,
        grid=(n // tm,),
        in_specs=[pl.BlockSpec((tm, w), row),
                  pl.BlockSpec((8, w), lambda i: (jnp.maximum(i * t8 - 1, 0), 0)),
                  pl.BlockSpec((8, w), lambda i: (jnp.minimum((i + 1) * t8, nb8 - 1), 0)),
                  pl.BlockSpec((1, w), const),
                  pl.BlockSpec((2 * LORA_RANK, 2 * D_RWKV), const),
                  pl.BlockSpec((2 * LORA_RANK, 2 * D_RWKV), const),
                  pl.BlockSpec((1, 2 * D_RWKV), const),
                  pl.BlockSpec((1, 2 * D_RWKV), const),
                  pl.BlockSpec((1, D_RWKV), const),
                  pl.BlockSpec((1, D_RWKV), const),
                  pl.BlockSpec((1, D_RWKV), const),
                  pl.BlockSpec((D_RWKV, 128), const),
                  pl.BlockSpec((128, D_RWKV), const)],
        out_specs=(ospec,) * 10,
        compiler_params=_cparams(("parallel",)),
        name="rw_pre",
    )(rw_in, rw_in, rw_in, mu, wup, aup, w0, a0, k_k, k_a, r_k, e, et)


def _wkv_kernel(r_ref, kk_ref, v_ref, kd_ref, b_ref, lw_ref, y_ref, s_ref, *, rev):
    t = WKV_CHUNK
    hd = RWKV_HEAD

    @pl.when(pl.program_id(1) == 0)
    def _():
        s_ref[...] = jnp.zeros_like(s_ref)

    rr = lax.broadcasted_iota(jnp.int32, (t, t), 0)
    cc = lax.broadcasted_iota(jnp.int32, (t, t), 1)
    if rev:
        incl, strict = rr <= cc, rr < cc
    else:
        incl, strict = rr >= cc, rr > cc
    tri = incl.astype(BF16)
    eye = rr == cc
    eye_f = eye.astype(F32)
    t_last = 0 if rev else t - 1

    def pair(hp, carry):
        off = pl.multiple_of(hp * HEAD_PAIR, HEAD_PAIR)
        sl = pl.ds(off, HEAD_PAIR)
        r, kk, v = r_ref[:, sl], kk_ref[:, sl], v_ref[:, sl]
        kd, beta, lw = kd_ref[:, sl], b_ref[:, sl], lw_ref[:, sl]
        lw_hi = lw.astype(BF16)
        lw_lo = (lw - lw_hi.astype(F32)).astype(BF16)
        lcum = _dot(tri, lw_hi) + _dot(tri, lw_lo)
        lcum_x = lcum - lw
        cref = lcum[t // 2:t // 2 + 1, :]
        ltot = lcum[t_last:t_last + 1, :]
        e1 = jnp.exp(lcum - cref)
        e1x = jnp.exp(lcum_x - cref)
        e2 = jnp.exp(cref - lcum)
        ec = jnp.exp(cref)
        ewt = jnp.exp(ltot - cref)
        wtot = jnp.exp(ltot)
        r_t = r * e1
        a_t = -kk * e1x
        b_t = beta * e2
        k_t = kd * e2
        a_0 = a_t * ec
        r_0 = r_t * ec
        b_h = b_t * ewt
        k_h = k_t * ewt
        ys = []
        for q in range(2):
            hs = slice(q * hd, (q + 1) * hd)
            lm = jnp.concatenate([a_t[:, hs], r_t[:, hs]], axis=0).astype(BF16)
            rm = jnp.concatenate([b_t[:, hs], k_t[:, hs]], axis=0).astype(BF16)
            gm = _dot_nt(lm, rm)
            a_ab = jnp.where(strict, gm[:t, :t], 0.0)
            a_ak = jnp.where(strict, gm[:t, t:], 0.0)
            a_rb = jnp.where(incl, gm[t:, :t], 0.0)
            a_rk = jnp.where(incl, gm[t:, t:], 0.0)
            inv = eye_f + a_ab
            n_b = a_ab.astype(BF16)
            nk = _dot(n_b, n_b)
            for s in range(1, NEUMANN_STEPS + 1):
                nk_b = nk.astype(BF16)
                if s < NEUMANN_STEPS:
                    nx = _dot(nk_b, jnp.concatenate([nk, inv], axis=1).astype(BF16))
                    nk = nx[:, :t]
                    inv = inv + nx[:, t:]
                else:
                    inv = inv + _dot(nk_b, inv.astype(BF16))
            vh = v[:, hs].astype(BF16)
            av = _dot(jnp.concatenate([a_ak, a_rk], axis=0).astype(BF16), vh)
            pp = _dot(inv.astype(BF16),
                      jnp.concatenate([a_0[:, hs], av[:t]], axis=1).astype(BF16))
            pp_b = pp.astype(BF16)
            qq = _dot(a_rb.astype(BF16), pp_b) + jnp.concatenate([r_0[:, hs], av[t:]], axis=1)
            mn = _dot_tn(b_h[:, hs].astype(BF16), pp_b)
            nn = mn[:, hd:] + _dot_tn(k_h[:, hs].astype(BF16), vh)
            st = s_ref[2 * hp + q]
            st_b = st.astype(BF16)
            ys.append(_dot(qq[:, :hd].astype(BF16), st_b) + qq[:, hd:])
            wcol = jnp.sum(jnp.where(eye, jnp.broadcast_to(wtot[:, hs], (hd, hd)), 0.0),
                           axis=1, keepdims=True)
            s_ref[2 * hp + q] = wcol * st + _dot(mn[:, :hd].astype(BF16), st_b) + nn
        y_ref[:, sl] = jnp.concatenate(ys, axis=1)
        return carry

    lax.fori_loop(0, RWKV_HEADS // 2, pair, 0)


def _wkv(r, kk, v, kd, beta, lw, nseq, rev):
    n = r.shape[0]
    t = WKV_CHUNK
    nc = n // nseq // t
    if rev:
        idx = lambda b, c: (b * nc + nc - 1 - c, 0)
    else:
        idx = lambda b, c: (b * nc + c, 0)
    spec = pl.BlockSpec((t, D_RWKV), idx)
    return pl.pallas_call(
        functools.partial(_wkv_kernel, rev=rev),
        out_shape=jax.ShapeDtypeStruct((n, D_RWKV), F32),
        grid=(nseq, nc),
        in_specs=[spec] * 6,
        out_specs=spec,
        scratch_shapes=[pltpu.VMEM((RWKV_HEADS, RWKV_HEAD, RWKV_HEAD), F32)],
        compiler_params=_cparams(("parallel", "arbitrary")),
        name="wkv_bwd" if rev else "wkv_fwd",
    )(r, kk, v, kd, beta, lw)


def _rw_post_kernel(yf_ref, yb_ref, bonus_ref, gate_ref, mg_ref, lnw_ref, lnb_ref, e_ref, et_ref,
                    wout_ref, o_ref):
    e, et = e_ref[...], et_ref[...]
    y = yf_ref[...] + yb_ref[...]
    inv_n = 1.0 / RWKV_HEAD
    mean = _head_sum_bcast(y, e, et) * inv_n
    yc = y - mean
    var = _head_sum_bcast(yc * yc, e, et) * inv_n
    y = yc * lax.rsqrt(var + GN_EPS) * lnw_ref[...] + lnb_ref[...] + bonus_ref[...]
    y = y * _silu(gate_ref[...])
    p = _dot(y.astype(BF16), wout_ref[...])
    o_ref[...] = _sigmoid(mg_ref[...]) * p


def _rw_post(yf, yb, bonus, gate, merge, ln_w, ln_b, e, et, w_out, tm):
    n = yf.shape[0]
    row = lambda i: (i, 0)
    const = lambda i: (0, 0)
    tspec = pl.BlockSpec((tm, D_RWKV), row)
    return pl.pallas_call(
        _rw_post_kernel,
        out_shape=jax.ShapeDtypeStruct((n, D_MODEL), F32),
        grid=(n // tm,),
        in_specs=[tspec, tspec, tspec, tspec,
                  pl.BlockSpec((tm, D_MODEL), lambda i: (i, 1)),
                  pl.BlockSpec((1, D_RWKV), const), pl.BlockSpec((1, D_RWKV), const),
                  pl.BlockSpec((D_RWKV, 128), const), pl.BlockSpec((128, D_RWKV), const),
                  pl.BlockSpec((D_RWKV, D_MODEL), const)],
        out_specs=pl.BlockSpec((tm, D_MODEL), row),
        compiler_params=_cparams(("parallel",)),
        name="rw_post",
    )(yf, yb, bonus, gate, merge, ln_w, ln_b, e, et, w_out)


def _final_kernel(hs_ref, hr_ref, x_ref, wo_ref, g_ref, o_ref):
    h = (hs_ref[...] + hr_ref[...]).astype(BF16)
    out = _dot(h, wo_ref[...])
    ms = jnp.mean(out * out, axis=-1, keepdims=True)
    o_ref[...] = x_ref[...] + out * lax.rsqrt(ms + RMS_EPS) * g_ref[...]


def _final(hs, hr, x, w_o, g, tm):
    n = x.shape[0]
    row = lambda i: (i, 0)
    const = lambda i: (0, 0)
    tspec = pl.BlockSpec((tm, D_MODEL), row)
    return pl.pallas_call(
        _final_kernel,
        out_shape=jax.ShapeDtypeStruct((n, D_MODEL), F32),
        grid=(n // tm,),
        in_specs=[tspec, tspec, tspec, pl.BlockSpec((D_MODEL, D_MODEL), const),
                  pl.BlockSpec((1, D_MODEL), const)],
        out_specs=tspec,
        compiler_params=_cparams(("parallel",)),
        name="final",
    )(hs, hr, x, w_o, g)


def _pick(n, pref):
    while n % pref:
        pref //= 2
    return pref


def _prepare(pre_norm_g, post_norm_g, w_in, s5_lam_re, s5_lam_im, s5_log_dt, s5_b_re, s5_b_im,
             s5_c_re, s5_c_im, s5_d, s5_w_glu, s5_w_out, rw_mu, rw_w0, rw_w_up, rw_a0, rw_a_up,
             rw_k_k, rw_k_a, rw_r_k, rw_ln_w, rw_ln_b, rw_w_out, w_o):
    p = {}
    p['pre_g'] = pre_norm_g.astype(F32).reshape(1, D_MODEL)
    p['post_g'] = post_norm_g.astype(F32).reshape(1, D_MODEL)
    w = w_in.astype(BF16)
    o = 0
    for name, width in (('w_u', D_S5), ('w_sg', D_S5), ('w_rw', D_RW_IN), ('w_rg', D_RWKV),
                        ('w_mg', 2 * D_MODEL)):
        p[name] = w[:, o:o + width]
        o += width
    p['s5_w1'], p['s5_q'], p['s5_a'] = _s5_weights(
        s5_lam_re, s5_lam_im, s5_log_dt, s5_b_re, s5_b_im, s5_c_re, s5_c_im, s5_d)
    p['w_glu'] = s5_w_glu.astype(BF16)
    p['s5_w_out'] = s5_w_out.astype(BF16)
    p['mu'] = rw_mu.astype(F32).reshape(1, D_RW_IN)
    zeros = jnp.zeros((LORA_RANK, D_RWKV), F32)

    def blockdiag(u):
        return jnp.concatenate([jnp.concatenate([u[0], zeros], axis=1),
                                jnp.concatenate([zeros, u[1]], axis=1)], axis=0).astype(BF16)

    p['wup'] = blockdiag(rw_w_up.astype(F32))
    p['aup'] = blockdiag(rw_a_up.astype(F32))
    p['w0'] = rw_w0.astype(F32).reshape(1, 2 * D_RWKV)
    p['a0'] = rw_a0.astype(F32).reshape(1, 2 * D_RWKV)
    p['k_k'] = rw_k_k.astype(F32).reshape(1, D_RWKV)
    p['k_a'] = rw_k_a.astype(F32).reshape(1, D_RWKV)
    p['r_k'] = rw_r_k.astype(F32).reshape(1, D_RWKV)
    p['ln_w'] = rw_ln_w.astype(F32).reshape(1, D_RWKV)
    p['ln_b'] = rw_ln_b.astype(F32).reshape(1, D_RWKV)
    p['rw_w_out'] = rw_w_out.astype(BF16)
    p['w_o'] = w_o.astype(BF16)
    e = _head_indicator()
    p['e'] = e
    p['et'] = e.T
    return p


def _s5_mix(u, p, nseq):
    n = u.shape[0]
    t, g, h = S5_CHUNK, S5_GROUPS, S5_GROUP
    r = n // t
    x2 = u.reshape(r, t, g, h).transpose(0, 2, 1, 3).reshape(r, g * t * h)
    rows_per_seq = r // nseq
    rb = _pick(r, 512)
    y0, pfr, pfi, pbr, pbi = _s5_chunk(x2, p['s5_w1'], rb)
    cb = _pick(rows_per_seq, 128)
    hfr, hfi, hbr, hbi = _s5_scan(p['s5_a'], pfr, pfi, pbr, pbi, nseq, cb, 1024)
    y2 = _s5_state_out(y0, hfr, hfi, hbr, hbi, p['s5_q'], rb)
    return y2.reshape(r, g, t, h).transpose(0, 2, 1, 3).reshape(n, g * h)


def _layer(x, p):
    bsz, seq, _ = x.shape
    n = bsz * seq
    x2 = x.reshape(n, D_MODEL)
    tm = _pick(n, 1024)
    hn = _rms_cast(x2, p['pre_g'], _pick(n, 512))
    u = _matmul(hn, p['w_u'], tm, 512, BF16)
    s5_gate = _matmul(hn, p['w_sg'], tm, 512)
    rw_in = _matmul(hn, p['w_rw'], tm, 256)
    rw_gate = _matmul(hn, p['w_rg'], tm, 512)
    merge = _matmul(hn, p['w_mg'], tm, 512)

    y_s5 = _s5_mix(u, p, bsz)
    hs = _s5_post(y_s5, s5_gate, merge, p['w_glu'], p['s5_w_out'], _pick(n, 512))

    tr = _pick(seq, 256)
    (r, kk, v, kdf, kdb, bf, bb, lwf, lwb, bonus) = _rw_pre(
        rw_in, p['mu'], p['wup'], p['aup'], p['w0'], p['a0'], p['k_k'], p['k_a'], p['r_k'],
        p['e'], p['et'], tr, seq)
    yf = _wkv(r, kk, v, kdf, bf, lwf, bsz, False)
    yb = _wkv(r, kk, v, kdb, bb, lwb, bsz, True)
    hr = _rw_post(yf, yb, bonus, rw_gate, merge, p['ln_w'], p['ln_b'], p['e'], p['et'],
                  p['rw_w_out'], _pick(n, 512))

    out = _final(hs, hr, x2, p['w_o'], p['post_g'], _pick(n, 512))
    return out.reshape(bsz, seq, D_MODEL)


def kernel(x_prompt, x_sample, pre_norm_g, post_norm_g, w_in, s5_lam_re, s5_lam_im, s5_log_dt, s5_b_re, s5_b_im, s5_c_re, s5_c_im, s5_d, s5_w_glu, s5_w_out, rw_mu, rw_w0, rw_w_up, rw_a0, rw_a_up, rw_k_k, rw_k_a, rw_r_k, rw_ln_w, rw_ln_b, rw_w_out, w_o):
    params = (pre_norm_g, post_norm_g, w_in, s5_lam_re, s5_lam_im, s5_log_dt, s5_b_re, s5_b_im,
              s5_c_re, s5_c_im, s5_d, s5_w_glu, s5_w_out, rw_mu, rw_w0, rw_w_up, rw_a0, rw_a_up,
              rw_k_k, rw_k_a, rw_r_k, rw_ln_w, rw_ln_b, rw_w_out, w_o)
    y_prompt, y_sample = x_prompt, x_sample
    for layer in range(w_in.shape[0]):
        p = _prepare(*[w[layer] for w in params])
        y_prompt = _layer(y_prompt, p)
        y_sample = _layer(y_sample, p)
    return (y_prompt, y_sample)
```

```python
import functools
import math

import jax
import jax.numpy as jnp
from jax import lax
from jax.experimental import pallas as pl
from jax.experimental.pallas import tpu as pltpu

F32 = jnp.float32
BF16 = jnp.bfloat16

D_MODEL = 2048
D_S5 = 1024
S5_GROUP = 16
S5_GROUPS = 64
S5_STATE = 64
D_RWKV = 1024
RWKV_HEAD = 64
RWKV_HEADS = 16
LORA_RANK = 64
D_RW_IN = 3 * D_RWKV + 4 * LORA_RANK
RMS_EPS = 1e-6
GN_EPS = 64e-5
L2_EPS = 1e-12

S5_CHUNK = 16
S5_PAIR = 2 * S5_CHUNK * S5_GROUP
WKV_CHUNK = 64
HEAD_PAIR = 2 * RWKV_HEAD
NEUMANN_STEPS = 5
WKV_PAIRS_PER_STEP = 8

VMEM_LIMIT = 48 * 1024 * 1024


def _cparams(sem):
    return pltpu.CompilerParams(dimension_semantics=sem, vmem_limit_bytes=VMEM_LIMIT)


def _dot(a, b):
    return jnp.dot(a, b, preferred_element_type=F32)


def _dot_nt(a, b):
    return lax.dot_general(a, b, (((1,), (1,)), ((), ())), preferred_element_type=F32)


def _dot_tn(a, b):
    return lax.dot_general(a, b, (((0,), (0,)), ((), ())), preferred_element_type=F32)


def _dot_split(a, b_bf16):
    hi = a.astype(BF16)
    lo = (a - hi.astype(F32)).astype(BF16)
    return _dot(hi, b_bf16) + _dot(lo, b_bf16)


def _sigmoid(x):
    return 1.0 / (1.0 + jnp.exp(-x))


def _silu(x):
    return x * _sigmoid(x)


def _gelu_tanh(x):
    c = math.sqrt(2.0 / math.pi)
    return 0.5 * x * (1.0 + jnp.tanh(c * (x + 0.044715 * (x * x * x))))


def _rms_cast_kernel(x_ref, g_ref, o_ref):
    x = x_ref[...]
    ms = jnp.mean(x * x, axis=-1, keepdims=True)
    o_ref[...] = (x * lax.rsqrt(ms + RMS_EPS) * g_ref[...]).astype(BF16)


def _rms_cast(x, g, tm):
    n, d = x.shape
    return pl.pallas_call(
        _rms_cast_kernel,
        out_shape=jax.ShapeDtypeStruct((n, d), BF16),
        grid=(n // tm,),
        in_specs=[pl.BlockSpec((tm, d), lambda i: (i, 0)),
                  pl.BlockSpec((1, d), lambda i: (0, 0))],
        out_specs=pl.BlockSpec((tm, d), lambda i: (i, 0)),
        compiler_params=_cparams(("parallel",)),
        name="rms_cast",
    )(x, g)


def _mm_kernel(a_ref, w_ref, o_ref):
    o_ref[...] = _dot(a_ref[...], w_ref[...]).astype(o_ref.dtype)


def _matmul(a, w, tm, tn, out_dtype=F32):
    n, k = a.shape
    m = w.shape[1]
    return pl.pallas_call(
        _mm_kernel,
        out_shape=jax.ShapeDtypeStruct((n, m), out_dtype),
        grid=(n // tm, m // tn),
        in_specs=[pl.BlockSpec((tm, k), lambda i, j: (i, 0)),
                  pl.BlockSpec((k, tn), lambda i, j: (0, j))],
        out_specs=pl.BlockSpec((tm, tn), lambda i, j: (i, j)),
        compiler_params=_cparams(("parallel", "arbitrary")),
        name="matmul",
    )(a, w)


def _s5_weights(lam_re, lam_im, log_dt, b_re, b_im, c_re, c_im, d):
    t = S5_CHUNK
    g, p, h = S5_GROUPS, S5_STATE, S5_GROUP
    hp = lax.Precision.HIGHEST
    lam = lax.complex(lam_re.astype(F32), lam_im.astype(F32))
    dt = jnp.exp(log_dt.astype(F32))[..., None]
    lam_dt = lam * dt
    lam_bar = jnp.exp(lam_dt)
    bbar = ((lam_bar - 1.0) / lam)[..., None] * lax.complex(b_re.astype(F32), b_im.astype(F32))
    c = lax.complex(c_re.astype(F32), c_im.astype(F32))
    steps = jnp.arange(t + 1, dtype=F32)
    pw = jnp.exp(lam_dt[None] * steps[:, None, None, None])

    kern = jnp.einsum('dghp,ldgp,dgpj->dglhj', c, pw[:t], bbar, precision=hp).real
    tt = jnp.arange(t)
    lag = tt[None, :] - tt[:, None]
    kf = kern[0][:, jnp.abs(lag)]
    kb = kern[1][:, jnp.abs(lag)]
    k0 = kern[0][:, 0] + kern[1][:, 0] + jnp.eye(h, dtype=F32) * d.astype(F32).reshape(g, h)[:, :, None]
    lag5 = lag[None, :, :, None, None]
    m = jnp.where(lag5 > 0, kf, jnp.where(lag5 < 0, kb, k0[:, None, None]))
    m = m.transpose(0, 1, 4, 2, 3).reshape(g, t * h, t * h)

    pf = pw[t - 1 - tt, 0][..., None] * bbar[0][None]
    pb = pw[tt, 1][..., None] * bbar[1][None]

    def _p_mat(x):
        return x.transpose(1, 0, 3, 2).reshape(g, t * h, p)

    p_parts = [_p_mat(pf.real), _p_mat(pf.imag), _p_mat(pb.real), _p_mat(pb.imag)]

    cf = c[0][None] * pw[tt + 1, 0][:, :, None, :]
    cb = c[1][None] * pw[t - tt, 1][:, :, None, :]

    def _q_mat(x):
        return x.transpose(1, 3, 0, 2).reshape(g, p, t * h)

    q_parts = [_q_mat(cf.real), _q_mat(-cf.imag), _q_mat(cb.real), _q_mat(-cb.imag)]

    gp = g // 2
    eye2 = jnp.eye(2, dtype=F32)

    def _pair(x, r, cdim):
        x = x.reshape(gp, 2, r, cdim)
        return jnp.einsum('gqrc,qs->gqrsc', x, eye2).reshape(gp, 2 * r, 2 * cdim)

    w1 = jnp.concatenate([_pair(m, t * h, t * h)] + [_pair(x, t * h, p) for x in p_parts], axis=2)
    qmat = jnp.concatenate([_pair(x, p, t * h) for x in q_parts], axis=1)
    at = pw[t]
    a_rows = jnp.stack([at[0].real, at[0].imag, at[1].real, at[1].imag]).reshape(4, g * p)
    return w1.astype(BF16), qmat.astype(BF16), a_rows


def _s5_chunk_kernel(x_ref, w_ref, y_ref, pfr_ref, pfi_ref, pbr_ref, pbi_ref):
    res = _dot(x_ref[...], w_ref[0])
    y_ref[...] = res[:, :S5_PAIR]
    for k, ref in enumerate((pfr_ref, pfi_ref, pbr_ref, pbi_ref)):
        ref[...] = res[:, S5_PAIR + 128 * k:S5_PAIR + 128 * (k + 1)]


def _s5_chunk(x2, w1, rb):
    r = x2.shape[0]
    gp = S5_GROUPS // 2
    st = jax.ShapeDtypeStruct((r, gp * 128), F32)
    pspec = pl.BlockSpec((rb, 128), lambda g, i: (i, g))
    return pl.pallas_call(
        _s5_chunk_kernel,
        out_shape=(jax.ShapeDtypeStruct((r, gp * S5_PAIR), F32), st, st, st, st),
        grid=(gp, r // rb),
        in_specs=[pl.BlockSpec((rb, S5_PAIR), lambda g, i: (i, g)),
                  pl.BlockSpec((1, S5_PAIR, 2 * S5_PAIR), lambda g, i: (g, 0, 0))],
        out_specs=(pl.BlockSpec((rb, S5_PAIR), lambda g, i: (i, g)), pspec, pspec, pspec, pspec),
        compiler_params=_cparams(("parallel", "arbitrary")),
        name="s5_chunk",
    )(x2, w1)


def _s5_scan_kernel(a_ref, pfr_ref, pfi_ref, pbr_ref, pbi_ref,
                    hfr_ref, hfi_ref, hbr_ref, hbi_ref, carry_ref, *, rows):
    @pl.when(pl.program_id(2) == 0)
    def _():
        carry_ref[...] = jnp.zeros_like(carry_ref)

    afr, afi = a_ref[0:1, :], a_ref[1:2, :]
    abr, abi = a_ref[2:3, :], a_ref[3:4, :]

    def tile(i, carry):
        hfr, hfi, hbr, hbi = carry
        r0 = pl.multiple_of(i * 8, 8)
        pfr, pfi = pfr_ref[pl.ds(r0, 8), :], pfi_ref[pl.ds(r0, 8), :]
        rb0 = pl.multiple_of(rows - 8 - i * 8, 8)
        pbr, pbi = pbr_ref[pl.ds(rb0, 8), :], pbi_ref[pl.ds(rb0, 8), :]
        of_r, of_i, ob_r, ob_i = [], [], [None] * 8, [None] * 8
        for s in range(8):
            of_r.append(hfr)
            of_i.append(hfi)
            hfr, hfi = (afr * hfr - afi * hfi + pfr[s:s + 1, :],
                        afr * hfi + afi * hfr + pfi[s:s + 1, :])
            sb = 7 - s
            ob_r[sb] = hbr
            ob_i[sb] = hbi
            hbr, hbi = (abr * hbr - abi * hbi + pbr[sb:sb + 1, :],
                        abr * hbi + abi * hbr + pbi[sb:sb + 1, :])
        hfr_ref[pl.ds(r0, 8), :] = jnp.concatenate(of_r, axis=0)
        hfi_ref[pl.ds(r0, 8), :] = jnp.concatenate(of_i, axis=0)
        hbr_ref[pl.ds(rb0, 8), :] = jnp.concatenate(ob_r, axis=0)
        hbi_ref[pl.ds(rb0, 8), :] = jnp.concatenate(ob_i, axis=0)
        return hfr, hfi, hbr, hbi

    init = tuple(carry_ref[k:k + 1, :] for k in range(4))
    out = lax.fori_loop(0, rows // 8, tile, init)
    for k in range(4):
        carry_ref[k:k + 1, :] = out[k]


def _s5_scan(a_rows, pfr, pfi, pbr, pbi, nseq, cb, lt):
    r, w = pfr.shape
    nblk = r // nseq // cb
    fspec = pl.BlockSpec((cb, lt), lambda b, j, i: (b * nblk + i, j))
    bspec = pl.BlockSpec((cb, lt), lambda b, j, i: (b * nblk + nblk - 1 - i, j))
    st = jax.ShapeDtypeStruct((r, w), F32)
    return pl.pallas_call(
        functools.partial(_s5_scan_kernel, rows=cb),
        out_shape=(st, st, st, st),
        grid=(nseq, w // lt, nblk),
        in_specs=[pl.BlockSpec((4, lt), lambda b, j, i: (0, j)), fspec, fspec, bspec, bspec],
        out_specs=(fspec, fspec, bspec, bspec),
        scratch_shapes=[pltpu.VMEM((4, lt), F32)],
        compiler_params=_cparams(("parallel", "parallel", "arbitrary")),
        name="s5_scan",
    )(a_rows, pfr, pfi, pbr, pbi)


def _s5_state_out_kernel(y0_ref, hfr_ref, hfi_ref, hbr_ref, hbi_ref, q_ref, y_ref):
    h = jnp.concatenate([hfr_ref[...], hfi_ref[...], hbr_ref[...], hbi_ref[...]], axis=1)
    y_ref[...] = y0_ref[...] + _dot(h.astype(BF16), q_ref[0])


def _s5_state_out(y0, hfr, hfi, hbr, hbi, qmat, rb):
    r = y0.shape[0]
    gp = S5_GROUPS // 2
    hspec = pl.BlockSpec((rb, 128), lambda g, i: (i, g))
    yspec = pl.BlockSpec((rb, S5_PAIR), lambda g, i: (i, g))
    return pl.pallas_call(
        _s5_state_out_kernel,
        out_shape=jax.ShapeDtypeStruct(y0.shape, F32),
        grid=(gp, r // rb),
        in_specs=[yspec, hspec, hspec, hspec, hspec,
                  pl.BlockSpec((1, S5_PAIR, S5_PAIR), lambda g, i: (g, 0, 0))],
        out_specs=yspec,
        compiler_params=_cparams(("parallel", "arbitrary")),
        name="s5_state_out",
    )(y0, hfr, hfi, hbr, hbi, qmat)


def _s5_post_kernel(y_ref, gate_ref, mg_ref, wglu_ref, wout_ref, o_ref):
    z = _gelu_tanh(y_ref[...])
    z = z * _sigmoid(_dot(z.astype(BF16), wglu_ref[...]))
    z = z * _silu(gate_ref[...])
    p = _dot(z.astype(BF16), wout_ref[...])
    o_ref[...] = _sigmoid(mg_ref[...]) * p


def _s5_post(y, gate, merge, w_glu, w_out, tm):
    n = y.shape[0]
    return pl.pallas_call(
        _s5_post_kernel,
        out_shape=jax.ShapeDtypeStruct((n, D_MODEL), F32),
        grid=(n // tm,),
        in_specs=[pl.BlockSpec((tm, D_S5), lambda i: (i, 0)),
                  pl.BlockSpec((tm, D_S5), lambda i: (i, 0)),
                  pl.BlockSpec((tm, D_MODEL), lambda i: (i, 0)),
                  pl.BlockSpec((D_S5, D_S5), lambda i: (0, 0)),
                  pl.BlockSpec((D_S5, D_MODEL), lambda i: (0, 0))],
        out_specs=pl.BlockSpec((tm, D_MODEL), lambda i: (i, 0)),
        compiler_params=_cparams(("parallel",)),
        name="s5_post",
    )(y, gate, merge, w_glu, w_out)


def _head_indicator():
    lane = lax.broadcasted_iota(jnp.int32, (D_RWKV, 128), 0) // RWKV_HEAD
    col = lax.broadcasted_iota(jnp.int32, (D_RWKV, 128), 1)
    return (lane == col).astype(BF16)


def _head_sum_bcast(x, e, et):
    return _dot_split(_dot_split(x, e), et)


def _rw_pre_kernel(x_ref, prev_ref, next_ref, mu_ref, wup_ref, aup_ref, w0_ref, a0_ref,
                   kk_ref_p, ka_ref_p, rk_ref_p, e_ref, et_ref,
                   r_ref, kk_ref, v_ref, kdf_ref, kdb_ref, bf_ref, bb_ref, lwf_ref, lwb_ref,
                   bonus_ref, *, tm, seq):
    i = pl.program_id(0)
    x = x_ref[...]
    first = (i * tm) % seq == 0
    last = ((i + 1) * tm) % seq == 0
    prev_row = jnp.where(first, 0.0, prev_ref[7:8, :])
    next_row = jnp.where(last, 0.0, next_ref[0:1, :])
    row = lax.broadcasted_iota(jnp.int32, x.shape, 0)
    prev = jnp.where(row == 0, prev_row, pltpu.roll(x, 1, 0))
    nxt = jnp.where(row == tm - 1, next_row, pltpu.roll(x, tm - 1, 0))
    z = x + (0.5 * (prev + nxt) - x) * mu_ref[...]

    r = z[:, :D_RWKV]
    k = z[:, D_RWKV:2 * D_RWKV]
    v = z[:, 2 * D_RWKV:3 * D_RWKV]
    xw = z[:, 3 * D_RWKV:3 * D_RWKV + 2 * LORA_RANK]
    xa = z[:, 3 * D_RWKV + 2 * LORA_RANK:]
    e, et = e_ref[...], et_ref[...]

    kk = k * kk_ref_p[...]
    kk = kk * lax.rsqrt(_head_sum_bcast(kk * kk, e, et) + L2_EPS)
    bonus_ref[...] = _head_sum_bcast(r * k * rk_ref_p[...], e, et) * v
    r_ref[...] = r
    kk_ref[...] = kk
    v_ref[...] = v

    wpre = _dot(jnp.tanh(xw).astype(BF16), wup_ref[...]) + w0_ref[...]
    apre = _dot(xa.astype(BF16), aup_ref[...]) + a0_ref[...]
    lw = -math.exp(-0.5) * _sigmoid(wpre)
    a = _sigmoid(apre)
    ka = ka_ref_p[...]
    lwf_ref[...] = lw[:, :D_RWKV]
    lwb_ref[...] = lw[:, D_RWKV:]
    af, ab = a[:, :D_RWKV], a[:, D_RWKV:]
    kdf_ref[...] = k * (1.0 + (af - 1.0) * ka)
    kdb_ref[...] = k * (1.0 + (ab - 1.0) * ka)
    bf_ref[...] = kk * af
    bb_ref[...] = kk * ab


def _rw_pre(rw_in, mu, wup, aup, w0, a0, k_k, k_a, r_k, e, et, tm, seq):
    n, w = rw_in.shape
    nb8 = n // 8
    t8 = tm // 8
    row = lambda i: (i, 0)
    const = lambda i: (0, 0)
    ospec = pl.BlockSpec((tm, D_RWKV), row)
    st = jax.ShapeDtypeStruct((n, D_RWKV), F32)
    return pl.pallas_call(
        functools.partial(_rw_pre_kernel, tm=tm, seq=seq),
        out_shape=(st,) * 10,
        grid=(n // tm,),
        in_specs=[pl.BlockSpec((tm, w), row),
                  pl.BlockSpec((8, w), lambda i: (jnp.maximum(i * t8 - 1, 0), 0)),
                  pl.BlockSpec((8, w), lambda i: (jnp.minimum((i + 1) * t8, nb8 - 1), 0)),
                  pl.BlockSpec((1, w), const),
                  pl.BlockSpec((2 * LORA_RANK, 2 * D_RWKV), const),
                  pl.BlockSpec((2 * LORA_RANK, 2 * D_RWKV), const),
                  pl.BlockSpec((1, 2 * D_RWKV), const),
                  pl.BlockSpec((1, 2 * D_RWKV), const),
                  pl.BlockSpec((1, D_RWKV), const),
                  pl.BlockSpec((1, D_RWKV), const),
                  pl.BlockSpec((1, D_RWKV), const),
                  pl.BlockSpec((D_RWKV, 128), const),
                  pl.BlockSpec((128, D_RWKV), const)],
        out_specs=(ospec,) * 10,
        compiler_params=_cparams(("parallel",)),
        name="rw_pre",
    )(rw_in, rw_in, rw_in, mu, wup, aup, w0, a0, k_k, k_a, r_k, e, et)


def _wkv_kernel(r_ref, kk_ref, v_ref, kd_ref, b_ref, lw_ref, y_ref, s_ref, *, rev):
    t = WKV_CHUNK
    hd = RWKV_HEAD

    @pl.when(pl.program_id(1) == 0)
    def _():
        s_ref[...] = jnp.zeros_like(s_ref)

    rr = lax.broadcasted_iota(jnp.int32, (t, t), 0)
    cc = lax.broadcasted_iota(jnp.int32, (t, t), 1)
    if rev:
        incl, strict = rr <= cc, rr < cc
    else:
        incl, strict = rr >= cc, rr > cc
    tri = incl.astype(BF16)
    eye = rr == cc
    eye_f = eye.astype(F32)
    t_last = 0 if rev else t - 1

    def group(gi, carry):
        hv = []
        for j in range(WKV_PAIRS_PER_STEP):
            hp = gi * WKV_PAIRS_PER_STEP + j
            sl = pl.ds(pl.multiple_of(hp * HEAD_PAIR, HEAD_PAIR), HEAD_PAIR)
            r, kk, v = r_ref[:, sl], kk_ref[:, sl], v_ref[:, sl]
            kd, beta, lw = kd_ref[:, sl], b_ref[:, sl], lw_ref[:, sl]
            lw_hi = lw.astype(BF16)
            lw_lo = (lw - lw_hi.astype(F32)).astype(BF16)
            lcum = _dot(tri, lw_hi) + _dot(tri, lw_lo)
            lcum_x = lcum - lw
            cref = lcum[t // 2:t // 2 + 1, :]
            ltot = lcum[t_last:t_last + 1, :]
            e1 = jnp.exp(lcum - cref)
            e1x = jnp.exp(lcum_x - cref)
            e2 = jnp.exp(cref - lcum)
            ec = jnp.exp(cref)
            ewt = jnp.exp(ltot - cref)
            wtot = jnp.exp(ltot)
            r_t = r * e1
            a_t = -kk * e1x
            b_t = beta * e2
            k_t = kd * e2
            a_0 = a_t * ec
            r_0 = r_t * ec
            b_h = b_t * ewt
            k_h = k_t * ewt
            for q in range(2):
                hs = slice(q * hd, (q + 1) * hd)
                hv.append(dict(
                    idx=2 * hp + q,
                    lm=jnp.concatenate([a_t[:, hs], r_t[:, hs]], axis=0).astype(BF16),
                    rm=jnp.concatenate([b_t[:, hs], k_t[:, hs]], axis=0).astype(BF16),
                    v=v[:, hs].astype(BF16), a_0=a_0[:, hs], r_0=r_0[:, hs],
                    b_h=b_h[:, hs].astype(BF16), k_h=k_h[:, hs].astype(BF16), wtot=wtot[:, hs]))
        for h in hv:
            gm = _dot_nt(h['lm'], h['rm'])
            h['a_ab'] = jnp.where(strict, gm[:t, :t], 0.0)
            h['a_ak'] = jnp.where(strict, gm[:t, t:], 0.0)
            h['a_rb'] = jnp.where(incl, gm[t:, :t], 0.0).astype(BF16)
            h['a_rk'] = jnp.where(incl, gm[t:, t:], 0.0)
        for h in hv:
            n_b = h['a_ab'].astype(BF16)
            h['inv'] = eye_f + h['a_ab']
            h['nk'] = _dot(n_b, n_b)
            h['av'] = _dot(jnp.concatenate([h['a_ak'], h['a_rk']], axis=0).astype(BF16), h['v'])
        for s in range(1, NEUMANN_STEPS + 1):
            for h in hv:
                nk_b = h['nk'].astype(BF16)
                if s < NEUMANN_STEPS:
                    nx = _dot(nk_b, jnp.concatenate([h['nk'], h['inv']], axis=1).astype(BF16))
                    h['nk'] = nx[:, :t]
                    h['inv'] = h['inv'] + nx[:, t:]
                else:
                    h['inv'] = h['inv'] + _dot(nk_b, h['inv'].astype(BF16))
        for h in hv:
            pp = _dot(h['inv'].astype(BF16),
                      jnp.concatenate([h['a_0'], h['av'][:t]], axis=1).astype(BF16))
            h['pp'] = pp.astype(BF16)
        for h in hv:
            h['qq'] = _dot(h['a_rb'], h['pp']) + jnp.concatenate([h['r_0'], h['av'][t:]], axis=1)
            h['mn'] = _dot_tn(h['b_h'], h['pp'])
            h['nn'] = _dot_tn(h['k_h'], h['v'])
        ys = []
        for h in hv:
            st = s_ref[h['idx']]
            st_b = st.astype(BF16)
            ys.append(_dot(h['qq'][:, :hd].astype(BF16), st_b) + h['qq'][:, hd:])
            wcol = jnp.sum(jnp.where(eye, jnp.broadcast_to(h['wtot'], (hd, hd)), 0.0),
                           axis=1, keepdims=True)
            s_ref[h['idx']] = (wcol * st + _dot(h['mn'][:, :hd].astype(BF16), st_b)
                               + h['mn'][:, hd:] + h['nn'])
        for j in range(WKV_PAIRS_PER_STEP):
            hp = gi * WKV_PAIRS_PER_STEP + j
            sl = pl.ds(pl.multiple_of(hp * HEAD_PAIR, HEAD_PAIR), HEAD_PAIR)
            y_ref[:, sl] = jnp.concatenate(ys[2 * j:2 * j + 2], axis=1)
        return carry

    lax.fori_loop(0, RWKV_HEADS // (2 * WKV_PAIRS_PER_STEP), group, 0)


def _wkv(r, kk, v, kd, beta, lw, nseq, rev):
    n = r.shape[0]
    t = WKV_CHUNK
    nc = n // nseq // t
    if rev:
        idx = lambda b, c: (b * nc + nc - 1 - c, 0)
    else:
        idx = lambda b, c: (b * nc + c, 0)
    spec = pl.BlockSpec((t, D_RWKV), idx)
    return pl.pallas_call(
        functools.partial(_wkv_kernel, rev=rev),
        out_shape=jax.ShapeDtypeStruct((n, D_RWKV), F32),
        grid=(nseq, nc),
        in_specs=[spec] * 6,
        out_specs=spec,
        scratch_shapes=[pltpu.VMEM((RWKV_HEADS, RWKV_HEAD, RWKV_HEAD), F32)],
        compiler_params=_cparams(("parallel", "arbitrary")),
        name="wkv_bwd" if rev else "wkv_fwd",
    )(r, kk, v, kd, beta, lw)


def _rw_post_kernel(yf_ref, yb_ref, bonus_ref, gate_ref, mg_ref, lnw_ref, lnb_ref, e_ref, et_ref,
                    wout_ref, o_ref):
    e, et = e_ref[...], et_ref[...]
    y = yf_ref[...] + yb_ref[...]
    inv_n = 1.0 / RWKV_HEAD
    mean = _head_sum_bcast(y, e, et) * inv_n
    yc = y - mean
    var = _head_sum_bcast(yc * yc, e, et) * inv_n
    y = yc * lax.rsqrt(var + GN_EPS) * lnw_ref[...] + lnb_ref[...] + bonus_ref[...]
    y = y * _silu(gate_ref[...])
    p = _dot(y.astype(BF16), wout_ref[...])
    o_ref[...] = _sigmoid(mg_ref[...]) * p


def _rw_post(yf, yb, bonus, gate, merge, ln_w, ln_b, e, et, w_out, tm):
    n = yf.shape[0]
    row = lambda i: (i, 0)
    const = lambda i: (0, 0)
    tspec = pl.BlockSpec((tm, D_RWKV), row)
    return pl.pallas_call(
        _rw_post_kernel,
        out_shape=jax.ShapeDtypeStruct((n, D_MODEL), F32),
        grid=(n // tm,),
        in_specs=[tspec, tspec, tspec, tspec,
                  pl.BlockSpec((tm, D_MODEL), lambda i: (i, 1)),
                  pl.BlockSpec((1, D_RWKV), const), pl.BlockSpec((1, D_RWKV), const),
                  pl.BlockSpec((D_RWKV, 128), const), pl.BlockSpec((128, D_RWKV), const),
                  pl.BlockSpec((D_RWKV, D_MODEL), const)],
        out_specs=pl.BlockSpec((tm, D_MODEL), row),
        compiler_params=_cparams(("parallel",)),
        name="rw_post",
    )(yf, yb, bonus, gate, merge, ln_w, ln_b, e, et, w_out)


def _final_kernel(hs_ref, hr_ref, x_ref, wo_ref, g_ref, o_ref):
    h = (hs_ref[...] + hr_ref[...]).astype(BF16)
    out = _dot(h, wo_ref[...])
    ms = jnp.mean(out * out, axis=-1, keepdims=True)
    o_ref[...] = x_ref[...] + out * lax.rsqrt(ms + RMS_EPS) * g_ref[...]


def _final(hs, hr, x, w_o, g, tm):
    n = x.shape[0]
    row = lambda i: (i, 0)
    const = lambda i: (0, 0)
    tspec = pl.BlockSpec((tm, D_MODEL), row)
    return pl.pallas_call(
        _final_kernel,
        out_shape=jax.ShapeDtypeStruct((n, D_MODEL), F32),
        grid=(n // tm,),
        in_specs=[tspec, tspec, tspec, pl.BlockSpec((D_MODEL, D_MODEL), const),
                  pl.BlockSpec((1, D_MODEL), const)],
        out_specs=tspec,
        compiler_params=_cparams(("parallel",)),
        name="final",
    )(hs, hr, x, w_o, g)


def _pick(n, pref):
    while n % pref:
        pref //= 2
    return pref


def _prepare(pre_norm_g, post_norm_g, w_in, s5_lam_re, s5_lam_im, s5_log_dt, s5_b_re, s5_b_im,
             s5_c_re, s5_c_im, s5_d, s5_w_glu, s5_w_out, rw_mu, rw_w0, rw_w_up, rw_a0, rw_a_up,
             rw_k_k, rw_k_a, rw_r_k, rw_ln_w, rw_ln_b, rw_w_out, w_o):
    p = {}
    p['pre_g'] = pre_norm_g.astype(F32).reshape(1, D_MODEL)
    p['post_g'] = post_norm_g.astype(F32).reshape(1, D_MODEL)
    w = w_in.astype(BF16)
    o = 0
    for name, width in (('w_u', D_S5), ('w_sg', D_S5), ('w_rw', D_RW_IN), ('w_rg', D_RWKV),
                        ('w_mg', 2 * D_MODEL)):
        p[name] = w[:, o:o + width]
        o += width
    p['s5_w1'], p['s5_q'], p['s5_a'] = _s5_weights(
        s5_lam_re, s5_lam_im, s5_log_dt, s5_b_re, s5_b_im, s5_c_re, s5_c_im, s5_d)
    p['w_glu'] = s5_w_glu.astype(BF16)
    p['s5_w_out'] = s5_w_out.astype(BF16)
    p['mu'] = rw_mu.astype(F32).reshape(1, D_RW_IN)
    zeros = jnp.zeros((LORA_RANK, D_RWKV), F32)

    def blockdiag(u):
        return jnp.concatenate([jnp.concatenate([u[0], zeros], axis=1),
                                jnp.concatenate([zeros, u[1]], axis=1)], axis=0).astype(BF16)

    p['wup'] = blockdiag(rw_w_up.astype(F32))
    p['aup'] = blockdiag(rw_a_up.astype(F32))
    p['w0'] = rw_w0.astype(F32).reshape(1, 2 * D_RWKV)
    p['a0'] = rw_a0.astype(F32).reshape(1, 2 * D_RWKV)
    p['k_k'] = rw_k_k.astype(F32).reshape(1, D_RWKV)
    p['k_a'] = rw_k_a.astype(F32).reshape(1, D_RWKV)
    p['r_k'] = rw_r_k.astype(F32).reshape(1, D_RWKV)
    p['ln_w'] = rw_ln_w.astype(F32).reshape(1, D_RWKV)
    p['ln_b'] = rw_ln_b.astype(F32).reshape(1, D_RWKV)
    p['rw_w_out'] = rw_w_out.astype(BF16)
    p['w_o'] = w_o.astype(BF16)
    e = _head_indicator()
    p['e'] = e
    p['et'] = e.T
    return p


def _s5_mix(u, p, nseq):
    n = u.shape[0]
    t, g, h = S5_CHUNK, S5_GROUPS, S5_GROUP
    r = n // t
    x2 = u.reshape(r, t, g, h).transpose(0, 2, 1, 3).reshape(r, g * t * h)
    rows_per_seq = r // nseq
    rb = _pick(r, 512)
    y0, pfr, pfi, pbr, pbi = _s5_chunk(x2, p['s5_w1'], rb)
    cb = _pick(rows_per_seq, 128)
    hfr, hfi, hbr, hbi = _s5_scan(p['s5_a'], pfr, pfi, pbr, pbi, nseq, cb, 1024)
    y2 = _s5_state_out(y0, hfr, hfi, hbr, hbi, p['s5_q'], rb)
    return y2.reshape(r, g, t, h).transpose(0, 2, 1, 3).reshape(n, g * h)


def _layer(x, p):
    bsz, seq, _ = x.shape
    n = bsz * seq
    x2 = x.reshape(n, D_MODEL)
    tm = _pick(n, 1024)
    hn = _rms_cast(x2, p['pre_g'], _pick(n, 512))
    u = _matmul(hn, p['w_u'], tm, 512, BF16)
    s5_gate = _matmul(hn, p['w_sg'], tm, 512)
    rw_in = _matmul(hn, p['w_rw'], tm, 256)
    rw_gate = _matmul(hn, p['w_rg'], tm, 512)
    merge = _matmul(hn, p['w_mg'], tm, 512)

    y_s5 = _s5_mix(u, p, bsz)
    hs = _s5_post(y_s5, s5_gate, merge, p['w_glu'], p['s5_w_out'], _pick(n, 512))

    tr = _pick(seq, 256)
    (r, kk, v, kdf, kdb, bf, bb, lwf, lwb, bonus) = _rw_pre(
        rw_in, p['mu'], p['wup'], p['aup'], p['w0'], p['a0'], p['k_k'], p['k_a'], p['r_k'],
        p['e'], p['et'], tr, seq)
    yf = _wkv(r, kk, v, kdf, bf, lwf, bsz, False)
    yb = _wkv(r, kk, v, kdb, bb, lwb, bsz, True)
    hr = _rw_post(yf, yb, bonus, rw_gate, merge, p['ln_w'], p['ln_b'], p['e'], p['et'],
                  p['rw_w_out'], _pick(n, 512))

    out = _final(hs, hr, x2, p['w_o'], p['post_g'], _pick(n, 512))
    return out.reshape(bsz, seq, D_MODEL)


def kernel(x_prompt, x_sample, pre_norm_g, post_norm_g, w_in, s5_lam_re, s5_lam_im, s5_log_dt, s5_b_re, s5_b_im, s5_c_re, s5_c_im, s5_d, s5_w_glu, s5_w_out, rw_mu, rw_w0, rw_w_up, rw_a0, rw_a_up, rw_k_k, rw_k_a, rw_r_k, rw_ln_w, rw_ln_b, rw_w_out, w_o):
    params = (pre_norm_g, post_norm_g, w_in, s5_lam_re, s5_lam_im, s5_log_dt, s5_b_re, s5_b_im,
              s5_c_re, s5_c_im, s5_d, s5_w_glu, s5_w_out, rw_mu, rw_w0, rw_w_up, rw_a0, rw_a_up,
              rw_k_k, rw_k_a, rw_r_k, rw_ln_w, rw_ln_b, rw_w_out, w_o)
    y_prompt, y_sample = x_prompt, x_sample
    for layer in range(w_in.shape[0]):
        p = _prepare(*[w[layer] for w in params])
        y_prompt = _layer(y_prompt, p)
        y_sample = _layer(y_sample, p)
    return (y_prompt, y_sample)
```

```python
import functools
import math

import jax
import jax.numpy as jnp
from jax import lax
from jax.experimental import pallas as pl
from jax.experimental.pallas import tpu as pltpu

F32 = jnp.float32
BF16 = jnp.bfloat16

D_MODEL = 2048
D_S5 = 1024
S5_GROUP = 16
S5_GROUPS = 64
S5_STATE = 64
D_RWKV = 1024
RWKV_HEAD = 64
RWKV_HEADS = 16
LORA_RANK = 64
D_RW_IN = 3 * D_RWKV + 4 * LORA_RANK
RMS_EPS = 1e-6
GN_EPS = 64e-5
L2_EPS = 1e-12

S5_CHUNK = 16
S5_PAIR = 2 * S5_CHUNK * S5_GROUP
WKV_CHUNK = 64
HEAD_PAIR = 2 * RWKV_HEAD
NEUMANN_STEPS = 5
WKV_PAIRS_PER_STEP = 8
HALO = 16

VMEM_LIMIT = 48 * 1024 * 1024


def _cparams(sem):
    return pltpu.CompilerParams(dimension_semantics=sem, vmem_limit_bytes=VMEM_LIMIT)


def _dot(a, b):
    return jnp.dot(a, b, preferred_element_type=F32)


def _dot_nt(a, b):
    return lax.dot_general(a, b, (((1,), (1,)), ((), ())), preferred_element_type=F32)


def _dot_tn(a, b):
    return lax.dot_general(a, b, (((0,), (0,)), ((), ())), preferred_element_type=F32)


def _dot_split(a, b_bf16):
    hi = a.astype(BF16)
    lo = (a - hi.astype(F32)).astype(BF16)
    return _dot(hi, b_bf16) + _dot(lo, b_bf16)


def _sigmoid(x):
    return 1.0 / (1.0 + jnp.exp(-x))


def _silu(x):
    return x * _sigmoid(x)


def _gelu_tanh(x):
    c = math.sqrt(2.0 / math.pi)
    return 0.5 * x * (1.0 + jnp.tanh(c * (x + 0.044715 * (x * x * x))))


IN_TN = 512
D_RW_PAD = 3584
IN_SEGMENTS = (('u', D_S5), ('s5_gate', D_S5), ('rw_in', D_RW_PAD), ('rw_gate', D_RWKV),
               ('merge', 2 * D_MODEL))


def _in_proj_kernel(x_ref, g_ref, w_ref, *refs):
    out_refs, hn_ref = refs[:-1], refs[-1]
    j = pl.program_id(1)

    @pl.when(j == 0)
    def _():
        x = x_ref[...]
        ms = jnp.mean(x * x, axis=-1, keepdims=True)
        hn_ref[...] = (x * lax.rsqrt(ms + RMS_EPS) * g_ref[...]).astype(BF16)

    res = _dot(hn_ref[...], w_ref[...]).astype(BF16)
    start = 0
    for ref, (_, width) in zip(out_refs, IN_SEGMENTS):
        stop = start + width // IN_TN

        @pl.when((j >= start) & (j < stop))
        def _(ref=ref):
            ref[...] = res

        start = stop


def _in_proj(x, g, w, tm):
    n, d = x.shape
    ntiles = w.shape[1] // IN_TN
    out_shape, out_specs, start = [], [], 0
    for _, width in IN_SEGMENTS:
        nt = width // IN_TN
        out_shape.append(jax.ShapeDtypeStruct((n, width), BF16))
        out_specs.append(pl.BlockSpec(
            (tm, IN_TN), lambda i, j, start=start, nt=nt: (i, jnp.clip(j - start, 0, nt - 1))))
        start += nt
    return pl.pallas_call(
        _in_proj_kernel,
        out_shape=tuple(out_shape),
        grid=(n // tm, ntiles),
        in_specs=[pl.BlockSpec((tm, d), lambda i, j: (i, 0)),
                  pl.BlockSpec((1, d), lambda i, j: (0, 0)),
                  pl.BlockSpec((d, IN_TN), lambda i, j: (0, j))],
        out_specs=tuple(out_specs),
        scratch_shapes=[pltpu.VMEM((tm, d), BF16)],
        compiler_params=_cparams(("parallel", "arbitrary")),
        name="in_proj",
    )(x, g, w)


def _s5_weights(lam_re, lam_im, log_dt, b_re, b_im, c_re, c_im, d):
    t = S5_CHUNK
    g, p, h = S5_GROUPS, S5_STATE, S5_GROUP
    hp = lax.Precision.HIGHEST
    lam = lax.complex(lam_re.astype(F32), lam_im.astype(F32))
    dt = jnp.exp(log_dt.astype(F32))[..., None]
    lam_dt = lam * dt
    lam_bar = jnp.exp(lam_dt)
    bbar = ((lam_bar - 1.0) / lam)[..., None] * lax.complex(b_re.astype(F32), b_im.astype(F32))
    c = lax.complex(c_re.astype(F32), c_im.astype(F32))
    steps = jnp.arange(t + 1, dtype=F32)
    pw = jnp.exp(lam_dt[None] * steps[:, None, None, None])

    kern = jnp.einsum('dghp,ldgp,dgpj->dglhj', c, pw[:t], bbar, precision=hp).real
    tt = jnp.arange(t)
    lag = tt[None, :] - tt[:, None]
    kf = kern[0][:, jnp.abs(lag)]
    kb = kern[1][:, jnp.abs(lag)]
    k0 = kern[0][:, 0] + kern[1][:, 0] + jnp.eye(h, dtype=F32) * d.astype(F32).reshape(g, h)[:, :, None]
    lag5 = lag[None, :, :, None, None]
    m = jnp.where(lag5 > 0, kf, jnp.where(lag5 < 0, kb, k0[:, None, None]))
    m = m.transpose(0, 1, 4, 2, 3).reshape(g, t * h, t * h)

    pf = pw[t - 1 - tt, 0][..., None] * bbar[0][None]
    pb = pw[tt, 1][..., None] * bbar[1][None]

    def _p_mat(x):
        return x.transpose(1, 0, 3, 2).reshape(g, t * h, p)

    p_parts = [_p_mat(pf.real), _p_mat(pf.imag), _p_mat(pb.real), _p_mat(pb.imag)]

    cf = c[0][None] * pw[tt + 1, 0][:, :, None, :]
    cb = c[1][None] * pw[t - tt, 1][:, :, None, :]

    def _q_mat(x):
        return x.transpose(1, 3, 0, 2).reshape(g, p, t * h)

    q_parts = [_q_mat(cf.real), _q_mat(-cf.imag), _q_mat(cb.real), _q_mat(-cb.imag)]

    gp = g // 2
    eye2 = jnp.eye(2, dtype=F32)

    def _pair(x, r, cdim):
        x = x.reshape(gp, 2, r, cdim)
        return jnp.einsum('gqrc,qs->gqrsc', x, eye2).reshape(gp, 2 * r, 2 * cdim)

    w1 = jnp.concatenate([_pair(m, t * h, t * h)] + [_pair(x, t * h, p) for x in p_parts], axis=2)
    qmat = jnp.concatenate([_pair(x, p, t * h) for x in q_parts], axis=1)
    at = pw[t]
    a_rows = jnp.stack([at[0].real, at[0].imag, at[1].real, at[1].imag]).reshape(4, g * p)
    return w1.astype(BF16), qmat.astype(BF16), a_rows


def _s5_chunk_kernel(x_ref, w_ref, y_ref, pfr_ref, pfi_ref, pbr_ref, pbi_ref):
    res = _dot(x_ref[...], w_ref[0])
    y_ref[...] = res[:, :S5_PAIR]
    for k, ref in enumerate((pfr_ref, pfi_ref, pbr_ref, pbi_ref)):
        ref[...] = res[:, S5_PAIR + 128 * k:S5_PAIR + 128 * (k + 1)]


def _s5_chunk(x2, w1, rb):
    r = x2.shape[0]
    gp = S5_GROUPS // 2
    st = jax.ShapeDtypeStruct((r, gp * 128), F32)
    pspec = pl.BlockSpec((rb, 128), lambda g, i: (i, g))
    return pl.pallas_call(
        _s5_chunk_kernel,
        out_shape=(jax.ShapeDtypeStruct((r, gp * S5_PAIR), F32), st, st, st, st),
        grid=(gp, r // rb),
        in_specs=[pl.BlockSpec((rb, S5_PAIR), lambda g, i: (i, g)),
                  pl.BlockSpec((1, S5_PAIR, 2 * S5_PAIR), lambda g, i: (g, 0, 0))],
        out_specs=(pl.BlockSpec((rb, S5_PAIR), lambda g, i: (i, g)), pspec, pspec, pspec, pspec),
        compiler_params=_cparams(("parallel", "arbitrary")),
        name="s5_chunk",
    )(x2, w1)


def _s5_scan_kernel(a_ref, pfr_ref, pfi_ref, pbr_ref, pbi_ref,
                    hfr_ref, hfi_ref, hbr_ref, hbi_ref, carry_ref, *, rows):
    @pl.when(pl.program_id(2) == 0)
    def _():
        carry_ref[...] = jnp.zeros_like(carry_ref)

    afr, afi = a_ref[0:1, :], a_ref[1:2, :]
    abr, abi = a_ref[2:3, :], a_ref[3:4, :]

    def tile(i, carry):
        hfr, hfi, hbr, hbi = carry
        r0 = pl.multiple_of(i * 8, 8)
        pfr, pfi = pfr_ref[pl.ds(r0, 8), :], pfi_ref[pl.ds(r0, 8), :]
        rb0 = pl.multiple_of(rows - 8 - i * 8, 8)
        pbr, pbi = pbr_ref[pl.ds(rb0, 8), :], pbi_ref[pl.ds(rb0, 8), :]
        of_r, of_i, ob_r, ob_i = [], [], [None] * 8, [None] * 8
        for s in range(8):
            of_r.append(hfr)
            of_i.append(hfi)
            hfr, hfi = (afr * hfr - afi * hfi + pfr[s:s + 1, :],
                        afr * hfi + afi * hfr + pfi[s:s + 1, :])
            sb = 7 - s
            ob_r[sb] = hbr
            ob_i[sb] = hbi
            hbr, hbi = (abr * hbr - abi * hbi + pbr[sb:sb + 1, :],
                        abr * hbi + abi * hbr + pbi[sb:sb + 1, :])
        hfr_ref[pl.ds(r0, 8), :] = jnp.concatenate(of_r, axis=0)
        hfi_ref[pl.ds(r0, 8), :] = jnp.concatenate(of_i, axis=0)
        hbr_ref[pl.ds(rb0, 8), :] = jnp.concatenate(ob_r, axis=0)
        hbi_ref[pl.ds(rb0, 8), :] = jnp.concatenate(ob_i, axis=0)
        return hfr, hfi, hbr, hbi

    init = tuple(carry_ref[k:k + 1, :] for k in range(4))
    out = lax.fori_loop(0, rows // 8, tile, init)
    for k in range(4):
        carry_ref[k:k + 1, :] = out[k]


def _s5_scan(a_rows, pfr, pfi, pbr, pbi, nseq, cb, lt):
    r, w = pfr.shape
    nblk = r // nseq // cb
    fspec = pl.BlockSpec((cb, lt), lambda b, j, i: (b * nblk + i, j))
    bspec = pl.BlockSpec((cb, lt), lambda b, j, i: (b * nblk + nblk - 1 - i, j))
    st = jax.ShapeDtypeStruct((r, w), F32)
    return pl.pallas_call(
        functools.partial(_s5_scan_kernel, rows=cb),
        out_shape=(st, st, st, st),
        grid=(nseq, w // lt, nblk),
        in_specs=[pl.BlockSpec((4, lt), lambda b, j, i: (0, j)), fspec, fspec, bspec, bspec],
        out_specs=(fspec, fspec, bspec, bspec),
        scratch_shapes=[pltpu.VMEM((4, lt), F32)],
        compiler_params=_cparams(("parallel", "parallel", "arbitrary")),
        name="s5_scan",
    )(a_rows, pfr, pfi, pbr, pbi)


def _s5_state_out_kernel(y0_ref, hfr_ref, hfi_ref, hbr_ref, hbi_ref, q_ref, y_ref):
    h = jnp.concatenate([hfr_ref[...], hfi_ref[...], hbr_ref[...], hbi_ref[...]], axis=1)
    y_ref[...] = (y0_ref[...] + _dot(h.astype(BF16), q_ref[0])).astype(BF16)


def _s5_state_out(y0, hfr, hfi, hbr, hbi, qmat, rb):
    r = y0.shape[0]
    gp = S5_GROUPS // 2
    hspec = pl.BlockSpec((rb, 128), lambda g, i: (i, g))
    yspec = pl.BlockSpec((rb, S5_PAIR), lambda g, i: (i, g))
    return pl.pallas_call(
        _s5_state_out_kernel,
        out_shape=jax.ShapeDtypeStruct(y0.shape, BF16),
        grid=(gp, r // rb),
        in_specs=[yspec, hspec, hspec, hspec, hspec,
                  pl.BlockSpec((1, S5_PAIR, S5_PAIR), lambda g, i: (g, 0, 0))],
        out_specs=yspec,
        compiler_params=_cparams(("parallel", "arbitrary")),
        name="s5_state_out",
    )(y0, hfr, hfi, hbr, hbi, qmat)


def _head_indicator():
    lane = lax.broadcasted_iota(jnp.int32, (D_RWKV, 128), 0) // RWKV_HEAD
    col = lax.broadcasted_iota(jnp.int32, (D_RWKV, 128), 1)
    return (lane == col).astype(BF16)


def _head_sum_bcast(x, e, et):
    return _dot_split(_dot_split(x, e), et)


def _rw_pre_kernel(x_ref, prev_ref, next_ref, mu_ref, wup_ref, aup_ref, w0_ref, a0_ref,
                   kk_ref_p, ka_ref_p, rk_ref_p, e_ref, et_ref,
                   r_ref, kk_ref, v_ref, kdf_ref, kdb_ref, bf_ref, bb_ref, lwf_ref, lwb_ref,
                   bonus_ref, *, tm, seq):
    i = pl.program_id(0)
    x = x_ref[...].astype(F32)
    first = (i * tm) % seq == 0
    last = ((i + 1) * tm) % seq == 0
    prev_row = jnp.where(first, 0.0, prev_ref[HALO - 1:HALO, :].astype(F32))
    next_row = jnp.where(last, 0.0, next_ref[0:1, :].astype(F32))
    row = lax.broadcasted_iota(jnp.int32, x.shape, 0)
    prev = jnp.where(row == 0, prev_row, pltpu.roll(x, 1, 0))
    nxt = jnp.where(row == tm - 1, next_row, pltpu.roll(x, tm - 1, 0))
    z = x + (0.5 * (prev + nxt) - x) * mu_ref[...]

    r = z[:, :D_RWKV]
    k = z[:, D_RWKV:2 * D_RWKV]
    v = z[:, 2 * D_RWKV:3 * D_RWKV]
    xw = z[:, 3 * D_RWKV:3 * D_RWKV + 2 * LORA_RANK]
    xa = z[:, 3 * D_RWKV + 2 * LORA_RANK:]
    e, et = e_ref[...], et_ref[...]

    kk = k * kk_ref_p[...]
    kk = kk * lax.rsqrt(_head_sum_bcast(kk * kk, e, et) + L2_EPS)
    bonus_ref[...] = (_head_sum_bcast(r * k * rk_ref_p[...], e, et) * v).astype(BF16)
    r_ref[...] = r.astype(BF16)
    kk_ref[...] = kk.astype(BF16)
    v_ref[...] = v.astype(BF16)

    wpre = _dot(jnp.tanh(xw).astype(BF16), wup_ref[...]) + w0_ref[...]
    apre = _dot(xa.astype(BF16), aup_ref[...]) + a0_ref[...]
    lw = -math.exp(-0.5) * _sigmoid(wpre)
    a = _sigmoid(apre)
    ka = ka_ref_p[...]
    lwf_ref[...] = lw[:, :D_RWKV]
    lwb_ref[...] = lw[:, D_RWKV:]
    af, ab = a[:, :D_RWKV], a[:, D_RWKV:]
    kdf_ref[...] = (k * (1.0 + (af - 1.0) * ka)).astype(BF16)
    kdb_ref[...] = (k * (1.0 + (ab - 1.0) * ka)).astype(BF16)
    bf_ref[...] = (kk * af).astype(BF16)
    bb_ref[...] = (kk * ab).astype(BF16)


def _rw_pre(rw_in, mu, wup, aup, w0, a0, k_k, k_a, r_k, e, et, tm, seq):
    n = rw_in.shape[0]
    w = D_RW_IN
    nbh = n // HALO
    th = tm // HALO
    row = lambda i: (i, 0)
    const = lambda i: (0, 0)
    ospec = pl.BlockSpec((tm, D_RWKV), row)
    sb = jax.ShapeDtypeStruct((n, D_RWKV), BF16)
    sf = jax.ShapeDtypeStruct((n, D_RWKV), F32)
    return pl.pallas_call(
        functools.partial(_rw_pre_kernel, tm=tm, seq=seq),
        out_shape=(sb,) * 7 + (sf, sf, sb),
        grid=(n // tm,),
        in_specs=[pl.BlockSpec((tm, w), row),
                  pl.BlockSpec((HALO, w), lambda i: (jnp.maximum(i * th - 1, 0), 0)),
                  pl.BlockSpec((HALO, w), lambda i: (jnp.minimum((i + 1) * th, nbh - 1), 0)),
                  pl.BlockSpec((1, w), const),
                  pl.BlockSpec((2 * LORA_RANK, 2 * D_RWKV), const),
                  pl.BlockSpec((2 * LORA_RANK, 2 * D_RWKV), const),
                  pl.BlockSpec((1, 2 * D_RWKV), const),
                  pl.BlockSpec((1, 2 * D_RWKV), const),
                  pl.BlockSpec((1, D_RWKV), const),
                  pl.BlockSpec((1, D_RWKV), const),
                  pl.BlockSpec((1, D_RWKV), const),
                  pl.BlockSpec((D_RWKV, 128), const),
                  pl.BlockSpec((128, D_RWKV), const)],
        out_specs=(ospec,) * 10,
        compiler_params=_cparams(("parallel",)),
        name="rw_pre",
    )(rw_in, rw_in, rw_in, mu, wup, aup, w0, a0, k_k, k_a, r_k, e, et)


def _wkv_kernel(r_ref, kk_ref, v_ref, kd_ref, b_ref, lw_ref, y_ref, s_ref, *, rev):
    t = WKV_CHUNK
    hd = RWKV_HEAD

    @pl.when(pl.program_id(1) == 0)
    def _():
        s_ref[...] = jnp.zeros_like(s_ref)

    rr = lax.broadcasted_iota(jnp.int32, (t, t), 0)
    cc = lax.broadcasted_iota(jnp.int32, (t, t), 1)
    if rev:
        incl, strict = rr <= cc, rr < cc
    else:
        incl, strict = rr >= cc, rr > cc
    tri = incl.astype(BF16)
    eye = rr == cc
    eye_f = eye.astype(F32)
    t_last = 0 if rev else t - 1

    def group(gi, carry):
        hv = []
        for j in range(WKV_PAIRS_PER_STEP):
            hp = gi * WKV_PAIRS_PER_STEP + j
            sl = pl.ds(pl.multiple_of(hp * HEAD_PAIR, HEAD_PAIR), HEAD_PAIR)
            r, kk, v = (ref[:, sl].astype(F32) for ref in (r_ref, kk_ref, v_ref))
            kd, beta, lw = kd_ref[:, sl].astype(F32), b_ref[:, sl].astype(F32), lw_ref[:, sl]
            lw_hi = lw.astype(BF16)
            lw_lo = (lw - lw_hi.astype(F32)).astype(BF16)
            lcum = _dot(tri, lw_hi) + _dot(tri, lw_lo)
            lcum_x = lcum - lw
            cref = lcum[t // 2:t // 2 + 1, :]
            ltot = lcum[t_last:t_last + 1, :]
            e1 = jnp.exp(lcum - cref)
            e1x = jnp.exp(lcum_x - cref)
            e2 = jnp.exp(cref - lcum)
            ec = jnp.exp(cref)
            ewt = jnp.exp(ltot - cref)
            wtot = jnp.exp(ltot)
            r_t = r * e1
            a_t = -kk * e1x
            b_t = beta * e2
            k_t = kd * e2
            a_0 = a_t * ec
            r_0 = r_t * ec
            b_h = b_t * ewt
            k_h = k_t * ewt
            for q in range(2):
                hs = slice(q * hd, (q + 1) * hd)
                hv.append(dict(
                    idx=2 * hp + q,
                    lm=jnp.concatenate([a_t[:, hs], r_t[:, hs]], axis=0).astype(BF16),
                    rm=jnp.concatenate([b_t[:, hs], k_t[:, hs]], axis=0).astype(BF16),
                    v=v[:, hs].astype(BF16), a_0=a_0[:, hs], r_0=r_0[:, hs],
                    b_h=b_h[:, hs].astype(BF16), k_h=k_h[:, hs].astype(BF16), wtot=wtot[:, hs]))
        for h in hv:
            gm = _dot_nt(h['lm'], h['rm'])
            h['a_ab'] = jnp.where(strict, gm[:t, :t], 0.0)
            h['a_ak'] = jnp.where(strict, gm[:t, t:], 0.0)
            h['a_rb'] = jnp.where(incl, gm[t:, :t], 0.0).astype(BF16)
            h['a_rk'] = jnp.where(incl, gm[t:, t:], 0.0)
        for h in hv:
            n_b = h['a_ab'].astype(BF16)
            h['inv'] = eye_f + h['a_ab']
            h['nk'] = _dot(n_b, n_b)
            h['av'] = _dot(jnp.concatenate([h['a_ak'], h['a_rk']], axis=0).astype(BF16), h['v'])
        for s in range(1, NEUMANN_STEPS + 1):
            for h in hv:
                nk_b = h['nk'].astype(BF16)
                if s < NEUMANN_STEPS:
                    nx = _dot(nk_b, jnp.concatenate([h['nk'], h['inv']], axis=1).astype(BF16))
                    h['nk'] = nx[:, :t]
                    h['inv'] = h['inv'] + nx[:, t:]
                else:
                    h['inv'] = h['inv'] + _dot(nk_b, h['inv'].astype(BF16))
        for h in hv:
            pp = _dot(h['inv'].astype(BF16),
                      jnp.concatenate([h['a_0'], h['av'][:t]], axis=1).astype(BF16))
            h['pp'] = pp.astype(BF16)
        for h in hv:
            h['qq'] = _dot(h['a_rb'], h['pp']) + jnp.concatenate([h['r_0'], h['av'][t:]], axis=1)
            h['mn'] = _dot_tn(h['b_h'], h['pp'])
            h['nn'] = _dot_tn(h['k_h'], h['v'])
        ys = []
        for h in hv:
            st = s_ref[h['idx']]
            st_b = st.astype(BF16)
            ys.append(_dot(h['qq'][:, :hd].astype(BF16), st_b) + h['qq'][:, hd:])
            wcol = jnp.sum(jnp.where(eye, jnp.broadcast_to(h['wtot'], (hd, hd)), 0.0),
                           axis=1, keepdims=True)
            s_ref[h['idx']] = (wcol * st + _dot(h['mn'][:, :hd].astype(BF16), st_b)
                               + h['mn'][:, hd:] + h['nn'])
        for j in range(WKV_PAIRS_PER_STEP):
            hp = gi * WKV_PAIRS_PER_STEP + j
            sl = pl.ds(pl.multiple_of(hp * HEAD_PAIR, HEAD_PAIR), HEAD_PAIR)
            y_ref[:, sl] = jnp.concatenate(ys[2 * j:2 * j + 2], axis=1)
        return carry

    lax.fori_loop(0, RWKV_HEADS // (2 * WKV_PAIRS_PER_STEP), group, 0)


def _wkv(r, kk, v, kd, beta, lw, nseq, rev):
    n = r.shape[0]
    t = WKV_CHUNK
    nc = n // nseq // t
    if rev:
        idx = lambda b, c: (b * nc + nc - 1 - c, 0)
    else:
        idx = lambda b, c: (b * nc + c, 0)
    spec = pl.BlockSpec((t, D_RWKV), idx)
    return pl.pallas_call(
        functools.partial(_wkv_kernel, rev=rev),
        out_shape=jax.ShapeDtypeStruct((n, D_RWKV), F32),
        grid=(nseq, nc),
        in_specs=[spec] * 6,
        out_specs=spec,
        scratch_shapes=[pltpu.VMEM((RWKV_HEADS, RWKV_HEAD, RWKV_HEAD), F32)],
        compiler_params=_cparams(("parallel", "arbitrary")),
        name="wkv_bwd" if rev else "wkv_fwd",
    )(r, kk, v, kd, beta, lw)


def _tail_kernel(ys_ref, sg_ref, ms_ref, yf_ref, yb_ref, bonus_ref, rg_ref, mr_ref, x_ref,
                 wglu_ref, swout_ref, lnw_ref, lnb_ref, e_ref, et_ref, rwout_ref, wo_ref, g_ref,
                 o_ref):
    f32 = lambda ref: ref[...].astype(F32)
    z = _gelu_tanh(f32(ys_ref))
    z = z * _sigmoid(_dot(z.astype(BF16), wglu_ref[...]))
    z = z * _silu(f32(sg_ref))
    h = _sigmoid(f32(ms_ref)) * _dot(z.astype(BF16), swout_ref[...])

    e, et = e_ref[...], et_ref[...]
    y = yf_ref[...] + yb_ref[...]
    inv_n = 1.0 / RWKV_HEAD
    mean = _head_sum_bcast(y, e, et) * inv_n
    yc = y - mean
    var = _head_sum_bcast(yc * yc, e, et) * inv_n
    y = yc * lax.rsqrt(var + GN_EPS) * lnw_ref[...] + lnb_ref[...] + f32(bonus_ref)
    y = y * _silu(f32(rg_ref))
    h = h + _sigmoid(f32(mr_ref)) * _dot(y.astype(BF16), rwout_ref[...])

    out = _dot(h.astype(BF16), wo_ref[...])
    ms = jnp.mean(out * out, axis=-1, keepdims=True)
    o_ref[...] = x_ref[...] + out * lax.rsqrt(ms + RMS_EPS) * g_ref[...]


def _tail(y_s5, s5_gate, merge, yf, yb, bonus, rw_gate, x, p, tm):
    n = x.shape[0]
    row = lambda i: (i, 0)
    const = lambda i: (0, 0)
    half = pl.BlockSpec((tm, D_RWKV), row)
    full = pl.BlockSpec((tm, D_MODEL), row)

    def resident(shape):
        return pl.BlockSpec(shape, const, pipeline_mode=pl.Buffered(1))

    return pl.pallas_call(
        _tail_kernel,
        out_shape=jax.ShapeDtypeStruct((n, D_MODEL), F32),
        grid=(n // tm,),
        in_specs=[half, half, full,
                  half, half, half, half,
                  pl.BlockSpec((tm, D_MODEL), lambda i: (i, 1)),
                  full,
                  resident((D_S5, D_S5)), resident((D_S5, D_MODEL)),
                  resident((1, D_RWKV)), resident((1, D_RWKV)),
                  resident((D_RWKV, 128)), resident((128, D_RWKV)),
                  resident((D_RWKV, D_MODEL)), resident((D_MODEL, D_MODEL)),
                  resident((1, D_MODEL))],
        out_specs=full,
        compiler_params=_cparams(("parallel",)),
        name="tail",
    )(y_s5, s5_gate, merge, yf, yb, bonus, rw_gate, merge, x,
      p['w_glu'], p['s5_w_out'], p['ln_w'], p['ln_b'], p['e'], p['et'], p['rw_w_out'], p['w_o'],
      p['post_g'])


def _pick(n, pref):
    while n % pref:
        pref //= 2
    return pref


def _prepare(pre_norm_g, post_norm_g, w_in, s5_lam_re, s5_lam_im, s5_log_dt, s5_b_re, s5_b_im,
             s5_c_re, s5_c_im, s5_d, s5_w_glu, s5_w_out, rw_mu, rw_w0, rw_w_up, rw_a0, rw_a_up,
             rw_k_k, rw_k_a, rw_r_k, rw_ln_w, rw_ln_b, rw_w_out, w_o):
    p = {}
    p['pre_g'] = pre_norm_g.astype(F32).reshape(1, D_MODEL)
    p['post_g'] = post_norm_g.astype(F32).reshape(1, D_MODEL)
    w = w_in.astype(BF16)
    rw_end = 2 * D_S5 + D_RW_IN
    p['w_in'] = jnp.concatenate(
        [w[:, :rw_end], jnp.zeros((D_MODEL, D_RW_PAD - D_RW_IN), BF16), w[:, rw_end:]], axis=1)
    p['s5_w1'], p['s5_q'], p['s5_a'] = _s5_weights(
        s5_lam_re, s5_lam_im, s5_log_dt, s5_b_re, s5_b_im, s5_c_re, s5_c_im, s5_d)
    p['w_glu'] = s5_w_glu.astype(BF16)
    p['s5_w_out'] = s5_w_out.astype(BF16)
    p['mu'] = rw_mu.astype(F32).reshape(1, D_RW_IN)
    zeros = jnp.zeros((LORA_RANK, D_RWKV), F32)

    def blockdiag(u):
        return jnp.concatenate([jnp.concatenate([u[0], zeros], axis=1),
                                jnp.concatenate([zeros, u[1]], axis=1)], axis=0).astype(BF16)

    p['wup'] = blockdiag(rw_w_up.astype(F32))
    p['aup'] = blockdiag(rw_a_up.astype(F32))
    p['w0'] = rw_w0.astype(F32).reshape(1, 2 * D_RWKV)
    p['a0'] = rw_a0.astype(F32).reshape(1, 2 * D_RWKV)
    p['k_k'] = rw_k_k.astype(F32).reshape(1, D_RWKV)
    p['k_a'] = rw_k_a.astype(F32).reshape(1, D_RWKV)
    p['r_k'] = rw_r_k.astype(F32).reshape(1, D_RWKV)
    p['ln_w'] = rw_ln_w.astype(F32).reshape(1, D_RWKV)
    p['ln_b'] = rw_ln_b.astype(F32).reshape(1, D_RWKV)
    p['rw_w_out'] = rw_w_out.astype(BF16)
    p['w_o'] = w_o.astype(BF16)
    e = _head_indicator()
    p['e'] = e
    p['et'] = e.T
    return p


def _s5_mix(u, p, nseq):
    n = u.shape[0]
    t, g, h = S5_CHUNK, S5_GROUPS, S5_GROUP
    r = n // t
    x2 = u.reshape(r, t, g, h).transpose(0, 2, 1, 3).reshape(r, g * t * h)
    rows_per_seq = r // nseq
    rb = _pick(r, 512)
    y0, pfr, pfi, pbr, pbi = _s5_chunk(x2, p['s5_w1'], rb)
    cb = _pick(rows_per_seq, 128)
    hfr, hfi, hbr, hbi = _s5_scan(p['s5_a'], pfr, pfi, pbr, pbi, nseq, cb, 1024)
    y2 = _s5_state_out(y0, hfr, hfi, hbr, hbi, p['s5_q'], rb)
    return y2.reshape(r, g, t, h).transpose(0, 2, 1, 3).reshape(n, g * h)


def _layer(x, p):
    bsz, seq, _ = x.shape
    n = bsz * seq
    x2 = x.reshape(n, D_MODEL)
    u, s5_gate, rw_in, rw_gate, merge = _in_proj(x2, p['pre_g'], p['w_in'], _pick(n, 1024))

    y_s5 = _s5_mix(u, p, bsz)

    (r, kk, v, kdf, kdb, bf, bb, lwf, lwb, bonus) = _rw_pre(
        rw_in, p['mu'], p['wup'], p['aup'], p['w0'], p['a0'], p['k_k'], p['k_a'], p['r_k'],
        p['e'], p['et'], _pick(seq, 256), seq)
    yf = _wkv(r, kk, v, kdf, bf, lwf, bsz, False)
    yb = _wkv(r, kk, v, kdb, bb, lwb, bsz, True)

    out = _tail(y_s5, s5_gate, merge, yf, yb, bonus, rw_gate, x2, p, _pick(n, 256))
    return out.reshape(bsz, seq, D_MODEL)


def kernel(x_prompt, x_sample, pre_norm_g, post_norm_g, w_in, s5_lam_re, s5_lam_im, s5_log_dt, s5_b_re, s5_b_im, s5_c_re, s5_c_im, s5_d, s5_w_glu, s5_w_out, rw_mu, rw_w0, rw_w_up, rw_a0, rw_a_up, rw_k_k, rw_k_a, rw_r_k, rw_ln_w, rw_ln_b, rw_w_out, w_o):
    params = (pre_norm_g, post_norm_g, w_in, s5_lam_re, s5_lam_im, s5_log_dt, s5_b_re, s5_b_im,
              s5_c_re, s5_c_im, s5_d, s5_w_glu, s5_w_out, rw_mu, rw_w0, rw_w_up, rw_a0, rw_a_up,
              rw_k_k, rw_k_a, rw_r_k, rw_ln_w, rw_ln_b, rw_w_out, w_o)
    y_prompt, y_sample = x_prompt, x_sample
    for layer in range(w_in.shape[0]):
        p = _prepare(*[w[layer] for w in params])
        y_prompt = _layer(y_prompt, p)
        y_sample = _layer(y_sample, p)
    return (y_prompt, y_sample)
```

```python
import functools
import math

import jax
import jax.numpy as jnp
from jax import lax
from jax.experimental import pallas as pl
from jax.experimental.pallas import tpu as pltpu

F32 = jnp.float32
BF16 = jnp.bfloat16

D_MODEL = 2048
D_S5 = 1024
S5_GROUP = 16
S5_GROUPS = 64
S5_STATE = 64
D_RWKV = 1024
RWKV_HEAD = 64
RWKV_HEADS = 16
LORA_RANK = 64
D_RW_IN = 3 * D_RWKV + 4 * LORA_RANK
RMS_EPS = 1e-6
GN_EPS = 64e-5
L2_EPS = 1e-12

S5_CHUNK = 16
S5_PAIR = 2 * S5_CHUNK * S5_GROUP
WKV_CHUNK = 64
HEAD_PAIR = 2 * RWKV_HEAD
NEUMANN_STEPS = 5
WKV_PAIRS_PER_STEP = 8
WKV_SEQS = 2
HALO = 16

VMEM_LIMIT = 48 * 1024 * 1024


def _cparams(sem):
    return pltpu.CompilerParams(dimension_semantics=sem, vmem_limit_bytes=VMEM_LIMIT)


def _dot(a, b):
    return jnp.dot(a, b, preferred_element_type=F32)


def _dot_nt(a, b):
    return lax.dot_general(a, b, (((1,), (1,)), ((), ())), preferred_element_type=F32)


def _dot_tn(a, b):
    return lax.dot_general(a, b, (((0,), (0,)), ((), ())), preferred_element_type=F32)


def _dot_split(a, b_bf16):
    hi = a.astype(BF16)
    lo = (a - hi.astype(F32)).astype(BF16)
    return _dot(hi, b_bf16) + _dot(lo, b_bf16)


def _sigmoid(x):
    return 1.0 / (1.0 + jnp.exp(-x))


def _silu(x):
    return x * _sigmoid(x)


def _gelu_tanh(x):
    c = math.sqrt(2.0 / math.pi)
    return 0.5 * x * (1.0 + jnp.tanh(c * (x + 0.044715 * (x * x * x))))


IN_TN = 512
D_RW_PAD = 3584
IN_SEGMENTS = (('u', D_S5), ('s5_gate', D_S5), ('rw_in', D_RW_PAD), ('rw_gate', D_RWKV),
               ('merge', 2 * D_MODEL))


def _in_proj_kernel(x_ref, g_ref, w_ref, *refs):
    out_refs, hn_ref = refs[:-1], refs[-1]
    j = pl.program_id(1)

    @pl.when(j == 0)
    def _():
        x = x_ref[...]
        ms = jnp.mean(x * x, axis=-1, keepdims=True)
        hn_ref[...] = (x * lax.rsqrt(ms + RMS_EPS) * g_ref[...]).astype(BF16)

    res = _dot(hn_ref[...], w_ref[...]).astype(BF16)
    start = 0
    for ref, (_, width) in zip(out_refs, IN_SEGMENTS):
        stop = start + width // IN_TN

        @pl.when((j >= start) & (j < stop))
        def _(ref=ref):
            ref[...] = res

        start = stop


def _in_proj(x, g, w, tm):
    n, d = x.shape
    ntiles = w.shape[1] // IN_TN
    out_shape, out_specs, start = [], [], 0
    for _, width in IN_SEGMENTS:
        nt = width // IN_TN
        out_shape.append(jax.ShapeDtypeStruct((n, width), BF16))
        out_specs.append(pl.BlockSpec(
            (tm, IN_TN), lambda i, j, start=start, nt=nt: (i, jnp.clip(j - start, 0, nt - 1))))
        start += nt
    return pl.pallas_call(
        _in_proj_kernel,
        out_shape=tuple(out_shape),
        grid=(n // tm, ntiles),
        in_specs=[pl.BlockSpec((tm, d), lambda i, j: (i, 0)),
                  pl.BlockSpec((1, d), lambda i, j: (0, 0)),
                  pl.BlockSpec((d, IN_TN), lambda i, j: (0, j))],
        out_specs=tuple(out_specs),
        scratch_shapes=[pltpu.VMEM((tm, d), BF16)],
        compiler_params=_cparams(("parallel", "arbitrary")),
        name="in_proj",
    )(x, g, w)


def _s5_weights(lam_re, lam_im, log_dt, b_re, b_im, c_re, c_im, d):
    t = S5_CHUNK
    g, p, h = S5_GROUPS, S5_STATE, S5_GROUP
    hp = lax.Precision.HIGHEST
    lam = lax.complex(lam_re.astype(F32), lam_im.astype(F32))
    dt = jnp.exp(log_dt.astype(F32))[..., None]
    lam_dt = lam * dt
    lam_bar = jnp.exp(lam_dt)
    bbar = ((lam_bar - 1.0) / lam)[..., None] * lax.complex(b_re.astype(F32), b_im.astype(F32))
    c = lax.complex(c_re.astype(F32), c_im.astype(F32))
    steps = jnp.arange(t + 1, dtype=F32)
    pw = jnp.exp(lam_dt[None] * steps[:, None, None, None])

    kern = jnp.einsum('dghp,ldgp,dgpj->dglhj', c, pw[:t], bbar, precision=hp).real
    tt = jnp.arange(t)
    lag = tt[None, :] - tt[:, None]
    kf = kern[0][:, jnp.abs(lag)]
    kb = kern[1][:, jnp.abs(lag)]
    k0 = kern[0][:, 0] + kern[1][:, 0] + jnp.eye(h, dtype=F32) * d.astype(F32).reshape(g, h)[:, :, None]
    lag5 = lag[None, :, :, None, None]
    m = jnp.where(lag5 > 0, kf, jnp.where(lag5 < 0, kb, k0[:, None, None]))
    m = m.transpose(0, 1, 4, 2, 3).reshape(g, t * h, t * h)

    pf = pw[t - 1 - tt, 0][..., None] * bbar[0][None]
    pb = pw[tt, 1][..., None] * bbar[1][None]

    def _p_mat(x):
        return x.transpose(1, 0, 3, 2).reshape(g, t * h, p)

    p_parts = [_p_mat(pf.real), _p_mat(pf.imag), _p_mat(pb.real), _p_mat(pb.imag)]

    cf = c[0][None] * pw[tt + 1, 0][:, :, None, :]
    cb = c[1][None] * pw[t - tt, 1][:, :, None, :]

    def _q_mat(x):
        return x.transpose(1, 3, 0, 2).reshape(g, p, t * h)

    q_parts = [_q_mat(cf.real), _q_mat(-cf.imag), _q_mat(cb.real), _q_mat(-cb.imag)]

    gp = g // 2
    eye2 = jnp.eye(2, dtype=F32)

    def _pair(x, r, cdim):
        x = x.reshape(gp, 2, r, cdim)
        return jnp.einsum('gqrc,qs->gqrsc', x, eye2).reshape(gp, 2 * r, 2 * cdim)

    w1 = jnp.concatenate([_pair(m, t * h, t * h)] + [_pair(x, t * h, p) for x in p_parts], axis=2)
    qmat = jnp.concatenate([_pair(x, p, t * h) for x in q_parts], axis=1)
    at = pw[t]
    a_rows = jnp.stack([at[0].real, at[0].imag, at[1].real, at[1].imag]).reshape(4, g * p)
    return w1.astype(BF16), qmat.astype(BF16), a_rows


def _s5_chunk_kernel(x_ref, w_ref, y_ref, pfr_ref, pfi_ref, pbr_ref, pbi_ref):
    res = _dot(x_ref[...], w_ref[0])
    y_ref[...] = res[:, :S5_PAIR]
    for k, ref in enumerate((pfr_ref, pfi_ref, pbr_ref, pbi_ref)):
        ref[...] = res[:, S5_PAIR + 128 * k:S5_PAIR + 128 * (k + 1)]


def _s5_chunk(x2, w1, rb):
    r = x2.shape[0]
    gp = S5_GROUPS // 2
    st = jax.ShapeDtypeStruct((r, gp * 128), F32)
    pspec = pl.BlockSpec((rb, 128), lambda g, i: (i, g))
    return pl.pallas_call(
        _s5_chunk_kernel,
        out_shape=(jax.ShapeDtypeStruct((r, gp * S5_PAIR), F32), st, st, st, st),
        grid=(gp, r // rb),
        in_specs=[pl.BlockSpec((rb, S5_PAIR), lambda g, i: (i, g)),
                  pl.BlockSpec((1, S5_PAIR, 2 * S5_PAIR), lambda g, i: (g, 0, 0))],
        out_specs=(pl.BlockSpec((rb, S5_PAIR), lambda g, i: (i, g)), pspec, pspec, pspec, pspec),
        compiler_params=_cparams(("parallel", "arbitrary")),
        name="s5_chunk",
    )(x2, w1)


def _s5_scan_kernel(a_ref, pfr_ref, pfi_ref, pbr_ref, pbi_ref,
                    hfr_ref, hfi_ref, hbr_ref, hbi_ref, carry_ref, *, rows):
    @pl.when(pl.program_id(2) == 0)
    def _():
        carry_ref[...] = jnp.zeros_like(carry_ref)

    afr, afi = a_ref[0:1, :], a_ref[1:2, :]
    abr, abi = a_ref[2:3, :], a_ref[3:4, :]

    def tile(i, carry):
        hfr, hfi, hbr, hbi = carry
        r0 = pl.multiple_of(i * 8, 8)
        pfr, pfi = pfr_ref[pl.ds(r0, 8), :], pfi_ref[pl.ds(r0, 8), :]
        rb0 = pl.multiple_of(rows - 8 - i * 8, 8)
        pbr, pbi = pbr_ref[pl.ds(rb0, 8), :], pbi_ref[pl.ds(rb0, 8), :]
        of_r, of_i, ob_r, ob_i = [], [], [None] * 8, [None] * 8
        for s in range(8):
            of_r.append(hfr)
            of_i.append(hfi)
            hfr, hfi = (afr * hfr - afi * hfi + pfr[s:s + 1, :],
                        afr * hfi + afi * hfr + pfi[s:s + 1, :])
            sb = 7 - s
            ob_r[sb] = hbr
            ob_i[sb] = hbi
            hbr, hbi = (abr * hbr - abi * hbi + pbr[sb:sb + 1, :],
                        abr * hbi + abi * hbr + pbi[sb:sb + 1, :])
        hfr_ref[pl.ds(r0, 8), :] = jnp.concatenate(of_r, axis=0)
        hfi_ref[pl.ds(r0, 8), :] = jnp.concatenate(of_i, axis=0)
        hbr_ref[pl.ds(rb0, 8), :] = jnp.concatenate(ob_r, axis=0)
        hbi_ref[pl.ds(rb0, 8), :] = jnp.concatenate(ob_i, axis=0)
        return hfr, hfi, hbr, hbi

    init = tuple(carry_ref[k:k + 1, :] for k in range(4))
    out = lax.fori_loop(0, rows // 8, tile, init)
    for k in range(4):
        carry_ref[k:k + 1, :] = out[k]


def _s5_scan(a_rows, pfr, pfi, pbr, pbi, nseq, cb, lt):
    r, w = pfr.shape
    nblk = r // nseq // cb
    fspec = pl.BlockSpec((cb, lt), lambda b, j, i: (b * nblk + i, j))
    bspec = pl.BlockSpec((cb, lt), lambda b, j, i: (b * nblk + nblk - 1 - i, j))
    st = jax.ShapeDtypeStruct((r, w), F32)
    return pl.pallas_call(
        functools.partial(_s5_scan_kernel, rows=cb),
        out_shape=(st, st, st, st),
        grid=(nseq, w // lt, nblk),
        in_specs=[pl.BlockSpec((4, lt), lambda b, j, i: (0, j)), fspec, fspec, bspec, bspec],
        out_specs=(fspec, fspec, bspec, bspec),
        scratch_shapes=[pltpu.VMEM((4, lt), F32)],
        compiler_params=_cparams(("parallel", "parallel", "arbitrary")),
        name="s5_scan",
    )(a_rows, pfr, pfi, pbr, pbi)


def _s5_state_out_kernel(y0_ref, hfr_ref, hfi_ref, hbr_ref, hbi_ref, q_ref, y_ref):
    h = jnp.concatenate([hfr_ref[...], hfi_ref[...], hbr_ref[...], hbi_ref[...]], axis=1)
    y_ref[...] = (y0_ref[...] + _dot(h.astype(BF16), q_ref[0])).astype(BF16)


def _s5_state_out(y0, hfr, hfi, hbr, hbi, qmat, rb):
    r = y0.shape[0]
    gp = S5_GROUPS // 2
    hspec = pl.BlockSpec((rb, 128), lambda g, i: (i, g))
    yspec = pl.BlockSpec((rb, S5_PAIR), lambda g, i: (i, g))
    return pl.pallas_call(
        _s5_state_out_kernel,
        out_shape=jax.ShapeDtypeStruct(y0.shape, BF16),
        grid=(gp, r // rb),
        in_specs=[yspec, hspec, hspec, hspec, hspec,
                  pl.BlockSpec((1, S5_PAIR, S5_PAIR), lambda g, i: (g, 0, 0))],
        out_specs=yspec,
        compiler_params=_cparams(("parallel", "arbitrary")),
        name="s5_state_out",
    )(y0, hfr, hfi, hbr, hbi, qmat)


def _head_indicator():
    lane = lax.broadcasted_iota(jnp.int32, (D_RWKV, 128), 0) // RWKV_HEAD
    col = lax.broadcasted_iota(jnp.int32, (D_RWKV, 128), 1)
    return (lane == col).astype(BF16)


def _head_sum_bcast(x, e, et):
    return _dot_split(_dot_split(x, e), et)


def _rw_pre_kernel(x_ref, prev_ref, next_ref, mu_ref, wup_ref, aup_ref, w0_ref, a0_ref,
                   kk_ref_p, ka_ref_p, rk_ref_p, e_ref, et_ref,
                   r_ref, kk_ref, v_ref, kdf_ref, kdb_ref, bf_ref, bb_ref, lwf_ref, lwb_ref,
                   bonus_ref, *, tm, seq):
    i = pl.program_id(0)
    x = x_ref[...].astype(F32)
    first = (i * tm) % seq == 0
    last = ((i + 1) * tm) % seq == 0
    prev_row = jnp.where(first, 0.0, prev_ref[HALO - 1:HALO, :].astype(F32))
    next_row = jnp.where(last, 0.0, next_ref[0:1, :].astype(F32))
    row = lax.broadcasted_iota(jnp.int32, x.shape, 0)
    prev = jnp.where(row == 0, prev_row, pltpu.roll(x, 1, 0))
    nxt = jnp.where(row == tm - 1, next_row, pltpu.roll(x, tm - 1, 0))
    z = x + (0.5 * (prev + nxt) - x) * mu_ref[...]

    r = z[:, :D_RWKV]
    k = z[:, D_RWKV:2 * D_RWKV]
    v = z[:, 2 * D_RWKV:3 * D_RWKV]
    xw = z[:, 3 * D_RWKV:3 * D_RWKV + 2 * LORA_RANK]
    xa = z[:, 3 * D_RWKV + 2 * LORA_RANK:]
    e, et = e_ref[...], et_ref[...]

    kk = k * kk_ref_p[...]
    kk = kk * lax.rsqrt(_head_sum_bcast(kk * kk, e, et) + L2_EPS)
    bonus_ref[...] = (_head_sum_bcast(r * k * rk_ref_p[...], e, et) * v).astype(BF16)
    r_ref[...] = r.astype(BF16)
    kk_ref[...] = kk.astype(BF16)
    v_ref[...] = v.astype(BF16)

    wpre = _dot(jnp.tanh(xw).astype(BF16), wup_ref[...]) + w0_ref[...]
    apre = _dot(xa.astype(BF16), aup_ref[...]) + a0_ref[...]
    lw = -math.exp(-0.5) * _sigmoid(wpre)
    a = _sigmoid(apre)
    ka = ka_ref_p[...]
    lwf_ref[...] = lw[:, :D_RWKV]
    lwb_ref[...] = lw[:, D_RWKV:]
    af, ab = a[:, :D_RWKV], a[:, D_RWKV:]
    kdf_ref[...] = (k * (1.0 + (af - 1.0) * ka)).astype(BF16)
    kdb_ref[...] = (k * (1.0 + (ab - 1.0) * ka)).astype(BF16)
    bf_ref[...] = (kk * af).astype(BF16)
    bb_ref[...] = (kk * ab).astype(BF16)


def _rw_pre(rw_in, mu, wup, aup, w0, a0, k_k, k_a, r_k, e, et, tm, seq):
    n = rw_in.shape[0]
    w = D_RW_IN
    nbh = n // HALO
    th = tm // HALO
    row = lambda i: (i, 0)
    const = lambda i: (0, 0)
    ospec = pl.BlockSpec((tm, D_RWKV), row)
    sb = jax.ShapeDtypeStruct((n, D_RWKV), BF16)
    sf = jax.ShapeDtypeStruct((n, D_RWKV), F32)
    return pl.pallas_call(
        functools.partial(_rw_pre_kernel, tm=tm, seq=seq),
        out_shape=(sb,) * 7 + (sf, sf, sb),
        grid=(n // tm,),
        in_specs=[pl.BlockSpec((tm, w), row),
                  pl.BlockSpec((HALO, w), lambda i: (jnp.maximum(i * th - 1, 0), 0)),
                  pl.BlockSpec((HALO, w), lambda i: (jnp.minimum((i + 1) * th, nbh - 1), 0)),
                  pl.BlockSpec((1, w), const),
                  pl.BlockSpec((2 * LORA_RANK, 2 * D_RWKV), const),
                  pl.BlockSpec((2 * LORA_RANK, 2 * D_RWKV), const),
                  pl.BlockSpec((1, 2 * D_RWKV), const),
                  pl.BlockSpec((1, 2 * D_RWKV), const),
                  pl.BlockSpec((1, D_RWKV), const),
                  pl.BlockSpec((1, D_RWKV), const),
                  pl.BlockSpec((1, D_RWKV), const),
                  pl.BlockSpec((D_RWKV, 128), const),
                  pl.BlockSpec((128, D_RWKV), const)],
        out_specs=(ospec,) * 10,
        compiler_params=_cparams(("parallel",)),
        name="rw_pre",
    )(rw_in, rw_in, rw_in, mu, wup, aup, w0, a0, k_k, k_a, r_k, e, et)


def _wkv_kernel(r_ref, kk_ref, v_ref, kd_ref, b_ref, lw_ref, y_ref, s_ref, *, rev):
    t = WKV_CHUNK
    hd = RWKV_HEAD
    w = HEAD_PAIR

    @pl.when(pl.program_id(1) == 0)
    def _():
        s_ref[...] = jnp.zeros_like(s_ref)

    def iota(shape, dim):
        return lax.broadcasted_iota(jnp.int32, shape, dim)

    rr, cc = iota((t, t), 0), iota((t, t), 1)
    tri = ((rr <= cc) if rev else (rr >= cc)).astype(BF16)
    rw_, lane = iota((t, w), 0), iota((t, w), 1)
    col = lane % hd
    m0 = lane < hd
    if rev:
        incl, strict = rw_ <= col, rw_ < col
    else:
        incl, strict = rw_ >= col, rw_ > col
    eye_f = (rw_ == col).astype(F32)
    m0_2t = iota((2 * t, w), 1) < hd
    mask_2t = jnp.concatenate([strict, incl], axis=0)
    eye_w = iota((w, w), 0) == iota((w, w), 1)
    bd_mask = (iota((w, w), 0) < hd) == (iota((w, w), 1) < hd)
    t_last = 0 if rev else t - 1

    def bdiag(x):
        return jnp.concatenate([jnp.where(m0, x, 0.0), jnp.where(m0, 0.0, x)], axis=0)

    def group(gi, carry):
        hv = []
        for sq, j in [(sq, j) for sq in range(r_ref.shape[0]) for j in range(WKV_PAIRS_PER_STEP)]:
            hp = gi * WKV_PAIRS_PER_STEP + j
            sl = pl.ds(pl.multiple_of(hp * HEAD_PAIR, HEAD_PAIR), HEAD_PAIR)
            r, kk, v = (ref[sq, :, sl].astype(F32) for ref in (r_ref, kk_ref, v_ref))
            kd, beta, lw = kd_ref[sq, :, sl].astype(F32), b_ref[sq, :, sl].astype(F32), lw_ref[sq, :, sl]
            lw_hi = lw.astype(BF16)
            lw_lo = (lw - lw_hi.astype(F32)).astype(BF16)
            lcum = _dot(tri, lw_hi) + _dot(tri, lw_lo)
            lcum_x = lcum - lw
            cref = lcum[t // 2:t // 2 + 1, :]
            ltot = lcum[t_last:t_last + 1, :]
            e1 = jnp.exp(lcum - cref)
            e1x = jnp.exp(lcum_x - cref)
            e2 = jnp.exp(cref - lcum)
            ec = jnp.exp(cref)
            ewt = jnp.exp(ltot - cref)
            wtot = jnp.exp(ltot)
            r_t = r * e1
            a_t = -kk * e1x
            b_t = beta * e2
            k_t = kd * e2
            a_0 = a_t * ec
            r_0 = r_t * ec
            b_h = b_t * ewt
            k_h = k_t * ewt
            hv.append(dict(
                sq=sq, hp=hp, sl=sl, wtot=wtot, a_0=a_0, r_0=r_0,
                lm=jnp.concatenate([a_t, r_t], axis=0),
                rma=jnp.concatenate([b_t, k_t], axis=0).astype(BF16),
                rmb=jnp.concatenate([k_t, b_t], axis=0).astype(BF16),
                bk=jnp.concatenate([b_h, k_h], axis=0).astype(BF16), v=v))
        for h in hv:
            g0 = _dot_nt(jnp.where(m0_2t, h['lm'], 0.0).astype(BF16), h['rma'])
            g1 = _dot_nt(jnp.where(m0_2t, 0.0, h['lm']).astype(BF16), h['rmb'])
            own = jnp.where(m0_2t, g0, g1)
            oth = jnp.where(m0_2t, g1, g0)
            h['a_ab'] = jnp.where(strict, own[:t], 0.0)
            h['a_rb'] = jnp.where(incl, own[t:], 0.0).astype(BF16)
            h['a_k'] = jnp.where(mask_2t, oth, 0.0).astype(BF16)
        for h in hv:
            v = h['v']
            h['inv'] = eye_f + h['a_ab']
            h['nk'] = _dot(h['a_ab'].astype(BF16), bdiag(h['a_ab']).astype(BF16))
            v_rows = jnp.concatenate([jnp.where(m0, 0.0, v), jnp.where(m0, v, 0.0)], axis=0)
            h['av'] = _dot(h['a_k'], v_rows.astype(BF16))
        for s in range(1, NEUMANN_STEPS + 1):
            for h in hv:
                nk_bd = bdiag(h['nk']).astype(BF16)
                if s < NEUMANN_STEPS:
                    nx = _dot(jnp.concatenate([h['nk'], h['inv']], axis=0).astype(BF16), nk_bd)
                    h['nk'] = nx[:t]
                    h['inv'] = h['inv'] + nx[t:]
                else:
                    h['inv'] = h['inv'] + _dot(h['inv'].astype(BF16), nk_bd)
        for h in hv:
            rhs = jnp.concatenate([bdiag(h['a_0']), bdiag(h['av'][:t])], axis=1)
            h['pp'] = _dot(h['inv'].astype(BF16), rhs.astype(BF16))
        for h in hv:
            pp = h['pp']
            rhs = jnp.concatenate([bdiag(pp[:, :w]), bdiag(pp[:, w:])], axis=1)
            h['qq'] = (_dot(h['a_rb'], rhs.astype(BF16))
                       + jnp.concatenate([h['r_0'], h['av'][t:]], axis=1))
            pv = jnp.concatenate([pp, jnp.concatenate([jnp.zeros_like(h['v']), h['v']], axis=1)], axis=0)
            h['mn'] = _dot_tn(h['bk'], pv.astype(BF16))
        for h in hv:
            st = s_ref[h['sq'], h['hp']]
            st_b = st.astype(BF16)
            y_ref[h['sq'], :, h['sl']] = _dot(h['qq'][:, :w].astype(BF16), st_b) + h['qq'][:, w:]
            wcol = jnp.sum(jnp.where(eye_w, jnp.broadcast_to(h['wtot'], (w, w)), 0.0),
                           axis=1, keepdims=True)
            s_ref[h['sq'], h['hp']] = (
                wcol * st + _dot(jnp.where(bd_mask, h['mn'][:, :w], 0.0).astype(BF16), st_b)
                + jnp.where(bd_mask, h['mn'][:, w:], 0.0))
        return carry

    lax.fori_loop(0, RWKV_HEADS // (2 * WKV_PAIRS_PER_STEP), group, 0)


def _wkv(r, kk, v, kd, beta, lw, nseq, rev):
    n = r.shape[0]
    t = WKV_CHUNK
    seq = n // nseq
    nc = seq // t
    ns = WKV_SEQS if nseq % WKV_SEQS == 0 else 1
    if rev:
        idx = lambda b, c: (b, nc - 1 - c, 0)
    else:
        idx = lambda b, c: (b, c, 0)
    spec = pl.BlockSpec((ns, t, D_RWKV), idx)
    y = pl.pallas_call(
        functools.partial(_wkv_kernel, rev=rev),
        out_shape=jax.ShapeDtypeStruct((nseq, seq, D_RWKV), F32),
        grid=(nseq // ns, nc),
        in_specs=[spec] * 6,
        out_specs=spec,
        scratch_shapes=[pltpu.VMEM((ns, RWKV_HEADS // 2, HEAD_PAIR, HEAD_PAIR), F32)],
        compiler_params=_cparams(("parallel", "arbitrary")),
        name="wkv_bwd" if rev else "wkv_fwd",
    )(*[a.reshape(nseq, seq, D_RWKV) for a in (r, kk, v, kd, beta, lw)])
    return y.reshape(n, D_RWKV)


def _tail_kernel(ys_ref, sg_ref, ms_ref, yf_ref, yb_ref, bonus_ref, rg_ref, mr_ref, x_ref,
                 wglu_ref, swout_ref, lnw_ref, lnb_ref, e_ref, et_ref, rwout_ref, wo_ref, g_ref,
                 o_ref):
    f32 = lambda ref: ref[...].astype(F32)
    z = _gelu_tanh(f32(ys_ref))
    z = z * _sigmoid(_dot(z.astype(BF16), wglu_ref[...]))
    z = z * _silu(f32(sg_ref))
    h = _sigmoid(f32(ms_ref)) * _dot(z.astype(BF16), swout_ref[...])

    e, et = e_ref[...], et_ref[...]
    y = yf_ref[...] + yb_ref[...]
    inv_n = 1.0 / RWKV_HEAD
    mean = _head_sum_bcast(y, e, et) * inv_n
    yc = y - mean
    var = _head_sum_bcast(yc * yc, e, et) * inv_n
    y = yc * lax.rsqrt(var + GN_EPS) * lnw_ref[...] + lnb_ref[...] + f32(bonus_ref)
    y = y * _silu(f32(rg_ref))
    h = h + _sigmoid(f32(mr_ref)) * _dot(y.astype(BF16), rwout_ref[...])

    out = _dot(h.astype(BF16), wo_ref[...])
    ms = jnp.mean(out * out, axis=-1, keepdims=True)
    o_ref[...] = x_ref[...] + out * lax.rsqrt(ms + RMS_EPS) * g_ref[...]


def _tail(y_s5, s5_gate, merge, yf, yb, bonus, rw_gate, x, p, tm):
    n = x.shape[0]
    row = lambda i: (i, 0)
    const = lambda i: (0, 0)
    half = pl.BlockSpec((tm, D_RWKV), row)
    full = pl.BlockSpec((tm, D_MODEL), row)

    def resident(shape):
        return pl.BlockSpec(shape, const, pipeline_mode=pl.Buffered(1))

    return pl.pallas_call(
        _tail_kernel,
        out_shape=jax.ShapeDtypeStruct((n, D_MODEL), F32),
        grid=(n // tm,),
        in_specs=[half, half, full,
                  half, half, half, half,
                  pl.BlockSpec((tm, D_MODEL), lambda i: (i, 1)),
                  full,
                  resident((D_S5, D_S5)), resident((D_S5, D_MODEL)),
                  resident((1, D_RWKV)), resident((1, D_RWKV)),
                  resident((D_RWKV, 128)), resident((128, D_RWKV)),
                  resident((D_RWKV, D_MODEL)), resident((D_MODEL, D_MODEL)),
                  resident((1, D_MODEL))],
        out_specs=full,
        compiler_params=_cparams(("parallel",)),
        name="tail",
    )(y_s5, s5_gate, merge, yf, yb, bonus, rw_gate, merge, x,
      p['w_glu'], p['s5_w_out'], p['ln_w'], p['ln_b'], p['e'], p['et'], p['rw_w_out'], p['w_o'],
      p['post_g'])


def _pick(n, pref):
    while n % pref:
        pref //= 2
    return pref


def _prepare(pre_norm_g, post_norm_g, w_in, s5_lam_re, s5_lam_im, s5_log_dt, s5_b_re, s5_b_im,
             s5_c_re, s5_c_im, s5_d, s5_w_glu, s5_w_out, rw_mu, rw_w0, rw_w_up, rw_a0, rw_a_up,
             rw_k_k, rw_k_a, rw_r_k, rw_ln_w, rw_ln_b, rw_w_out, w_o):
    p = {}
    p['pre_g'] = pre_norm_g.astype(F32).reshape(1, D_MODEL)
    p['post_g'] = post_norm_g.astype(F32).reshape(1, D_MODEL)
    w = w_in.astype(BF16)
    rw_end = 2 * D_S5 + D_RW_IN
    p['w_in'] = jnp.concatenate(
        [w[:, :rw_end], jnp.zeros((D_MODEL, D_RW_PAD - D_RW_IN), BF16), w[:, rw_end:]], axis=1)
    p['s5_w1'], p['s5_q'], p['s5_a'] = _s5_weights(
        s5_lam_re, s5_lam_im, s5_log_dt, s5_b_re, s5_b_im, s5_c_re, s5_c_im, s5_d)
    p['w_glu'] = s5_w_glu.astype(BF16)
    p['s5_w_out'] = s5_w_out.astype(BF16)
    p['mu'] = rw_mu.astype(F32).reshape(1, D_RW_IN)
    zeros = jnp.zeros((LORA_RANK, D_RWKV), F32)

    def blockdiag(u):
        return jnp.concatenate([jnp.concatenate([u[0], zeros], axis=1),
                                jnp.concatenate([zeros, u[1]], axis=1)], axis=0).astype(BF16)

    p['wup'] = blockdiag(rw_w_up.astype(F32))
    p['aup'] = blockdiag(rw_a_up.astype(F32))
    p['w0'] = rw_w0.astype(F32).reshape(1, 2 * D_RWKV)
    p['a0'] = rw_a0.astype(F32).reshape(1, 2 * D_RWKV)
    p['k_k'] = rw_k_k.astype(F32).reshape(1, D_RWKV)
    p['k_a'] = rw_k_a.astype(F32).reshape(1, D_RWKV)
    p['r_k'] = rw_r_k.astype(F32).reshape(1, D_RWKV)
    p['ln_w'] = rw_ln_w.astype(F32).reshape(1, D_RWKV)
    p['ln_b'] = rw_ln_b.astype(F32).reshape(1, D_RWKV)
    p['rw_w_out'] = rw_w_out.astype(BF16)
    p['w_o'] = w_o.astype(BF16)
    e = _head_indicator()
    p['e'] = e
    p['et'] = e.T
    return p


def _s5_mix(u, p, nseq):
    n = u.shape[0]
    t, g, h = S5_CHUNK, S5_GROUPS, S5_GROUP
    r = n // t
    x2 = u.reshape(r, t, g, h).transpose(0, 2, 1, 3).reshape(r, g * t * h)
    rows_per_seq = r // nseq
    rb = _pick(r, 512)
    y0, pfr, pfi, pbr, pbi = _s5_chunk(x2, p['s5_w1'], rb)
    cb = _pick(rows_per_seq, 128)
    hfr, hfi, hbr, hbi = _s5_scan(p['s5_a'], pfr, pfi, pbr, pbi, nseq, cb, 1024)
    y2 = _s5_state_out(y0, hfr, hfi, hbr, hbi, p['s5_q'], rb)
    return y2.reshape(r, g, t, h).transpose(0, 2, 1, 3).reshape(n, g * h)


def _layer(x, p):
    bsz, seq, _ = x.shape
    n = bsz * seq
    x2 = x.reshape(n, D_MODEL)
    u, s5_gate, rw_in, rw_gate, merge = _in_proj(x2, p['pre_g'], p['w_in'], _pick(n, 1024))

    y_s5 = _s5_mix(u, p, bsz)

    (r, kk, v, kdf, kdb, bf, bb, lwf, lwb, bonus) = _rw_pre(
        rw_in, p['mu'], p['wup'], p['aup'], p['w0'], p['a0'], p['k_k'], p['k_a'], p['r_k'],
        p['e'], p['et'], _pick(seq, 256), seq)
    yf = _wkv(r, kk, v, kdf, bf, lwf, bsz, False)
    yb = _wkv(r, kk, v, kdb, bb, lwb, bsz, True)

    out = _tail(y_s5, s5_gate, merge, yf, yb, bonus, rw_gate, x2, p, _pick(n, 256))
    return out.reshape(bsz, seq, D_MODEL)


def kernel(x_prompt, x_sample, pre_norm_g, post_norm_g, w_in, s5_lam_re, s5_lam_im, s5_log_dt, s5_b_re, s5_b_im, s5_c_re, s5_c_im, s5_d, s5_w_glu, s5_w_out, rw_mu, rw_w0, rw_w_up, rw_a0, rw_a_up, rw_k_k, rw_k_a, rw_r_k, rw_ln_w, rw_ln_b, rw_w_out, w_o):
    params = (pre_norm_g, post_norm_g, w_in, s5_lam_re, s5_lam_im, s5_log_dt, s5_b_re, s5_b_im,
              s5_c_re, s5_c_im, s5_d, s5_w_glu, s5_w_out, rw_mu, rw_w0, rw_w_up, rw_a0, rw_a_up,
              rw_k_k, rw_k_a, rw_r_k, rw_ln_w, rw_ln_b, rw_w_out, w_o)
    y_prompt, y_sample = x_prompt, x_sample
    for layer in range(w_in.shape[0]):
        p = _prepare(*[w[layer] for w in params])
        y_prompt = _layer(y_prompt, p)
        y_sample = _layer(y_sample, p)
    return (y_prompt, y_sample)
```

```python
import functools
import math

import jax
import jax.numpy as jnp
from jax import lax
from jax.experimental import pallas as pl
from jax.experimental.pallas import tpu as pltpu

F32 = jnp.float32
BF16 = jnp.bfloat16

D_MODEL = 2048
D_S5 = 1024
S5_GROUP = 16
S5_GROUPS = 64
S5_STATE = 64
D_RWKV = 1024
RWKV_HEAD = 64
RWKV_HEADS = 16
LORA_RANK = 64
D_RW_IN = 3 * D_RWKV + 4 * LORA_RANK
RMS_EPS = 1e-6
GN_EPS = 64e-5
L2_EPS = 1e-12

S5_CHUNK = 16
S5_OCT_GROUPS = 8
S5_OCT = S5_OCT_GROUPS * S5_CHUNK * S5_GROUP
WKV_CHUNK = 64
HEAD_PAIR = 2 * RWKV_HEAD
NEUMANN_STEPS = 5
WKV_PAIRS_PER_STEP = 8
WKV_SEQS = 2
HALO = 16

VMEM_LIMIT = 48 * 1024 * 1024


def _cparams(sem):
    return pltpu.CompilerParams(dimension_semantics=sem, vmem_limit_bytes=VMEM_LIMIT)


def _dot(a, b):
    return jnp.dot(a, b, preferred_element_type=F32)


def _dot_nt(a, b):
    return lax.dot_general(a, b, (((1,), (1,)), ((), ())), preferred_element_type=F32)


def _dot_tn(a, b):
    return lax.dot_general(a, b, (((0,), (0,)), ((), ())), preferred_element_type=F32)


def _dot_split(a, b_bf16):
    hi = a.astype(BF16)
    lo = (a - hi.astype(F32)).astype(BF16)
    return _dot(hi, b_bf16) + _dot(lo, b_bf16)


def _sigmoid(x):
    return 1.0 / (1.0 + jnp.exp(-x))


def _silu(x):
    return x * _sigmoid(x)


def _gelu_tanh(x):
    c = math.sqrt(2.0 / math.pi)
    return 0.5 * x * (1.0 + jnp.tanh(c * (x + 0.044715 * (x * x * x))))


IN_TN = 512
D_RW_PAD = 3584
IN_SEGMENTS = (('u', D_S5), ('s5_gate', D_S5), ('rw_in', D_RW_PAD), ('rw_gate', D_RWKV),
               ('merge', 2 * D_MODEL))


def _in_proj_kernel(x_ref, g_ref, w_ref, *refs):
    out_refs, hn_ref, ubuf_ref = refs[:-2], refs[-2], refs[-1]
    j = pl.program_id(1)
    rows = ubuf_ref.shape[1] // S5_CHUNK

    @pl.when(j == 0)
    def _():
        x = x_ref[...]
        ms = jnp.mean(x * x, axis=-1, keepdims=True)
        hn_ref[...] = (x * lax.rsqrt(ms + RMS_EPS) * g_ref[...]).astype(BF16)

    res = _dot(hn_ref[...], w_ref[...])
    start = 0
    for ref, (name, width) in zip(out_refs, IN_SEGMENTS):
        stop = start + width // IN_TN

        if name == 'u':
            @pl.when(j < stop)
            def _(ref=ref):
                for o in range(IN_TN // 128):
                    ubuf_ref[o] = res[:, o * 128:(o + 1) * 128]
                    for t in range(S5_CHUNK):
                        tok = ubuf_ref[o, pl.ds(t, rows, stride=S5_CHUNK), :]
                        lane = o * S5_OCT + t * 128
                        ref[:, lane:lane + 128] = tok.astype(BF16)
        else:
            @pl.when((j >= start) & (j < stop))
            def _(ref=ref):
                ref[...] = res.astype(BF16)

        start = stop


def _in_proj(x, g, w, tm):
    n, d = x.shape
    ntiles = w.shape[1] // IN_TN
    out_shape, out_specs, start = [], [], 0
    for name, width in IN_SEGMENTS:
        nt = width // IN_TN
        clamp = lambda i, j, start=start, nt=nt: (i, jnp.clip(j - start, 0, nt - 1))
        if name == 'u':
            out_shape.append(jax.ShapeDtypeStruct((n // S5_CHUNK, width * S5_CHUNK), BF16))
            out_specs.append(pl.BlockSpec((tm // S5_CHUNK, IN_TN * S5_CHUNK), clamp))
        else:
            out_shape.append(jax.ShapeDtypeStruct((n, width), BF16))
            out_specs.append(pl.BlockSpec((tm, IN_TN), clamp))
        start += nt
    return pl.pallas_call(
        _in_proj_kernel,
        out_shape=tuple(out_shape),
        grid=(n // tm, ntiles),
        in_specs=[pl.BlockSpec((tm, d), lambda i, j: (i, 0)),
                  pl.BlockSpec((1, d), lambda i, j: (0, 0)),
                  pl.BlockSpec((d, IN_TN), lambda i, j: (0, j))],
        out_specs=tuple(out_specs),
        scratch_shapes=[pltpu.VMEM((tm, d), BF16), pltpu.VMEM((IN_TN // 128, tm, 128), F32)],
        compiler_params=_cparams(("parallel", "arbitrary")),
        name="in_proj",
    )(x, g, w)


def _s5_weights(lam_re, lam_im, log_dt, b_re, b_im, c_re, c_im, d):
    t = S5_CHUNK
    g, p, h = S5_GROUPS, S5_STATE, S5_GROUP
    hp = lax.Precision.HIGHEST
    lam = lax.complex(lam_re.astype(F32), lam_im.astype(F32))
    dt = jnp.exp(log_dt.astype(F32))[..., None]
    lam_dt = lam * dt
    lam_bar = jnp.exp(lam_dt)
    bbar = ((lam_bar - 1.0) / lam)[..., None] * lax.complex(b_re.astype(F32), b_im.astype(F32))
    c = lax.complex(c_re.astype(F32), c_im.astype(F32))
    steps = jnp.arange(t + 1, dtype=F32)
    pw = jnp.exp(lam_dt[None] * steps[:, None, None, None])

    kern = jnp.einsum('dghp,ldgp,dgpj->dglhj', c, pw[:t], bbar, precision=hp).real
    tt = jnp.arange(t)
    lag = tt[None, :] - tt[:, None]
    kf = kern[0][:, jnp.abs(lag)]
    kb = kern[1][:, jnp.abs(lag)]
    k0 = kern[0][:, 0] + kern[1][:, 0] + jnp.eye(h, dtype=F32) * d.astype(F32).reshape(g, h)[:, :, None]
    lag5 = lag[None, :, :, None, None]
    m = jnp.where(lag5 > 0, kf, jnp.where(lag5 < 0, kb, k0[:, None, None]))
    m = m.transpose(0, 1, 4, 2, 3).reshape(g, t * h, t * h)

    pf = pw[t - 1 - tt, 0][..., None] * bbar[0][None]
    pb = pw[tt, 1][..., None] * bbar[1][None]

    def _p_mat(x):
        return x.transpose(1, 0, 3, 2).reshape(g, t * h, p)

    p_parts = [_p_mat(pf.real), _p_mat(pf.imag), _p_mat(pb.real), _p_mat(pb.imag)]

    cf = c[0][None] * pw[tt + 1, 0][:, :, None, :]
    cb = c[1][None] * pw[t - tt, 1][:, :, None, :]

    def _q_mat(x):
        return x.transpose(1, 3, 0, 2).reshape(g, p, t * h)

    q_parts = [_q_mat(cf.real), _q_mat(-cf.imag), _q_mat(cb.real), _q_mat(-cb.imag)]

    k8 = S5_OCT_GROUPS
    no = g // k8
    eye8 = jnp.eye(k8, dtype=BF16)
    m_oct = jnp.einsum('ogtjuh,gk->otgjukh', m.astype(BF16).reshape(no, k8, t, h, t, h), eye8)
    m_oct = m_oct.reshape(no, S5_OCT, S5_OCT)

    def _p_oct(x):
        x = x.astype(BF16).reshape(no, k8, t, h, p)
        return jnp.einsum('ogtjn,gk->otgjkn', x, eye8).reshape(no, S5_OCT, k8 * p)

    def _q_oct(x):
        x = x.astype(BF16).reshape(no, k8, p, t, h)
        return jnp.einsum('ognuh,gk->ognukh', x, eye8).reshape(no, k8 * p, S5_OCT)

    w_p = jnp.concatenate([_p_oct(x) for x in p_parts], axis=2)
    qmat = jnp.concatenate([_q_oct(x) for x in q_parts], axis=1)
    at = pw[t]
    a_rows = jnp.stack([at[0].real, at[0].imag, at[1].real, at[1].imag]).reshape(4, g * p)
    return m_oct, w_p, qmat, a_rows


def _s5_mm_kernel(x_ref, w_ref, *o_refs):
    res = _dot(x_ref[...], w_ref[0])
    n = res.shape[1] // len(o_refs)
    for k, ref in enumerate(o_refs):
        ref[...] = res[:, n * k:n * (k + 1)].astype(ref.dtype)


def _s5_mm(x2, w, rb, nout, name):
    r = x2.shape[0]
    no, _, c = w.shape
    cw = c // nout
    return pl.pallas_call(
        _s5_mm_kernel,
        out_shape=tuple(jax.ShapeDtypeStruct((r, no * cw), F32) for _ in range(nout)),
        grid=(no, r // rb),
        in_specs=[pl.BlockSpec((rb, S5_OCT), lambda o, i: (i, o)),
                  pl.BlockSpec((1, S5_OCT, c), lambda o, i: (o, 0, 0))],
        out_specs=tuple(pl.BlockSpec((rb, cw), lambda o, i: (i, o)) for _ in range(nout)),
        compiler_params=_cparams(("parallel", "arbitrary")),
        name=name,
    )(x2, w)


def _s5_scan_kernel(a_ref, pfr_ref, pfi_ref, pbr_ref, pbi_ref,
                    hfr_ref, hfi_ref, hbr_ref, hbi_ref, carry_ref, *, rows):
    @pl.when(pl.program_id(2) == 0)
    def _():
        carry_ref[...] = jnp.zeros_like(carry_ref)

    afr, afi = a_ref[0:1, :], a_ref[1:2, :]
    abr, abi = a_ref[2:3, :], a_ref[3:4, :]

    def tile(i, carry):
        hfr, hfi, hbr, hbi = carry
        r0 = pl.multiple_of(i * 8, 8)
        pfr, pfi = pfr_ref[pl.ds(r0, 8), :], pfi_ref[pl.ds(r0, 8), :]
        rb0 = pl.multiple_of(rows - 8 - i * 8, 8)
        pbr, pbi = pbr_ref[pl.ds(rb0, 8), :], pbi_ref[pl.ds(rb0, 8), :]
        of_r, of_i, ob_r, ob_i = [], [], [None] * 8, [None] * 8
        for s in range(8):
            of_r.append(hfr)
            of_i.append(hfi)
            hfr, hfi = (afr * hfr - afi * hfi + pfr[s:s + 1, :],
                        afr * hfi + afi * hfr + pfi[s:s + 1, :])
            sb = 7 - s
            ob_r[sb] = hbr
            ob_i[sb] = hbi
            hbr, hbi = (abr * hbr - abi * hbi + pbr[sb:sb + 1, :],
                        abr * hbi + abi * hbr + pbi[sb:sb + 1, :])
        hfr_ref[pl.ds(r0, 8), :] = jnp.concatenate(of_r, axis=0)
        hfi_ref[pl.ds(r0, 8), :] = jnp.concatenate(of_i, axis=0)
        hbr_ref[pl.ds(rb0, 8), :] = jnp.concatenate(ob_r, axis=0)
        hbi_ref[pl.ds(rb0, 8), :] = jnp.concatenate(ob_i, axis=0)
        return hfr, hfi, hbr, hbi

    init = tuple(carry_ref[k:k + 1, :] for k in range(4))
    out = lax.fori_loop(0, rows // 8, tile, init)
    for k in range(4):
        carry_ref[k:k + 1, :] = out[k]


def _s5_scan(a_rows, pfr, pfi, pbr, pbi, nseq, cb, lt):
    r, w = pfr.shape
    nblk = r // nseq // cb
    fspec = pl.BlockSpec((cb, lt), lambda b, j, i: (b * nblk + i, j))
    bspec = pl.BlockSpec((cb, lt), lambda b, j, i: (b * nblk + nblk - 1 - i, j))
    st = jax.ShapeDtypeStruct((r, w), F32)
    return pl.pallas_call(
        functools.partial(_s5_scan_kernel, rows=cb),
        out_shape=(st, st, st, st),
        grid=(nseq, w // lt, nblk),
        in_specs=[pl.BlockSpec((4, lt), lambda b, j, i: (0, j)), fspec, fspec, bspec, bspec],
        out_specs=(fspec, fspec, bspec, bspec),
        scratch_shapes=[pltpu.VMEM((4, lt), F32)],
        compiler_params=_cparams(("parallel", "parallel", "arbitrary")),
        name="s5_scan",
    )(a_rows, pfr, pfi, pbr, pbi)


def _s5_state_out_kernel(y0_ref, hfr_ref, hfi_ref, hbr_ref, hbi_ref, q_ref, y_ref):
    h = jnp.concatenate([hfr_ref[...], hfi_ref[...], hbr_ref[...], hbi_ref[...]], axis=1)
    y_ref[...] = (y0_ref[...] + _dot(h.astype(BF16), q_ref[0])).astype(BF16)


def _s5_state_out(y0, hfr, hfi, hbr, hbi, qmat, rb):
    r = y0.shape[0]
    no, c, _ = qmat.shape
    hspec = pl.BlockSpec((rb, c // 4), lambda o, i: (i, o))
    yspec = pl.BlockSpec((rb, S5_OCT), lambda o, i: (i, o))
    return pl.pallas_call(
        _s5_state_out_kernel,
        out_shape=jax.ShapeDtypeStruct(y0.shape, BF16),
        grid=(no, r // rb),
        in_specs=[yspec, hspec, hspec, hspec, hspec,
                  pl.BlockSpec((1, c, S5_OCT), lambda o, i: (o, 0, 0))],
        out_specs=yspec,
        compiler_params=_cparams(("parallel", "arbitrary")),
        name="s5_state_out",
    )(y0, hfr, hfi, hbr, hbi, qmat)


def _head_indicator():
    lane = lax.broadcasted_iota(jnp.int32, (D_RWKV, 128), 0) // RWKV_HEAD
    col = lax.broadcasted_iota(jnp.int32, (D_RWKV, 128), 1)
    return (lane == col).astype(BF16)


def _head_sum_bcast(x, e, et):
    return _dot_split(_dot_split(x, e), et)


def _rw_pre_kernel(x_ref, prev_ref, next_ref, mu_ref, wup_ref, aup_ref, w0_ref, a0_ref,
                   kk_ref_p, ka_ref_p, rk_ref_p, e_ref, et_ref,
                   r_ref, kk_ref, v_ref, kdf_ref, kdb_ref, bf_ref, bb_ref, lwf_ref, lwb_ref,
                   bonus_ref, *, tm, seq):
    i = pl.program_id(0)
    x = x_ref[...].astype(F32)
    first = (i * tm) % seq == 0
    last = ((i + 1) * tm) % seq == 0
    prev_row = jnp.where(first, 0.0, prev_ref[HALO - 1:HALO, :].astype(F32))
    next_row = jnp.where(last, 0.0, next_ref[0:1, :].astype(F32))
    row = lax.broadcasted_iota(jnp.int32, x.shape, 0)
    prev = jnp.where(row == 0, prev_row, pltpu.roll(x, 1, 0))
    nxt = jnp.where(row == tm - 1, next_row, pltpu.roll(x, tm - 1, 0))
    z = x + (0.5 * (prev + nxt) - x) * mu_ref[...]

    r = z[:, :D_RWKV]
    k = z[:, D_RWKV:2 * D_RWKV]
    v = z[:, 2 * D_RWKV:3 * D_RWKV]
    xw = z[:, 3 * D_RWKV:3 * D_RWKV + 2 * LORA_RANK]
    xa = z[:, 3 * D_RWKV + 2 * LORA_RANK:]
    e, et = e_ref[...], et_ref[...]

    kk = k * kk_ref_p[...]
    kk = kk * lax.rsqrt(_head_sum_bcast(kk * kk, e, et) + L2_EPS)
    bonus_ref[...] = (_head_sum_bcast(r * k * rk_ref_p[...], e, et) * v).astype(BF16)
    r_ref[...] = r.astype(BF16)
    kk_ref[...] = kk.astype(BF16)
    v_ref[...] = v.astype(BF16)

    wpre = _dot(jnp.tanh(xw).astype(BF16), wup_ref[...]) + w0_ref[...]
    apre = _dot(xa.astype(BF16), aup_ref[...]) + a0_ref[...]
    lw = -math.exp(-0.5) * _sigmoid(wpre)
    a = _sigmoid(apre)
    ka = ka_ref_p[...]
    lwf_ref[...] = lw[:, :D_RWKV]
    lwb_ref[...] = lw[:, D_RWKV:]
    af, ab = a[:, :D_RWKV], a[:, D_RWKV:]
    kdf_ref[...] = (k * (1.0 + (af - 1.0) * ka)).astype(BF16)
    kdb_ref[...] = (k * (1.0 + (ab - 1.0) * ka)).astype(BF16)
    bf_ref[...] = (kk * af).astype(BF16)
    bb_ref[...] = (kk * ab).astype(BF16)


def _rw_pre(rw_in, mu, wup, aup, w0, a0, k_k, k_a, r_k, e, et, tm, seq):
    n = rw_in.shape[0]
    w = D_RW_IN
    nbh = n // HALO
    th = tm // HALO
    row = lambda i: (i, 0)
    const = lambda i: (0, 0)
    ospec = pl.BlockSpec((tm, D_RWKV), row)
    sb = jax.ShapeDtypeStruct((n, D_RWKV), BF16)
    sf = jax.ShapeDtypeStruct((n, D_RWKV), F32)
    return pl.pallas_call(
        functools.partial(_rw_pre_kernel, tm=tm, seq=seq),
        out_shape=(sb,) * 7 + (sf, sf, sb),
        grid=(n // tm,),
        in_specs=[pl.BlockSpec((tm, w), row),
                  pl.BlockSpec((HALO, w), lambda i: (jnp.maximum(i * th - 1, 0), 0)),
                  pl.BlockSpec((HALO, w), lambda i: (jnp.minimum((i + 1) * th, nbh - 1), 0)),
                  pl.BlockSpec((1, w), const),
                  pl.BlockSpec((2 * LORA_RANK, 2 * D_RWKV), const),
                  pl.BlockSpec((2 * LORA_RANK, 2 * D_RWKV), const),
                  pl.BlockSpec((1, 2 * D_RWKV), const),
                  pl.BlockSpec((1, 2 * D_RWKV), const),
                  pl.BlockSpec((1, D_RWKV), const),
                  pl.BlockSpec((1, D_RWKV), const),
                  pl.BlockSpec((1, D_RWKV), const),
                  pl.BlockSpec((D_RWKV, 128), const),
                  pl.BlockSpec((128, D_RWKV), const)],
        out_specs=(ospec,) * 10,
        compiler_params=_cparams(("parallel",)),
        name="rw_pre",
    )(rw_in, rw_in, rw_in, mu, wup, aup, w0, a0, k_k, k_a, r_k, e, et)


def _wkv_kernel(r_ref, kk_ref, v_ref, kd_ref, b_ref, lw_ref, y_ref, s_ref, *, rev):
    t = WKV_CHUNK
    hd = RWKV_HEAD
    w = HEAD_PAIR

    @pl.when(pl.program_id(1) == 0)
    def _():
        s_ref[...] = jnp.zeros_like(s_ref)

    def iota(shape, dim):
        return lax.broadcasted_iota(jnp.int32, shape, dim)

    rr, cc = iota((t, t), 0), iota((t, t), 1)
    tri = ((rr <= cc) if rev else (rr >= cc)).astype(BF16)
    rw_, lane = iota((t, w), 0), iota((t, w), 1)
    col = lane % hd
    m0 = lane < hd
    if rev:
        incl, strict = rw_ <= col, rw_ < col
    else:
        incl, strict = rw_ >= col, rw_ > col
    eye_f = (rw_ == col).astype(F32)
    m0_2t = iota((2 * t, w), 1) < hd
    mask_2t = jnp.concatenate([strict, incl], axis=0)
    eye_w = iota((w, w), 0) == iota((w, w), 1)
    bd_mask = (iota((w, w), 0) < hd) == (iota((w, w), 1) < hd)
    t_last = 0 if rev else t - 1

    def bdiag(x):
        return jnp.concatenate([jnp.where(m0, x, 0.0), jnp.where(m0, 0.0, x)], axis=0)

    def group(gi, carry):
        hv = []
        for sq, j in [(sq, j) for sq in range(r_ref.shape[0]) for j in range(WKV_PAIRS_PER_STEP)]:
            hp = gi * WKV_PAIRS_PER_STEP + j
            sl = pl.ds(pl.multiple_of(hp * HEAD_PAIR, HEAD_PAIR), HEAD_PAIR)
            r, kk, v = (ref[sq, :, sl].astype(F32) for ref in (r_ref, kk_ref, v_ref))
            kd, beta, lw = kd_ref[sq, :, sl].astype(F32), b_ref[sq, :, sl].astype(F32), lw_ref[sq, :, sl]
            lw_hi = lw.astype(BF16)
            lw_lo = (lw - lw_hi.astype(F32)).astype(BF16)
            lcum = _dot(tri, lw_hi) + _dot(tri, lw_lo)
            lcum_x = lcum - lw
            cref = lcum[t // 2:t // 2 + 1, :]
            ltot = lcum[t_last:t_last + 1, :]
            e1 = jnp.exp(lcum - cref)
            e1x = jnp.exp(lcum_x - cref)
            e2 = jnp.exp(cref - lcum)
            ec = jnp.exp(cref)
            ewt = jnp.exp(ltot - cref)
            wtot = jnp.exp(ltot)
            r_t = r * e1
            a_t = -kk * e1x
            b_t = beta * e2
            k_t = kd * e2
            a_0 = a_t * ec
            r_0 = r_t * ec
            b_h = b_t * ewt
            k_h = k_t * ewt
            hv.append(dict(
                sq=sq, hp=hp, sl=sl, wtot=wtot, a_0=a_0, r_0=r_0,
                lm=jnp.concatenate([a_t, r_t], axis=0),
                rma=jnp.concatenate([b_t, k_t], axis=0).astype(BF16),
                rmb=jnp.concatenate([k_t, b_t], axis=0).astype(BF16),
                bk=jnp.concatenate([b_h, k_h], axis=0).astype(BF16), v=v))
        for h in hv:
            g0 = _dot_nt(jnp.where(m0_2t, h['lm'], 0.0).astype(BF16), h['rma'])
            g1 = _dot_nt(jnp.where(m0_2t, 0.0, h['lm']).astype(BF16), h['rmb'])
            own = jnp.where(m0_2t, g0, g1)
            oth = jnp.where(m0_2t, g1, g0)
            h['a_ab'] = jnp.where(strict, own[:t], 0.0)
            h['a_rb'] = jnp.where(incl, own[t:], 0.0).astype(BF16)
            h['a_k'] = jnp.where(mask_2t, oth, 0.0).astype(BF16)
        for h in hv:
            v = h['v']
            h['inv'] = eye_f + h['a_ab']
            h['nk'] = _dot(h['a_ab'].astype(BF16), bdiag(h['a_ab']).astype(BF16))
            v_rows = jnp.concatenate([jnp.where(m0, 0.0, v), jnp.where(m0, v, 0.0)], axis=0)
            h['av'] = _dot(h['a_k'], v_rows.astype(BF16))
        for s in range(1, NEUMANN_STEPS + 1):
            for h in hv:
                nk_bd = bdiag(h['nk']).astype(BF16)
                if s < NEUMANN_STEPS:
                    nx = _dot(jnp.concatenate([h['nk'], h['inv']], axis=0).astype(BF16), nk_bd)
                    h['nk'] = nx[:t]
                    h['inv'] = h['inv'] + nx[t:]
                else:
                    h['inv'] = h['inv'] + _dot(h['inv'].astype(BF16), nk_bd)
        for h in hv:
            rhs = jnp.concatenate([bdiag(h['a_0']), bdiag(h['av'][:t])], axis=1)
            h['pp'] = _dot(h['inv'].astype(BF16), rhs.astype(BF16))
        for h in hv:
            pp = h['pp']
            rhs = jnp.concatenate([bdiag(pp[:, :w]), bdiag(pp[:, w:])], axis=1)
            h['qq'] = (_dot(h['a_rb'], rhs.astype(BF16))
                       + jnp.concatenate([h['r_0'], h['av'][t:]], axis=1))
            pv = jnp.concatenate([pp, jnp.concatenate([jnp.zeros_like(h['v']), h['v']], axis=1)], axis=0)
            h['mn'] = _dot_tn(h['bk'], pv.astype(BF16))
        for h in hv:
            st = s_ref[h['sq'], h['hp']]
            st_b = st.astype(BF16)
            y_ref[h['sq'], :, h['sl']] = _dot(h['qq'][:, :w].astype(BF16), st_b) + h['qq'][:, w:]
            wcol = jnp.sum(jnp.where(eye_w, jnp.broadcast_to(h['wtot'], (w, w)), 0.0),
                           axis=1, keepdims=True)
            s_ref[h['sq'], h['hp']] = (
                wcol * st + _dot(jnp.where(bd_mask, h['mn'][:, :w], 0.0).astype(BF16), st_b)
                + jnp.where(bd_mask, h['mn'][:, w:], 0.0))
        return carry

    lax.fori_loop(0, RWKV_HEADS // (2 * WKV_PAIRS_PER_STEP), group, 0)


def _wkv(r, kk, v, kd, beta, lw, nseq, rev):
    n = r.shape[0]
    t = WKV_CHUNK
    seq = n // nseq
    nc = seq // t
    ns = WKV_SEQS if nseq % WKV_SEQS == 0 else 1
    if rev:
        idx = lambda b, c: (b, nc - 1 - c, 0)
    else:
        idx = lambda b, c: (b, c, 0)
    spec = pl.BlockSpec((ns, t, D_RWKV), idx)
    y = pl.pallas_call(
        functools.partial(_wkv_kernel, rev=rev),
        out_shape=jax.ShapeDtypeStruct((nseq, seq, D_RWKV), F32),
        grid=(nseq // ns, nc),
        in_specs=[spec] * 6,
        out_specs=spec,
        scratch_shapes=[pltpu.VMEM((ns, RWKV_HEADS // 2, HEAD_PAIR, HEAD_PAIR), F32)],
        compiler_params=_cparams(("parallel", "arbitrary")),
        name="wkv_bwd" if rev else "wkv_fwd",
    )(*[a.reshape(nseq, seq, D_RWKV) for a in (r, kk, v, kd, beta, lw)])
    return y.reshape(n, D_RWKV)


def _tail_kernel(ys_ref, sg_ref, ms_ref, yf_ref, yb_ref, bonus_ref, rg_ref, mr_ref, x_ref,
                 wglu_ref, swout_ref, lnw_ref, lnb_ref, e_ref, et_ref, rwout_ref, wo_ref, g_ref,
                 o_ref, ytok_ref):
    f32 = lambda ref: ref[...].astype(F32)
    rows = ys_ref.shape[0]
    octs = S5_GROUPS // S5_OCT_GROUPS
    for o in range(octs):
        for t in range(S5_CHUNK):
            lane = o * S5_OCT + t * 128
            ytok_ref[o, pl.ds(t, rows, stride=S5_CHUNK), :] = ys_ref[:, lane:lane + 128].astype(F32)
    z = _gelu_tanh(jnp.concatenate([ytok_ref[o] for o in range(octs)], axis=1))
    z = z * _sigmoid(_dot(z.astype(BF16), wglu_ref[...]))
    z = z * _silu(f32(sg_ref))
    h = _sigmoid(f32(ms_ref)) * _dot(z.astype(BF16), swout_ref[...])

    e, et = e_ref[...], et_ref[...]
    y = yf_ref[...] + yb_ref[...]
    inv_n = 1.0 / RWKV_HEAD
    mean = _head_sum_bcast(y, e, et) * inv_n
    yc = y - mean
    var = _head_sum_bcast(yc * yc, e, et) * inv_n
    y = yc * lax.rsqrt(var + GN_EPS) * lnw_ref[...] + lnb_ref[...] + f32(bonus_ref)
    y = y * _silu(f32(rg_ref))
    h = h + _sigmoid(f32(mr_ref)) * _dot(y.astype(BF16), rwout_ref[...])

    out = _dot(h.astype(BF16), wo_ref[...])
    ms = jnp.mean(out * out, axis=-1, keepdims=True)
    o_ref[...] = x_ref[...] + out * lax.rsqrt(ms + RMS_EPS) * g_ref[...]


def _tail(y_s5, s5_gate, merge, yf, yb, bonus, rw_gate, x, p, tm):
    n = x.shape[0]
    row = lambda i: (i, 0)
    const = lambda i: (0, 0)
    half = pl.BlockSpec((tm, D_RWKV), row)
    full = pl.BlockSpec((tm, D_MODEL), row)

    def resident(shape):
        return pl.BlockSpec(shape, const, pipeline_mode=pl.Buffered(1))

    return pl.pallas_call(
        _tail_kernel,
        out_shape=jax.ShapeDtypeStruct((n, D_MODEL), F32),
        grid=(n // tm,),
        in_specs=[pl.BlockSpec((tm // S5_CHUNK, S5_CHUNK * D_S5), row),
                  half, full,
                  half, half, half, half,
                  pl.BlockSpec((tm, D_MODEL), lambda i: (i, 1)),
                  full,
                  resident((D_S5, D_S5)), resident((D_S5, D_MODEL)),
                  resident((1, D_RWKV)), resident((1, D_RWKV)),
                  resident((D_RWKV, 128)), resident((128, D_RWKV)),
                  resident((D_RWKV, D_MODEL)), resident((D_MODEL, D_MODEL)),
                  resident((1, D_MODEL))],
        out_specs=full,
        scratch_shapes=[pltpu.VMEM((D_S5 // 128, tm, 128), F32)],
        compiler_params=_cparams(("parallel",)),
        name="tail",
    )(y_s5, s5_gate, merge, yf, yb, bonus, rw_gate, merge, x,
      p['w_glu'], p['s5_w_out'], p['ln_w'], p['ln_b'], p['e'], p['et'], p['rw_w_out'], p['w_o'],
      p['post_g'])


def _pick(n, pref):
    while n % pref:
        pref //= 2
    return pref


def _prepare(pre_norm_g, post_norm_g, w_in, s5_lam_re, s5_lam_im, s5_log_dt, s5_b_re, s5_b_im,
             s5_c_re, s5_c_im, s5_d, s5_w_glu, s5_w_out, rw_mu, rw_w0, rw_w_up, rw_a0, rw_a_up,
             rw_k_k, rw_k_a, rw_r_k, rw_ln_w, rw_ln_b, rw_w_out, w_o):
    p = {}
    p['pre_g'] = pre_norm_g.astype(F32).reshape(1, D_MODEL)
    p['post_g'] = post_norm_g.astype(F32).reshape(1, D_MODEL)
    w = w_in.astype(BF16)
    rw_end = 2 * D_S5 + D_RW_IN
    p['w_in'] = jnp.concatenate(
        [w[:, :rw_end], jnp.zeros((D_MODEL, D_RW_PAD - D_RW_IN), BF16), w[:, rw_end:]], axis=1)
    p['s5_m'], p['s5_p'], p['s5_q'], p['s5_a'] = _s5_weights(
        s5_lam_re, s5_lam_im, s5_log_dt, s5_b_re, s5_b_im, s5_c_re, s5_c_im, s5_d)
    p['w_glu'] = s5_w_glu.astype(BF16)
    p['s5_w_out'] = s5_w_out.astype(BF16)
    p['mu'] = rw_mu.astype(F32).reshape(1, D_RW_IN)
    zeros = jnp.zeros((LORA_RANK, D_RWKV), F32)

    def blockdiag(u):
        return jnp.concatenate([jnp.concatenate([u[0], zeros], axis=1),
                                jnp.concatenate([zeros, u[1]], axis=1)], axis=0).astype(BF16)

    p['wup'] = blockdiag(rw_w_up.astype(F32))
    p['aup'] = blockdiag(rw_a_up.astype(F32))
    p['w0'] = rw_w0.astype(F32).reshape(1, 2 * D_RWKV)
    p['a0'] = rw_a0.astype(F32).reshape(1, 2 * D_RWKV)
    p['k_k'] = rw_k_k.astype(F32).reshape(1, D_RWKV)
    p['k_a'] = rw_k_a.astype(F32).reshape(1, D_RWKV)
    p['r_k'] = rw_r_k.astype(F32).reshape(1, D_RWKV)
    p['ln_w'] = rw_ln_w.astype(F32).reshape(1, D_RWKV)
    p['ln_b'] = rw_ln_b.astype(F32).reshape(1, D_RWKV)
    p['rw_w_out'] = rw_w_out.astype(BF16)
    p['w_o'] = w_o.astype(BF16)
    e = _head_indicator()
    p['e'] = e
    p['et'] = e.T
    return p


def _s5_mix(x2, p, nseq):
    r = x2.shape[0]
    rb = _pick(r, 256)
    (y0,) = _s5_mm(x2, p['s5_m'], rb, 1, "s5_chunk_out")
    pfr, pfi, pbr, pbi = _s5_mm(x2, p['s5_p'], rb, 4, "s5_chunk_state")
    cb = _pick(r // nseq, 128)
    hfr, hfi, hbr, hbi = _s5_scan(p['s5_a'], pfr, pfi, pbr, pbi, nseq, cb, 1024)
    return _s5_state_out(y0, hfr, hfi, hbr, hbi, p['s5_q'], rb)


def _layer(x, p):
    bsz, seq, _ = x.shape
    n = bsz * seq
    x2 = x.reshape(n, D_MODEL)
    u, s5_gate, rw_in, rw_gate, merge = _in_proj(x2, p['pre_g'], p['w_in'], _pick(n, 1024))

    y_s5 = _s5_mix(u, p, bsz)

    (r, kk, v, kdf, kdb, bf, bb, lwf, lwb, bonus) = _rw_pre(
        rw_in, p['mu'], p['wup'], p['aup'], p['w0'], p['a0'], p['k_k'], p['k_a'], p['r_k'],
        p['e'], p['et'], _pick(seq, 256), seq)
    yf = _wkv(r, kk, v, kdf, bf, lwf, bsz, False)
    yb = _wkv(r, kk, v, kdb, bb, lwb, bsz, True)

    out = _tail(y_s5, s5_gate, merge, yf, yb, bonus, rw_gate, x2, p, _pick(n, 256))
    return out.reshape(bsz, seq, D_MODEL)


def kernel(x_prompt, x_sample, pre_norm_g, post_norm_g, w_in, s5_lam_re, s5_lam_im, s5_log_dt, s5_b_re, s5_b_im, s5_c_re, s5_c_im, s5_d, s5_w_glu, s5_w_out, rw_mu, rw_w0, rw_w_up, rw_a0, rw_a_up, rw_k_k, rw_k_a, rw_r_k, rw_ln_w, rw_ln_b, rw_w_out, w_o):
    params = (pre_norm_g, post_norm_g, w_in, s5_lam_re, s5_lam_im, s5_log_dt, s5_b_re, s5_b_im,
              s5_c_re, s5_c_im, s5_d, s5_w_glu, s5_w_out, rw_mu, rw_w0, rw_w_up, rw_a0, rw_a_up,
              rw_k_k, rw_k_a, rw_r_k, rw_ln_w, rw_ln_b, rw_w_out, w_o)
    y_prompt, y_sample = x_prompt, x_sample
    for layer in range(w_in.shape[0]):
        p = _prepare(*[w[layer] for w in params])
        y_prompt = _layer(y_prompt, p)
        y_sample = _layer(y_sample, p)
    return (y_prompt, y_sample)
```

```python
import functools
import math

import jax
import jax.numpy as jnp
from jax import lax
from jax.experimental import pallas as pl
from jax.experimental.pallas import tpu as pltpu

F32 = jnp.float32
BF16 = jnp.bfloat16

D_MODEL = 2048
D_S5 = 1024
S5_GROUP = 16
S5_GROUPS = 64
S5_STATE = 64
D_RWKV = 1024
RWKV_HEAD = 64
RWKV_HEADS = 16
LORA_RANK = 64
D_RW_IN = 3 * D_RWKV + 4 * LORA_RANK
RMS_EPS = 1e-6
GN_EPS = 64e-5
L2_EPS = 1e-12

S5_CHUNK = 16
S5_OCT_GROUPS = 8
S5_OCT = S5_OCT_GROUPS * S5_CHUNK * S5_GROUP
WKV_CHUNK = 64
HEAD_PAIR = 2 * RWKV_HEAD
NEUMANN_STEPS = 5
WKV_PAIRS_PER_STEP = 8
WKV_SEQS = 2
HALO = 16

VMEM_LIMIT = 48 * 1024 * 1024


def _cparams(sem):
    return pltpu.CompilerParams(dimension_semantics=sem, vmem_limit_bytes=VMEM_LIMIT)


def _dot(a, b):
    return jnp.dot(a, b, preferred_element_type=F32)


def _dot_nt(a, b):
    return lax.dot_general(a, b, (((1,), (1,)), ((), ())), preferred_element_type=F32)


def _dot_tn(a, b):
    return lax.dot_general(a, b, (((0,), (0,)), ((), ())), preferred_element_type=F32)


def _sigmoid(x):
    return 1.0 / (1.0 + jnp.exp2(x * (-math.log2(math.e))))


def _silu(x):
    return x * _sigmoid(x)


def _gelu_tanh(x):
    c = math.sqrt(2.0 / math.pi)
    return 0.5 * x * (1.0 + jnp.tanh(c * (x + 0.044715 * (x * x * x))))


IN_TN = 512
PROJ_RW_W = 4096
PROJ_W = PROJ_RW_W + D_S5 + D_RWKV + 2 * D_MODEL
PROJ_SG_BLK = PROJ_RW_W // D_S5
PROJ_RG_BLK = PROJ_SG_BLK + 1
PROJ_MG_BLK = (PROJ_RW_W + D_S5 + D_RWKV) // D_MODEL


def _rms_bf16(x, g):
    ms = jnp.mean(x * x, axis=-1, keepdims=True)
    return (x * lax.rsqrt(ms + RMS_EPS) * g).astype(BF16)


def _in_proj_u_kernel(x_ref, g_ref, w_ref, o_ref, ubuf_ref):
    res = _dot(_rms_bf16(x_ref[...], g_ref[...]), w_ref[...])
    rows = ubuf_ref.shape[1] // S5_CHUNK
    for o in range(ubuf_ref.shape[0]):
        ubuf_ref[o] = res[:, o * 128:(o + 1) * 128]
        for t in range(S5_CHUNK):
            tok = ubuf_ref[o, pl.ds(t, rows, stride=S5_CHUNK), :]
            lane = o * S5_OCT + t * 128
            o_ref[:, lane:lane + 128] = tok.astype(BF16)


def _in_proj_u(x, g, w_u, tm):
    n, d = x.shape
    return pl.pallas_call(
        _in_proj_u_kernel,
        out_shape=jax.ShapeDtypeStruct((n // S5_CHUNK, D_S5 * S5_CHUNK), BF16),
        grid=(n // tm,),
        in_specs=[pl.BlockSpec((tm, d), lambda i: (i, 0)),
                  pl.BlockSpec((1, d), lambda i: (0, 0)),
                  pl.BlockSpec((d, D_S5), lambda i: (0, 0))],
        out_specs=pl.BlockSpec((tm // S5_CHUNK, D_S5 * S5_CHUNK), lambda i: (i, 0)),
        scratch_shapes=[pltpu.VMEM((D_S5 // 128, tm, 128), F32)],
        compiler_params=_cparams(("parallel",)),
        name="in_proj_u",
    )(x, g, w_u)


def _in_proj_kernel(x_ref, g_ref, w_ref, o_ref, hn_ref):
    @pl.when(pl.program_id(1) == 0)
    def _():
        hn_ref[...] = _rms_bf16(x_ref[...], g_ref[...])

    o_ref[...] = _dot(hn_ref[...], w_ref[...]).astype(BF16)


def _in_proj(x, g, w, tm):
    n, d = x.shape
    return pl.pallas_call(
        _in_proj_kernel,
        out_shape=jax.ShapeDtypeStruct((n, w.shape[1]), BF16),
        grid=(n // tm, w.shape[1] // IN_TN),
        in_specs=[pl.BlockSpec((tm, d), lambda i, j: (i, 0)),
                  pl.BlockSpec((1, d), lambda i, j: (0, 0)),
                  pl.BlockSpec((d, IN_TN), lambda i, j: (0, j))],
        out_specs=pl.BlockSpec((tm, IN_TN), lambda i, j: (i, j)),
        scratch_shapes=[pltpu.VMEM((tm, d), BF16)],
        compiler_params=_cparams(("parallel", "arbitrary")),
        name="in_proj",
    )(x, g, w)


def _s5_weights(lam_re, lam_im, log_dt, b_re, b_im, c_re, c_im, d):
    t = S5_CHUNK
    g, p, h = S5_GROUPS, S5_STATE, S5_GROUP
    hp = lax.Precision.HIGHEST
    lam = lax.complex(lam_re.astype(F32), lam_im.astype(F32))
    dt = jnp.exp(log_dt.astype(F32))[..., None]
    lam_dt = lam * dt
    lam_bar = jnp.exp(lam_dt)
    bbar = ((lam_bar - 1.0) / lam)[..., None] * lax.complex(b_re.astype(F32), b_im.astype(F32))
    c = lax.complex(c_re.astype(F32), c_im.astype(F32))
    steps = jnp.arange(t + 1, dtype=F32)
    pw = jnp.exp(lam_dt[None] * steps[:, None, None, None])

    kern = jnp.einsum('dghp,ldgp,dgpj->dglhj', c, pw[:t], bbar, precision=hp).real
    tt = jnp.arange(t)
    lag = tt[None, :] - tt[:, None]
    kf = kern[0][:, jnp.abs(lag)]
    kb = kern[1][:, jnp.abs(lag)]
    k0 = kern[0][:, 0] + kern[1][:, 0] + jnp.eye(h, dtype=F32) * d.astype(F32).reshape(g, h)[:, :, None]
    lag5 = lag[None, :, :, None, None]
    m = jnp.where(lag5 > 0, kf, jnp.where(lag5 < 0, kb, k0[:, None, None]))
    m = m.transpose(0, 1, 4, 2, 3).reshape(g, t * h, t * h)

    pf = pw[t - 1 - tt, 0][..., None] * bbar[0][None]
    pb = pw[tt, 1][..., None] * bbar[1][None]

    def _p_mat(x):
        return x.transpose(1, 0, 3, 2).reshape(g, t * h, p)

    p_parts = [_p_mat(pf.real), _p_mat(pf.imag), _p_mat(pb.real), _p_mat(pb.imag)]

    cf = c[0][None] * pw[tt + 1, 0][:, :, None, :]
    cb = c[1][None] * pw[t - tt, 1][:, :, None, :]

    def _q_mat(x):
        return x.transpose(1, 3, 0, 2).reshape(g, p, t * h)

    q_parts = [_q_mat(cf.real), _q_mat(-cf.imag), _q_mat(cb.real), _q_mat(-cb.imag)]

    k8 = S5_OCT_GROUPS
    no = g // k8
    gi = jnp.arange(k8)

    def _spread(width):
        c = jnp.arange(width)
        tgt = (gi[:, None] * width + c[None, :])[:, :, None]
        return (tgt == jnp.arange(k8 * width)[None, None, :]).astype(BF16)

    def _spread_tok():
        u, hh = jnp.arange(t * h) // h, jnp.arange(t * h) % h
        tgt = (u[None, :] * (k8 * h) + gi[:, None] * h + hh[None, :])[:, :, None]
        return (tgt == jnp.arange(S5_OCT)[None, None, :]).astype(BF16)

    def _cols(x, spread):
        x = x.astype(BF16).reshape(no, k8, x.shape[1], x.shape[2])
        return jnp.einsum('ogrc,gcd->ogrd', x, spread, preferred_element_type=BF16)

    def _rows_tok(x):
        c = x.shape[-1]
        return x.reshape(no, k8, t, h, c).transpose(0, 2, 1, 3, 4).reshape(no, S5_OCT, c)

    sp_tok, sp_state = _spread_tok(), _spread(p)
    m_oct = _rows_tok(_cols(m, sp_tok))
    w_p = jnp.concatenate([_rows_tok(_cols(x, sp_state)) for x in p_parts], axis=2)
    qmat = jnp.concatenate([_cols(x, sp_tok).reshape(no, k8 * p, S5_OCT) for x in q_parts],
                           axis=1)
    at = pw[t]
    a_rows = jnp.stack([at[0].real, at[0].imag, at[1].real, at[1].imag]).reshape(4, g * p)
    return m_oct, w_p, qmat, a_rows


def _s5_mm_kernel(x_ref, w_ref, *o_refs):
    res = _dot(x_ref[...], w_ref[0])
    n = res.shape[1] // len(o_refs)
    for k, ref in enumerate(o_refs):
        ref[...] = res[:, n * k:n * (k + 1)].astype(ref.dtype)


def _s5_mm(x2, w, rb, nout, name):
    r = x2.shape[0]
    no, _, c = w.shape
    cw = c // nout
    return pl.pallas_call(
        _s5_mm_kernel,
        out_shape=tuple(jax.ShapeDtypeStruct((r, no * cw), F32) for _ in range(nout)),
        grid=(no, r // rb),
        in_specs=[pl.BlockSpec((rb, S5_OCT), lambda o, i: (i, o)),
                  pl.BlockSpec((1, S5_OCT, c), lambda o, i: (o, 0, 0))],
        out_specs=tuple(pl.BlockSpec((rb, cw), lambda o, i: (i, o)) for _ in range(nout)),
        compiler_params=_cparams(("parallel", "arbitrary")),
        name=name,
    )(x2, w)


def _s5_scan_kernel(a_ref, pfr_ref, pfi_ref, pbr_ref, pbi_ref,
                    hfr_ref, hfi_ref, hbr_ref, hbi_ref, carry_ref, *, rows):
    @pl.when(pl.program_id(2) == 0)
    def _():
        carry_ref[...] = jnp.zeros_like(carry_ref)

    afr, afi = a_ref[0:1, :], a_ref[1:2, :]
    abr, abi = a_ref[2:3, :], a_ref[3:4, :]

    def tile(i, carry):
        hfr, hfi, hbr, hbi = carry
        r0 = pl.multiple_of(i * 8, 8)
        pfr, pfi = pfr_ref[pl.ds(r0, 8), :], pfi_ref[pl.ds(r0, 8), :]
        rb0 = pl.multiple_of(rows - 8 - i * 8, 8)
        pbr, pbi = pbr_ref[pl.ds(rb0, 8), :], pbi_ref[pl.ds(rb0, 8), :]
        of_r, of_i, ob_r, ob_i = [], [], [None] * 8, [None] * 8
        for s in range(8):
            of_r.append(hfr)
            of_i.append(hfi)
            hfr, hfi = (afr * hfr - afi * hfi + pfr[s:s + 1, :],
                        afr * hfi + afi * hfr + pfi[s:s + 1, :])
            sb = 7 - s
            ob_r[sb] = hbr
            ob_i[sb] = hbi
            hbr, hbi = (abr * hbr - abi * hbi + pbr[sb:sb + 1, :],
                        abr * hbi + abi * hbr + pbi[sb:sb + 1, :])
        hfr_ref[pl.ds(r0, 8), :] = jnp.concatenate(of_r, axis=0)
        hfi_ref[pl.ds(r0, 8), :] = jnp.concatenate(of_i, axis=0)
        hbr_ref[pl.ds(rb0, 8), :] = jnp.concatenate(ob_r, axis=0)
        hbi_ref[pl.ds(rb0, 8), :] = jnp.concatenate(ob_i, axis=0)
        return hfr, hfi, hbr, hbi

    init = tuple(carry_ref[k:k + 1, :] for k in range(4))
    out = lax.fori_loop(0, rows // 8, tile, init)
    for k in range(4):
        carry_ref[k:k + 1, :] = out[k]


def _s5_scan(a_rows, pfr, pfi, pbr, pbi, nseq, cb, lt):
    r, w = pfr.shape
    nblk = r // nseq // cb
    fspec = pl.BlockSpec((cb, lt), lambda b, j, i: (b * nblk + i, j))
    bspec = pl.BlockSpec((cb, lt), lambda b, j, i: (b * nblk + nblk - 1 - i, j))
    st = jax.ShapeDtypeStruct((r, w), F32)
    return pl.pallas_call(
        functools.partial(_s5_scan_kernel, rows=cb),
        out_shape=(st, st, st, st),
        grid=(nseq, w // lt, nblk),
        in_specs=[pl.BlockSpec((4, lt), lambda b, j, i: (0, j)), fspec, fspec, bspec, bspec],
        out_specs=(fspec, fspec, bspec, bspec),
        scratch_shapes=[pltpu.VMEM((4, lt), F32)],
        compiler_params=_cparams(("parallel", "parallel", "arbitrary")),
        name="s5_scan",
    )(a_rows, pfr, pfi, pbr, pbi)


def _s5_state_out_kernel(y0_ref, hfr_ref, hfi_ref, hbr_ref, hbi_ref, q_ref, y_ref):
    h = jnp.concatenate([hfr_ref[...], hfi_ref[...], hbr_ref[...], hbi_ref[...]], axis=1)
    y_ref[...] = (y0_ref[...] + _dot(h.astype(BF16), q_ref[0])).astype(BF16)


def _s5_state_out(y0, hfr, hfi, hbr, hbi, qmat, rb):
    r = y0.shape[0]
    no, c, _ = qmat.shape
    hspec = pl.BlockSpec((rb, c // 4), lambda o, i: (i, o))
    yspec = pl.BlockSpec((rb, S5_OCT), lambda o, i: (i, o))
    return pl.pallas_call(
        _s5_state_out_kernel,
        out_shape=jax.ShapeDtypeStruct(y0.shape, BF16),
        grid=(no, r // rb),
        in_specs=[yspec, hspec, hspec, hspec, hspec,
                  pl.BlockSpec((1, c, S5_OCT), lambda o, i: (o, 0, 0))],
        out_specs=yspec,
        compiler_params=_cparams(("parallel", "arbitrary")),
        name="s5_state_out",
    )(y0, hfr, hfi, hbr, hbi, qmat)


def _head_indicator():
    lane = lax.broadcasted_iota(jnp.int32, (D_RWKV, 128), 0) // RWKV_HEAD
    col = lax.broadcasted_iota(jnp.int32, (D_RWKV, 128), 1)
    return (lane == col).astype(BF16)


def _head_sum_bcast(x, e, et):
    return _dot(_dot(x.astype(BF16), e).astype(BF16), et)


def _rw_pre_kernel(x_ref, prev_ref, next_ref, mu_ref, wup_ref, aup_ref, w0_ref, a0_ref,
                   kk_ref_p, ka_ref_p, rk_ref_p, e_ref, et_ref,
                   r_ref, kk_ref, v_ref, kdf_ref, kdb_ref, bf_ref, bb_ref, lwf_ref, lwb_ref,
                   bonus_ref, *, tm, seq):
    i = pl.program_id(0)
    xb = x_ref[...]
    first = (i * tm) % seq == 0
    last = ((i + 1) * tm) % seq == 0
    zero = jnp.zeros((HALO, xb.shape[1]), BF16)
    x_ext = jnp.concatenate([jnp.where(first, zero, prev_ref[...]), xb,
                             jnp.where(last, zero, next_ref[...])], axis=0)
    ri = lax.broadcasted_iota(jnp.int32, (tm, tm + 2 * HALO), 0) + HALO
    ci = lax.broadcasted_iota(jnp.int32, (tm, tm + 2 * HALO), 1)
    band = ((ci == ri - 1) | (ci == ri + 1)).astype(BF16)
    x = xb.astype(F32)
    z = x + (0.5 * _dot(band, x_ext) - x) * mu_ref[...]

    r = z[:, :D_RWKV]
    k = z[:, D_RWKV:2 * D_RWKV]
    v = z[:, 2 * D_RWKV:3 * D_RWKV]
    xw = z[:, 3 * D_RWKV:3 * D_RWKV + 2 * LORA_RANK]
    xa = z[:, 3 * D_RWKV + 2 * LORA_RANK:]
    e, et = e_ref[...], et_ref[...]

    kk = k * kk_ref_p[...]
    kk = kk * lax.rsqrt(_head_sum_bcast(kk * kk, e, et) + L2_EPS)
    bonus_ref[...] = (_head_sum_bcast(r * k * rk_ref_p[...], e, et) * v).astype(BF16)
    r_ref[...] = r.astype(BF16)
    kk_ref[...] = kk.astype(BF16)
    v_ref[...] = v.astype(BF16)

    wpre = _dot(jnp.tanh(xw).astype(BF16), wup_ref[...]) + w0_ref[...]
    apre = _dot(xa.astype(BF16), aup_ref[...]) + a0_ref[...]
    lw = -math.exp(-0.5) * _sigmoid(wpre)
    a = _sigmoid(apre)
    ka = ka_ref_p[...]
    lwf_ref[...] = lw[:, :D_RWKV]
    lwb_ref[...] = lw[:, D_RWKV:]
    af, ab = a[:, :D_RWKV], a[:, D_RWKV:]
    kdf_ref[...] = (k * (1.0 + (af - 1.0) * ka)).astype(BF16)
    kdb_ref[...] = (k * (1.0 + (ab - 1.0) * ka)).astype(BF16)
    bf_ref[...] = (kk * af).astype(BF16)
    bb_ref[...] = (kk * ab).astype(BF16)


def _rw_pre(rw_in, mu, wup, aup, w0, a0, k_k, k_a, r_k, e, et, tm, seq):
    n = rw_in.shape[0]
    w = D_RW_IN
    nbh = n // HALO
    th = tm // HALO
    row = lambda i: (i, 0)
    const = lambda i: (0, 0)
    ospec = pl.BlockSpec((tm, D_RWKV), row)
    sb = jax.ShapeDtypeStruct((n, D_RWKV), BF16)
    sf = jax.ShapeDtypeStruct((n, D_RWKV), F32)
    return pl.pallas_call(
        functools.partial(_rw_pre_kernel, tm=tm, seq=seq),
        out_shape=(sb,) * 7 + (sf, sf, sb),
        grid=(n // tm,),
        in_specs=[pl.BlockSpec((tm, w), row),
                  pl.BlockSpec((HALO, w), lambda i: (jnp.maximum(i * th - 1, 0), 0)),
                  pl.BlockSpec((HALO, w), lambda i: (jnp.minimum((i + 1) * th, nbh - 1), 0)),
                  pl.BlockSpec((1, w), const),
                  pl.BlockSpec((2 * LORA_RANK, 2 * D_RWKV), const),
                  pl.BlockSpec((2 * LORA_RANK, 2 * D_RWKV), const),
                  pl.BlockSpec((1, 2 * D_RWKV), const),
                  pl.BlockSpec((1, 2 * D_RWKV), const),
                  pl.BlockSpec((1, D_RWKV), const),
                  pl.BlockSpec((1, D_RWKV), const),
                  pl.BlockSpec((1, D_RWKV), const),
                  pl.BlockSpec((D_RWKV, 128), const),
                  pl.BlockSpec((128, D_RWKV), const)],
        out_specs=(ospec,) * 10,
        compiler_params=_cparams(("parallel",)),
        name="rw_pre",
    )(rw_in, rw_in, rw_in, mu, wup, aup, w0, a0, k_k, k_a, r_k, e, et)


def _wkv_kernel(r_ref, kk_ref, v_ref, kd_ref, b_ref, lw_ref, y_ref, s_ref, *, rev):
    t = WKV_CHUNK
    hd = RWKV_HEAD
    w = HEAD_PAIR

    @pl.when(pl.program_id(1) == 0)
    def _():
        s_ref[...] = jnp.zeros_like(s_ref)

    def iota(shape, dim):
        return lax.broadcasted_iota(jnp.int32, shape, dim)

    rr, cc = iota((t, t), 0), iota((t, t), 1)
    tri = ((rr <= cc) if rev else (rr >= cc)).astype(BF16)
    rw_, lane = iota((t, w), 0), iota((t, w), 1)
    col = lane % hd
    m0 = lane < hd
    if rev:
        incl, strict = rw_ <= col, rw_ < col
    else:
        incl, strict = rw_ >= col, rw_ > col
    eye_f = (rw_ == col).astype(F32)
    m0_2t = iota((2 * t, w), 1) < hd
    mask_2t = jnp.concatenate([strict, incl], axis=0)
    eye_w = iota((w, w), 0) == iota((w, w), 1)
    bd_mask = (iota((w, w), 0) < hd) == (iota((w, w), 1) < hd)
    t_last = 0 if rev else t - 1

    def bdiag(x):
        return jnp.concatenate([jnp.where(m0, x, 0.0), jnp.where(m0, 0.0, x)], axis=0)

    def group(gi, carry):
        hv = []
        for sq, j in [(sq, j) for sq in range(r_ref.shape[0]) for j in range(WKV_PAIRS_PER_STEP)]:
            hp = gi * WKV_PAIRS_PER_STEP + j
            sl = pl.ds(pl.multiple_of(hp * HEAD_PAIR, HEAD_PAIR), HEAD_PAIR)
            r, kk, v = (ref[sq, :, sl].astype(F32) for ref in (r_ref, kk_ref, v_ref))
            kd, beta, lw = kd_ref[sq, :, sl].astype(F32), b_ref[sq, :, sl].astype(F32), lw_ref[sq, :, sl]
            lw_hi = lw.astype(BF16)
            lw_lo = (lw - lw_hi.astype(F32)).astype(BF16)
            lcum = _dot(tri, lw_hi) + _dot(tri, lw_lo)
            lcum_x = lcum - lw
            cref = lcum[t // 2:t // 2 + 1, :]
            ltot = lcum[t_last:t_last + 1, :]
            e1 = jnp.exp(lcum - cref)
            e1x = jnp.exp(lcum_x - cref)
            e2 = jnp.exp(cref - lcum)
            ec = jnp.exp(cref)
            ewt = jnp.exp(ltot - cref)
            wtot = jnp.exp(ltot)
            r_t = r * e1
            a_t = -kk * e1x
            b_t = beta * e2
            k_t = kd * e2
            a_0 = a_t * ec
            r_0 = r_t * ec
            b_h = b_t * ewt
            k_h = k_t * ewt
            hv.append(dict(
                sq=sq, hp=hp, sl=sl, wtot=wtot, a_0=a_0, r_0=r_0,
                lm=jnp.concatenate([a_t, r_t], axis=0),
                rma=jnp.concatenate([b_t, k_t], axis=0).astype(BF16),
                rmb=jnp.concatenate([k_t, b_t], axis=0).astype(BF16),
                bk=jnp.concatenate([b_h, k_h], axis=0).astype(BF16), v=v))
        for h in hv:
            g0 = _dot_nt(jnp.where(m0_2t, h['lm'], 0.0).astype(BF16), h['rma'])
            g1 = _dot_nt(jnp.where(m0_2t, 0.0, h['lm']).astype(BF16), h['rmb'])
            own = jnp.where(m0_2t, g0, g1)
            oth = jnp.where(m0_2t, g1, g0)
            h['a_ab'] = jnp.where(strict, own[:t], 0.0)
            h['a_rb'] = jnp.where(incl, own[t:], 0.0).astype(BF16)
            h['a_k'] = jnp.where(mask_2t, oth, 0.0).astype(BF16)
        for h in hv:
            v = h['v']
            h['inv'] = eye_f + h['a_ab']
            h['nk'] = _dot(h['a_ab'].astype(BF16), bdiag(h['a_ab']).astype(BF16))
            v_rows = jnp.concatenate([jnp.where(m0, 0.0, v), jnp.where(m0, v, 0.0)], axis=0)
            h['av'] = _dot(h['a_k'], v_rows.astype(BF16))
        for s in range(1, NEUMANN_STEPS + 1):
            for h in hv:
                nk_bd = bdiag(h['nk']).astype(BF16)
                if s < NEUMANN_STEPS:
                    nx = _dot(jnp.concatenate([h['nk'], h['inv']], axis=0).astype(BF16), nk_bd)
                    h['nk'] = nx[:t]
                    h['inv'] = h['inv'] + nx[t:]
                else:
                    h['inv'] = h['inv'] + _dot(h['inv'].astype(BF16), nk_bd)
        for h in hv:
            rhs = jnp.concatenate([bdiag(h['a_0']), bdiag(h['av'][:t])], axis=1)
            h['pp'] = _dot(h['inv'].astype(BF16), rhs.astype(BF16))
        for h in hv:
            pp = h['pp']
            rhs = jnp.concatenate([bdiag(pp[:, :w]), bdiag(pp[:, w:])], axis=1)
            h['qq'] = (_dot(h['a_rb'], rhs.astype(BF16))
                       + jnp.concatenate([h['r_0'], h['av'][t:]], axis=1))
            pv = jnp.concatenate([pp, jnp.concatenate([jnp.zeros_like(h['v']), h['v']], axis=1)], axis=0)
            h['mn'] = _dot_tn(h['bk'], pv.astype(BF16))
        for h in hv:
            st = s_ref[h['sq'], h['hp']]
            st_b = st.astype(BF16)
            y_ref[h['sq'], :, h['sl']] = _dot(h['qq'][:, :w].astype(BF16), st_b) + h['qq'][:, w:]
            wcol = jnp.sum(jnp.where(eye_w, jnp.broadcast_to(h['wtot'], (w, w)), 0.0),
                           axis=1, keepdims=True)
            s_ref[h['sq'], h['hp']] = (
                wcol * st + _dot(jnp.where(bd_mask, h['mn'][:, :w], 0.0).astype(BF16), st_b)
                + jnp.where(bd_mask, h['mn'][:, w:], 0.0))
        return carry

    lax.fori_loop(0, RWKV_HEADS // (2 * WKV_PAIRS_PER_STEP), group, 0)


def _wkv(r, kk, v, kd, beta, lw, nseq, rev):
    n = r.shape[0]
    t = WKV_CHUNK
    seq = n // nseq
    nc = seq // t
    ns = WKV_SEQS if nseq % WKV_SEQS == 0 else 1
    if rev:
        idx = lambda b, c: (b, nc - 1 - c, 0)
    else:
        idx = lambda b, c: (b, c, 0)
    spec = pl.BlockSpec((ns, t, D_RWKV), idx)
    y = pl.pallas_call(
        functools.partial(_wkv_kernel, rev=rev),
        out_shape=jax.ShapeDtypeStruct((nseq, seq, D_RWKV), F32),
        grid=(nseq // ns, nc),
        in_specs=[spec] * 6,
        out_specs=spec,
        scratch_shapes=[pltpu.VMEM((ns, RWKV_HEADS // 2, HEAD_PAIR, HEAD_PAIR), F32)],
        compiler_params=_cparams(("parallel", "arbitrary")),
        name="wkv_bwd" if rev else "wkv_fwd",
    )(*[a.reshape(nseq, seq, D_RWKV) for a in (r, kk, v, kd, beta, lw)])
    return y.reshape(n, D_RWKV)


def _tail_kernel(ys_ref, sg_ref, ms_ref, yf_ref, yb_ref, bonus_ref, rg_ref, mr_ref, x_ref,
                 wglu_ref, swout_ref, lnw_ref, lnb_ref, e_ref, et_ref, rwout_ref, wo_ref, g_ref,
                 o_ref, ytok_ref):
    f32 = lambda ref: ref[...].astype(F32)
    rows = ys_ref.shape[0]
    octs = S5_GROUPS // S5_OCT_GROUPS
    for o in range(octs):
        for t in range(S5_CHUNK):
            lane = o * S5_OCT + t * 128
            ytok_ref[o, pl.ds(t, rows, stride=S5_CHUNK), :] = ys_ref[:, lane:lane + 128].astype(F32)
    z = _gelu_tanh(jnp.concatenate([ytok_ref[o] for o in range(octs)], axis=1))
    z = z * _sigmoid(_dot(z.astype(BF16), wglu_ref[...]))
    z = z * _silu(f32(sg_ref))
    h = _sigmoid(f32(ms_ref)) * _dot(z.astype(BF16), swout_ref[...])

    e, et = e_ref[...], et_ref[...]
    y = yf_ref[...] + yb_ref[...]
    inv_n = 1.0 / RWKV_HEAD
    mean = _head_sum_bcast(y, e, et) * inv_n
    yc = y - mean
    var = _head_sum_bcast(yc * yc, e, et) * inv_n
    y = yc * lax.rsqrt(var + GN_EPS) * lnw_ref[...] + lnb_ref[...] + f32(bonus_ref)
    y = y * _silu(f32(rg_ref))
    h = h + _sigmoid(f32(mr_ref)) * _dot(y.astype(BF16), rwout_ref[...])

    out = _dot(h.astype(BF16), wo_ref[...])
    ms = jnp.mean(out * out, axis=-1, keepdims=True)
    o_ref[...] = x_ref[...] + out * lax.rsqrt(ms + RMS_EPS) * g_ref[...]


def _tail(y_s5, proj, yf, yb, bonus, x, p, tm):
    n = x.shape[0]
    row = lambda i: (i, 0)
    const = lambda i: (0, 0)
    half = pl.BlockSpec((tm, D_RWKV), row)
    full = pl.BlockSpec((tm, D_MODEL), row)

    def proj_cols(width, blk):
        return pl.BlockSpec((tm, width), lambda i: (i, blk))

    def resident(shape):
        return pl.BlockSpec(shape, const, pipeline_mode=pl.Buffered(1))

    return pl.pallas_call(
        _tail_kernel,
        out_shape=jax.ShapeDtypeStruct((n, D_MODEL), F32),
        grid=(n // tm,),
        in_specs=[pl.BlockSpec((tm // S5_CHUNK, S5_CHUNK * D_S5), row),
                  proj_cols(D_S5, PROJ_SG_BLK), proj_cols(D_MODEL, PROJ_MG_BLK),
                  half, half, half,
                  proj_cols(D_RWKV, PROJ_RG_BLK), proj_cols(D_MODEL, PROJ_MG_BLK + 1),
                  full,
                  resident((D_S5, D_S5)), resident((D_S5, D_MODEL)),
                  resident((1, D_RWKV)), resident((1, D_RWKV)),
                  resident((D_RWKV, 128)), resident((128, D_RWKV)),
                  resident((D_RWKV, D_MODEL)), resident((D_MODEL, D_MODEL)),
                  resident((1, D_MODEL))],
        out_specs=full,
        scratch_shapes=[pltpu.VMEM((D_S5 // 128, tm, 128), F32)],
        compiler_params=_cparams(("parallel",)),
        name="tail",
    )(y_s5, proj, proj, yf, yb, bonus, proj, proj, x,
      p['w_glu'], p['s5_w_out'], p['ln_w'], p['ln_b'], p['e'], p['et'], p['rw_w_out'], p['w_o'],
      p['post_g'])


def _pick(n, pref):
    while n % pref:
        pref //= 2
    return pref


def _prepare(pre_norm_g, post_norm_g, w_in, s5_lam_re, s5_lam_im, s5_log_dt, s5_b_re, s5_b_im,
             s5_c_re, s5_c_im, s5_d, s5_w_glu, s5_w_out, rw_mu, rw_w0, rw_w_up, rw_a0, rw_a_up,
             rw_k_k, rw_k_a, rw_r_k, rw_ln_w, rw_ln_b, rw_w_out, w_o):
    p = {}
    p['pre_g'] = pre_norm_g.astype(F32).reshape(1, D_MODEL)
    p['post_g'] = post_norm_g.astype(F32).reshape(1, D_MODEL)
    w = w_in.astype(BF16)
    rw_end = 2 * D_S5 + D_RW_IN
    p['w_u'] = w[:, :D_S5]
    p['w_proj'] = jnp.concatenate(
        [w[:, 2 * D_S5:rw_end], jnp.zeros((D_MODEL, PROJ_RW_W - D_RW_IN), BF16),
         w[:, D_S5:2 * D_S5], w[:, rw_end:]], axis=1)
    p['s5_m'], p['s5_p'], p['s5_q'], p['s5_a'] = _s5_weights(
        s5_lam_re, s5_lam_im, s5_log_dt, s5_b_re, s5_b_im, s5_c_re, s5_c_im, s5_d)
    p['w_glu'] = s5_w_glu.astype(BF16)
    p['s5_w_out'] = s5_w_out.astype(BF16)
    p['mu'] = rw_mu.astype(F32).reshape(1, D_RW_IN)
    zeros = jnp.zeros((LORA_RANK, D_RWKV), F32)

    def blockdiag(u):
        return jnp.concatenate([jnp.concatenate([u[0], zeros], axis=1),
                                jnp.concatenate([zeros, u[1]], axis=1)], axis=0).astype(BF16)

    p['wup'] = blockdiag(rw_w_up.astype(F32))
    p['aup'] = blockdiag(rw_a_up.astype(F32))
    p['w0'] = rw_w0.astype(F32).reshape(1, 2 * D_RWKV)
    p['a0'] = rw_a0.astype(F32).reshape(1, 2 * D_RWKV)
    p['k_k'] = rw_k_k.astype(F32).reshape(1, D_RWKV)
    p['k_a'] = rw_k_a.astype(F32).reshape(1, D_RWKV)
    p['r_k'] = rw_r_k.astype(F32).reshape(1, D_RWKV)
    p['ln_w'] = rw_ln_w.astype(F32).reshape(1, D_RWKV)
    p['ln_b'] = rw_ln_b.astype(F32).reshape(1, D_RWKV)
    p['rw_w_out'] = rw_w_out.astype(BF16)
    p['w_o'] = w_o.astype(BF16)
    e = _head_indicator()
    p['e'] = e
    p['et'] = e.T
    return p


def _s5_mix(x2, p, nseq):
    r = x2.shape[0]
    rb = _pick(r, 256)
    (y0,) = _s5_mm(x2, p['s5_m'], rb, 1, "s5_chunk_out")
    pfr, pfi, pbr, pbi = _s5_mm(x2, p['s5_p'], rb, 4, "s5_chunk_state")
    cb = _pick(r // nseq, 128)
    hfr, hfi, hbr, hbi = _s5_scan(p['s5_a'], pfr, pfi, pbr, pbi, nseq, cb, 1024)
    return _s5_state_out(y0, hfr, hfi, hbr, hbi, p['s5_q'], rb)


def _layer(x, p):
    bsz, seq, _ = x.shape
    n = bsz * seq
    x2 = x.reshape(n, D_MODEL)
    u_rows = _in_proj_u(x2, p['pre_g'], p['w_u'], _pick(n, 512))
    proj = _in_proj(x2, p['pre_g'], p['w_proj'], _pick(n, 1024))

    y_s5 = _s5_mix(u_rows, p, bsz)

    (r, kk, v, kdf, kdb, bf, bb, lwf, lwb, bonus) = _rw_pre(
        proj, p['mu'], p['wup'], p['aup'], p['w0'], p['a0'], p['k_k'], p['k_a'], p['r_k'],
        p['e'], p['et'], _pick(seq, 256), seq)
    yf = _wkv(r, kk, v, kdf, bf, lwf, bsz, False)
    yb = _wkv(r, kk, v, kdb, bb, lwb, bsz, True)

    out = _tail(y_s5, proj, yf, yb, bonus, x2, p, _pick(n, 256))
    return out.reshape(bsz, seq, D_MODEL)


def kernel(x_prompt, x_sample, pre_norm_g, post_norm_g, w_in, s5_lam_re, s5_lam_im, s5_log_dt, s5_b_re, s5_b_im, s5_c_re, s5_c_im, s5_d, s5_w_glu, s5_w_out, rw_mu, rw_w0, rw_w_up, rw_a0, rw_a_up, rw_k_k, rw_k_a, rw_r_k, rw_ln_w, rw_ln_b, rw_w_out, w_o):
    params = (pre_norm_g, post_norm_g, w_in, s5_lam_re, s5_lam_im, s5_log_dt, s5_b_re, s5_b_im,
              s5_c_re, s5_c_im, s5_d, s5_w_glu, s5_w_out, rw_mu, rw_w0, rw_w_up, rw_a0, rw_a_up,
              rw_k_k, rw_k_a, rw_r_k, rw_ln_w, rw_ln_b, rw_w_out, w_o)
    y_prompt, y_sample = x_prompt, x_sample
    for layer in range(w_in.shape[0]):
        p = _prepare(*[w[layer] for w in params])
        y_prompt = _layer(y_prompt, p)
        y_sample = _layer(y_sample, p)
    return (y_prompt, y_sample)
```

```python
import functools
import math

import jax
import jax.numpy as jnp
from jax import lax
from jax.experimental import pallas as pl
from jax.experimental.pallas import tpu as pltpu

F32 = jnp.float32
BF16 = jnp.bfloat16

D_MODEL = 2048
D_S5 = 1024
S5_GROUP = 16
S5_GROUPS = 64
S5_STATE = 64
D_RWKV = 1024
RWKV_HEAD = 64
RWKV_HEADS = 16
LORA_RANK = 64
D_RW_IN = 3 * D_RWKV + 4 * LORA_RANK
RMS_EPS = 1e-6
GN_EPS = 64e-5
L2_EPS = 1e-12

S5_CHUNK = 16
S5_OCT_GROUPS = 8
S5_OCT = S5_OCT_GROUPS * S5_CHUNK * S5_GROUP
WKV_CHUNK = 64
HEAD_PAIR = 2 * RWKV_HEAD
NEUMANN_STEPS = 5
WKV_PAIRS_PER_STEP = 8
WKV_SEQS = 2
HALO = 16

VMEM_LIMIT = 48 * 1024 * 1024


def _cparams(sem):
    return pltpu.CompilerParams(dimension_semantics=sem, vmem_limit_bytes=VMEM_LIMIT)


def _dot(a, b):
    return jnp.dot(a, b, preferred_element_type=F32)


def _dot_nt(a, b):
    return lax.dot_general(a, b, (((1,), (1,)), ((), ())), preferred_element_type=F32)


def _dot_tn(a, b):
    return lax.dot_general(a, b, (((0,), (0,)), ((), ())), preferred_element_type=F32)


def _sigmoid(x):
    return 1.0 / (1.0 + jnp.exp2(x * (-math.log2(math.e))))


def _silu(x):
    return x * _sigmoid(x)


def _gelu_tanh(x):
    c = math.sqrt(2.0 / math.pi)
    return 0.5 * x * (1.0 + jnp.tanh(c * (x + 0.044715 * (x * x * x))))


IN_TN = 1024
PROJ_RW_W = 4096
PROJ_W = PROJ_RW_W + D_S5 + D_RWKV + 2 * D_MODEL
PROJ_SG_BLK = PROJ_RW_W // D_S5
PROJ_RG_BLK = PROJ_SG_BLK + 1
PROJ_MG_BLK = (PROJ_RW_W + D_S5 + D_RWKV) // D_MODEL


def _rms_bf16(x, g):
    ms = jnp.mean(x * x, axis=-1, keepdims=True)
    return (x * lax.rsqrt(ms + RMS_EPS) * g).astype(BF16)


def _in_proj_u_kernel(x_ref, g_ref, w_ref, o_ref, ubuf_ref):
    res = _dot(_rms_bf16(x_ref[...], g_ref[...]), w_ref[...])
    rows = ubuf_ref.shape[1] // S5_CHUNK
    for o in range(ubuf_ref.shape[0]):
        ubuf_ref[o] = res[:, o * 128:(o + 1) * 128]
        for t in range(S5_CHUNK):
            tok = ubuf_ref[o, pl.ds(t, rows, stride=S5_CHUNK), :]
            lane = o * S5_OCT + t * 128
            o_ref[:, lane:lane + 128] = tok.astype(BF16)


def _in_proj_u(x, g, w_u, tm):
    n, d = x.shape
    return pl.pallas_call(
        _in_proj_u_kernel,
        out_shape=jax.ShapeDtypeStruct((n // S5_CHUNK, D_S5 * S5_CHUNK), BF16),
        grid=(n // tm,),
        in_specs=[pl.BlockSpec((tm, d), lambda i: (i, 0)),
                  pl.BlockSpec((1, d), lambda i: (0, 0)),
                  pl.BlockSpec((d, D_S5), lambda i: (0, 0))],
        out_specs=pl.BlockSpec((tm // S5_CHUNK, D_S5 * S5_CHUNK), lambda i: (i, 0)),
        scratch_shapes=[pltpu.VMEM((D_S5 // 128, tm, 128), F32)],
        compiler_params=_cparams(("parallel",)),
        name="in_proj_u",
    )(x, g, w_u)


def _in_proj_kernel(x_ref, g_ref, w_ref, o_ref, hn_ref):
    @pl.when(pl.program_id(1) == 0)
    def _():
        hn_ref[...] = _rms_bf16(x_ref[...], g_ref[...])

    o_ref[...] = _dot(hn_ref[...], w_ref[...]).astype(BF16)


def _in_proj(x, g, w, tm):
    n, d = x.shape
    return pl.pallas_call(
        _in_proj_kernel,
        out_shape=jax.ShapeDtypeStruct((n, w.shape[1]), BF16),
        grid=(n // tm, w.shape[1] // IN_TN),
        in_specs=[pl.BlockSpec((tm, d), lambda i, j: (i, 0)),
                  pl.BlockSpec((1, d), lambda i, j: (0, 0)),
                  pl.BlockSpec((d, IN_TN), lambda i, j: (0, j))],
        out_specs=pl.BlockSpec((tm, IN_TN), lambda i, j: (i, j)),
        scratch_shapes=[pltpu.VMEM((tm, d), BF16)],
        compiler_params=_cparams(("parallel", "arbitrary")),
        name="in_proj",
    )(x, g, w)


def _s5_weights(lam_re, lam_im, log_dt, b_re, b_im, c_re, c_im, d):
    t = S5_CHUNK
    g, p, h = S5_GROUPS, S5_STATE, S5_GROUP
    hp = lax.Precision.HIGHEST
    lam = lax.complex(lam_re.astype(F32), lam_im.astype(F32))
    dt = jnp.exp(log_dt.astype(F32))[..., None]
    lam_dt = lam * dt
    lam_bar = jnp.exp(lam_dt)
    bbar = ((lam_bar - 1.0) / lam)[..., None] * lax.complex(b_re.astype(F32), b_im.astype(F32))
    c = lax.complex(c_re.astype(F32), c_im.astype(F32))
    steps = jnp.arange(t + 1, dtype=F32)
    pw = jnp.exp(lam_dt[None] * steps[:, None, None, None])

    kern = jnp.einsum('dghp,ldgp,dgpj->dglhj', c, pw[:t], bbar, precision=hp).real
    tt = jnp.arange(t)
    lag = tt[None, :] - tt[:, None]
    kf = kern[0][:, jnp.abs(lag)]
    kb = kern[1][:, jnp.abs(lag)]
    k0 = kern[0][:, 0] + kern[1][:, 0] + jnp.eye(h, dtype=F32) * d.astype(F32).reshape(g, h)[:, :, None]
    lag5 = lag[None, :, :, None, None]
    m = jnp.where(lag5 > 0, kf, jnp.where(lag5 < 0, kb, k0[:, None, None]))
    m = m.transpose(0, 1, 4, 2, 3).reshape(g, t * h, t * h)

    pf = pw[t - 1 - tt, 0][..., None] * bbar[0][None]
    pb = pw[tt, 1][..., None] * bbar[1][None]

    def _p_mat(x):
        return x.transpose(1, 0, 3, 2).reshape(g, t * h, p)

    p_parts = [_p_mat(pf.real), _p_mat(pf.imag), _p_mat(pb.real), _p_mat(pb.imag)]

    cf = c[0][None] * pw[tt + 1, 0][:, :, None, :]
    cb = c[1][None] * pw[t - tt, 1][:, :, None, :]

    def _q_mat(x):
        return x.transpose(1, 3, 0, 2).reshape(g, p, t * h)

    q_parts = [_q_mat(cf.real), _q_mat(-cf.imag), _q_mat(cb.real), _q_mat(-cb.imag)]

    k8 = S5_OCT_GROUPS
    no = g // k8
    gi = jnp.arange(k8)

    def _spread(width):
        c = jnp.arange(width)
        tgt = (gi[:, None] * width + c[None, :])[:, :, None]
        return (tgt == jnp.arange(k8 * width)[None, None, :]).astype(BF16)

    def _spread_tok():
        u, hh = jnp.arange(t * h) // h, jnp.arange(t * h) % h
        tgt = (u[None, :] * (k8 * h) + gi[:, None] * h + hh[None, :])[:, :, None]
        return (tgt == jnp.arange(S5_OCT)[None, None, :]).astype(BF16)

    def _cols(x, spread):
        x = x.astype(BF16).reshape(no, k8, x.shape[1], x.shape[2])
        return jnp.einsum('ogrc,gcd->ogrd', x, spread, preferred_element_type=BF16)

    def _rows_tok(x):
        c = x.shape[-1]
        return x.reshape(no, k8, t, h, c).transpose(0, 2, 1, 3, 4).reshape(no, S5_OCT, c)

    sp_tok, sp_state = _spread_tok(), _spread(p)
    m_oct = _rows_tok(_cols(m, sp_tok))
    w_p = jnp.concatenate([_rows_tok(_cols(x, sp_state)) for x in p_parts], axis=2)
    qmat = jnp.concatenate([_cols(x, sp_tok).reshape(no, k8 * p, S5_OCT) for x in q_parts],
                           axis=1)
    at = pw[t]
    a_rows = jnp.stack([at[0].real, at[0].imag, at[1].real, at[1].imag]).reshape(4, g * p)
    return m_oct, w_p, qmat, a_rows


def _s5_mm_kernel(x_ref, w_ref, *o_refs):
    res = _dot(x_ref[...], w_ref[0])
    n = res.shape[1] // len(o_refs)
    for k, ref in enumerate(o_refs):
        ref[...] = res[:, n * k:n * (k + 1)].astype(ref.dtype)


def _s5_mm(x2, w, rb, nout, name):
    r = x2.shape[0]
    no, _, c = w.shape
    cw = c // nout
    return pl.pallas_call(
        _s5_mm_kernel,
        out_shape=tuple(jax.ShapeDtypeStruct((r, no * cw), F32) for _ in range(nout)),
        grid=(no, r // rb),
        in_specs=[pl.BlockSpec((rb, S5_OCT), lambda o, i: (i, o)),
                  pl.BlockSpec((1, S5_OCT, c), lambda o, i: (o, 0, 0))],
        out_specs=tuple(pl.BlockSpec((rb, cw), lambda o, i: (i, o)) for _ in range(nout)),
        compiler_params=_cparams(("parallel", "arbitrary")),
        name=name,
    )(x2, w)


def _s5_scan_kernel(a_ref, pfr_ref, pfi_ref, pbr_ref, pbi_ref,
                    hfr_ref, hfi_ref, hbr_ref, hbi_ref, carry_ref, *, rows):
    @pl.when(pl.program_id(2) == 0)
    def _():
        carry_ref[...] = jnp.zeros_like(carry_ref)

    afr, afi = a_ref[0:1, :], a_ref[1:2, :]
    abr, abi = a_ref[2:3, :], a_ref[3:4, :]

    def tile(i, carry):
        hfr, hfi, hbr, hbi = carry
        r0 = pl.multiple_of(i * 8, 8)
        pfr, pfi = pfr_ref[pl.ds(r0, 8), :], pfi_ref[pl.ds(r0, 8), :]
        rb0 = pl.multiple_of(rows - 8 - i * 8, 8)
        pbr, pbi = pbr_ref[pl.ds(rb0, 8), :], pbi_ref[pl.ds(rb0, 8), :]
        of_r, of_i, ob_r, ob_i = [], [], [None] * 8, [None] * 8
        for s in range(8):
            of_r.append(hfr)
            of_i.append(hfi)
            hfr, hfi = (afr * hfr - afi * hfi + pfr[s:s + 1, :],
                        afr * hfi + afi * hfr + pfi[s:s + 1, :])
            sb = 7 - s
            ob_r[sb] = hbr
            ob_i[sb] = hbi
            hbr, hbi = (abr * hbr - abi * hbi + pbr[sb:sb + 1, :],
                        abr * hbi + abi * hbr + pbi[sb:sb + 1, :])
        hfr_ref[pl.ds(r0, 8), :] = jnp.concatenate(of_r, axis=0)
        hfi_ref[pl.ds(r0, 8), :] = jnp.concatenate(of_i, axis=0)
        hbr_ref[pl.ds(rb0, 8), :] = jnp.concatenate(ob_r, axis=0)
        hbi_ref[pl.ds(rb0, 8), :] = jnp.concatenate(ob_i, axis=0)
        return hfr, hfi, hbr, hbi

    init = tuple(carry_ref[k:k + 1, :] for k in range(4))
    out = lax.fori_loop(0, rows // 8, tile, init)
    for k in range(4):
        carry_ref[k:k + 1, :] = out[k]


def _s5_scan(a_rows, pfr, pfi, pbr, pbi, nseq, cb, lt):
    r, w = pfr.shape
    nblk = r // nseq // cb
    fspec = pl.BlockSpec((cb, lt), lambda b, j, i: (b * nblk + i, j))
    bspec = pl.BlockSpec((cb, lt), lambda b, j, i: (b * nblk + nblk - 1 - i, j))
    st = jax.ShapeDtypeStruct((r, w), F32)
    return pl.pallas_call(
        functools.partial(_s5_scan_kernel, rows=cb),
        out_shape=(st, st, st, st),
        grid=(nseq, w // lt, nblk),
        in_specs=[pl.BlockSpec((4, lt), lambda b, j, i: (0, j)), fspec, fspec, bspec, bspec],
        out_specs=(fspec, fspec, bspec, bspec),
        scratch_shapes=[pltpu.VMEM((4, lt), F32)],
        compiler_params=_cparams(("parallel", "parallel", "arbitrary")),
        name="s5_scan",
    )(a_rows, pfr, pfi, pbr, pbi)


def _s5_state_out_kernel(y0_ref, hfr_ref, hfi_ref, hbr_ref, hbi_ref, q_ref, y_ref):
    h = jnp.concatenate([hfr_ref[...], hfi_ref[...], hbr_ref[...], hbi_ref[...]], axis=1)
    y_ref[...] = (y0_ref[...] + _dot(h.astype(BF16), q_ref[0])).astype(BF16)


def _s5_state_out(y0, hfr, hfi, hbr, hbi, qmat, rb):
    r = y0.shape[0]
    no, c, _ = qmat.shape
    hspec = pl.BlockSpec((rb, c // 4), lambda o, i: (i, o))
    yspec = pl.BlockSpec((rb, S5_OCT), lambda o, i: (i, o))
    return pl.pallas_call(
        _s5_state_out_kernel,
        out_shape=jax.ShapeDtypeStruct(y0.shape, BF16),
        grid=(no, r // rb),
        in_specs=[yspec, hspec, hspec, hspec, hspec,
                  pl.BlockSpec((1, c, S5_OCT), lambda o, i: (o, 0, 0))],
        out_specs=yspec,
        compiler_params=_cparams(("parallel", "arbitrary")),
        name="s5_state_out",
    )(y0, hfr, hfi, hbr, hbi, qmat)


def _head_indicator():
    lane = lax.broadcasted_iota(jnp.int32, (D_RWKV, 128), 0) // RWKV_HEAD
    col = lax.broadcasted_iota(jnp.int32, (D_RWKV, 128), 1)
    return (lane == col).astype(BF16)


def _head_sum_bcast(x, e, et):
    return _dot(_dot(x.astype(BF16), e).astype(BF16), et)


def _rw_pre_kernel(x_ref, prev_ref, next_ref, mu_ref, wup_ref, aup_ref, w0_ref, a0_ref,
                   kk_ref_p, ka_ref_p, rk_ref_p, e_ref, et_ref,
                   r_ref, kk_ref, v_ref, kdf_ref, kdb_ref, bf_ref, bb_ref, lwf_ref, lwb_ref,
                   bonus_ref, *, tm, seq):
    i = pl.program_id(0)
    xb = x_ref[...]
    first = (i * tm) % seq == 0
    last = ((i + 1) * tm) % seq == 0
    zero = jnp.zeros((HALO, xb.shape[1]), BF16)
    x_ext = jnp.concatenate([jnp.where(first, zero, prev_ref[...]), xb,
                             jnp.where(last, zero, next_ref[...])], axis=0)
    ri = lax.broadcasted_iota(jnp.int32, (tm, tm + 2 * HALO), 0) + HALO
    ci = lax.broadcasted_iota(jnp.int32, (tm, tm + 2 * HALO), 1)
    band = ((ci == ri - 1) | (ci == ri + 1)).astype(BF16)
    x = xb.astype(F32)
    z = x + (0.5 * _dot(band, x_ext) - x) * mu_ref[...]

    r = z[:, :D_RWKV]
    k = z[:, D_RWKV:2 * D_RWKV]
    v = z[:, 2 * D_RWKV:3 * D_RWKV]
    xw = z[:, 3 * D_RWKV:3 * D_RWKV + 2 * LORA_RANK]
    xa = z[:, 3 * D_RWKV + 2 * LORA_RANK:]
    e, et = e_ref[...], et_ref[...]

    kk = k * kk_ref_p[...]
    kk = kk * lax.rsqrt(_head_sum_bcast(kk * kk, e, et) + L2_EPS)
    bonus_ref[...] = (_head_sum_bcast(r * k * rk_ref_p[...], e, et) * v).astype(BF16)
    r_ref[...] = r.astype(BF16)
    kk_ref[...] = kk.astype(BF16)
    v_ref[...] = v.astype(BF16)

    wpre = _dot(jnp.tanh(xw).astype(BF16), wup_ref[...]) + w0_ref[...]
    apre = _dot(xa.astype(BF16), aup_ref[...]) + a0_ref[...]
    lw = -math.exp(-0.5) * _sigmoid(wpre)
    a = _sigmoid(apre)
    ka = ka_ref_p[...]
    lwf_ref[...] = lw[:, :D_RWKV]
    lwb_ref[...] = lw[:, D_RWKV:]
    af, ab = a[:, :D_RWKV], a[:, D_RWKV:]
    kdf_ref[...] = (k * (1.0 + (af - 1.0) * ka)).astype(BF16)
    kdb_ref[...] = (k * (1.0 + (ab - 1.0) * ka)).astype(BF16)
    bf_ref[...] = (kk * af).astype(BF16)
    bb_ref[...] = (kk * ab).astype(BF16)


def _rw_pre(rw_in, mu, wup, aup, w0, a0, k_k, k_a, r_k, e, et, tm, seq):
    n = rw_in.shape[0]
    w = D_RW_IN
    nbh = n // HALO
    th = tm // HALO
    row = lambda i: (i, 0)
    const = lambda i: (0, 0)
    ospec = pl.BlockSpec((tm, D_RWKV), row)
    sb = jax.ShapeDtypeStruct((n, D_RWKV), BF16)
    sf = jax.ShapeDtypeStruct((n, D_RWKV), F32)
    return pl.pallas_call(
        functools.partial(_rw_pre_kernel, tm=tm, seq=seq),
        out_shape=(sb,) * 7 + (sf, sf, sb),
        grid=(n // tm,),
        in_specs=[pl.BlockSpec((tm, w), row),
                  pl.BlockSpec((HALO, w), lambda i: (jnp.maximum(i * th - 1, 0), 0)),
                  pl.BlockSpec((HALO, w), lambda i: (jnp.minimum((i + 1) * th, nbh - 1), 0)),
                  pl.BlockSpec((1, w), const),
                  pl.BlockSpec((2 * LORA_RANK, 2 * D_RWKV), const),
                  pl.BlockSpec((2 * LORA_RANK, 2 * D_RWKV), const),
                  pl.BlockSpec((1, 2 * D_RWKV), const),
                  pl.BlockSpec((1, 2 * D_RWKV), const),
                  pl.BlockSpec((1, D_RWKV), const),
                  pl.BlockSpec((1, D_RWKV), const),
                  pl.BlockSpec((1, D_RWKV), const),
                  pl.BlockSpec((D_RWKV, 128), const),
                  pl.BlockSpec((128, D_RWKV), const)],
        out_specs=(ospec,) * 10,
        compiler_params=_cparams(("parallel",)),
        name="rw_pre",
    )(rw_in, rw_in, rw_in, mu, wup, aup, w0, a0, k_k, k_a, r_k, e, et)


def _wkv_kernel(rf_ref, kkf_ref, vf_ref, kdf_ref, bf_ref, lwf_ref,
                rb_ref, kkb_ref, vb_ref, kdb_ref, bb_ref, lwb_ref, yf_ref, yb_ref, s_ref):
    t = WKV_CHUNK
    hd = RWKV_HEAD
    w = HEAD_PAIR

    @pl.when(pl.program_id(1) == 0)
    def _():
        s_ref[...] = jnp.zeros_like(s_ref)

    def iota(shape, dim):
        return lax.broadcasted_iota(jnp.int32, shape, dim)

    rr, cc = iota((t, t), 0), iota((t, t), 1)
    rw_, lane = iota((t, w), 0), iota((t, w), 1)
    col = lane % hd
    m0 = lane < hd
    eye_f = (rw_ == col).astype(F32)
    m0_2t = iota((2 * t, w), 1) < hd
    eye_w = iota((w, w), 0) == iota((w, w), 1)
    bd_mask = (iota((w, w), 0) < hd) == (iota((w, w), 1) < hd)

    def direction(rev, refs, y_ref):
        if rev:
            incl, strict, tri = rw_ <= col, rw_ < col, rr <= cc
        else:
            incl, strict, tri = rw_ >= col, rw_ > col, rr >= cc
        return dict(rev=int(rev), refs=refs, y_ref=y_ref, incl=incl, strict=strict,
                    tri=tri.astype(BF16),
                    mask_2t=jnp.concatenate([strict, incl], axis=0), t_last=0 if rev else t - 1)

    dirs = (direction(False, (rf_ref, kkf_ref, vf_ref, kdf_ref, bf_ref, lwf_ref), yf_ref),
            direction(True, (rb_ref, kkb_ref, vb_ref, kdb_ref, bb_ref, lwb_ref), yb_ref))

    def bdiag(x):
        return jnp.concatenate([jnp.where(m0, x, 0.0), jnp.where(m0, 0.0, x)], axis=0)

    def group(gi, carry):
        hv = []
        for dm, sq, j in [(dm, sq, j) for dm in dirs for sq in range(rf_ref.shape[0])
                          for j in range(WKV_PAIRS_PER_STEP)]:
            hp = gi * WKV_PAIRS_PER_STEP + j
            sl = pl.ds(pl.multiple_of(hp * HEAD_PAIR, HEAD_PAIR), HEAD_PAIR)
            r, kk, v, kd, beta = (ref[sq, :, sl].astype(F32) for ref in dm['refs'][:5])
            lw = dm['refs'][5][sq, :, sl]
            tri, t_last = dm['tri'], dm['t_last']
            lw_hi = lw.astype(BF16)
            lw_lo = (lw - lw_hi.astype(F32)).astype(BF16)
            lcum = _dot(tri, lw_hi) + _dot(tri, lw_lo)
            lcum_x = lcum - lw
            cref = lcum[t // 2:t // 2 + 1, :]
            ltot = lcum[t_last:t_last + 1, :]
            e1 = jnp.exp(lcum - cref)
            e1x = jnp.exp(lcum_x - cref)
            e2 = jnp.exp(cref - lcum)
            ec = jnp.exp(cref)
            ewt = jnp.exp(ltot - cref)
            wtot = jnp.exp(ltot)
            r_t = r * e1
            a_t = -kk * e1x
            b_t = beta * e2
            k_t = kd * e2
            a_0 = a_t * ec
            r_0 = r_t * ec
            b_h = b_t * ewt
            k_h = k_t * ewt
            hv.append(dict(
                dm=dm, sq=sq, hp=hp, sl=sl, wtot=wtot, a_0=a_0, r_0=r_0,
                lm=jnp.concatenate([a_t, r_t], axis=0),
                rma=jnp.concatenate([b_t, k_t], axis=0).astype(BF16),
                rmb=jnp.concatenate([k_t, b_t], axis=0).astype(BF16),
                bk=jnp.concatenate([b_h, k_h], axis=0).astype(BF16), v=v))
        for h in hv:
            g0 = _dot_nt(jnp.where(m0_2t, h['lm'], 0.0).astype(BF16), h['rma'])
            g1 = _dot_nt(jnp.where(m0_2t, 0.0, h['lm']).astype(BF16), h['rmb'])
            own = jnp.where(m0_2t, g0, g1)
            oth = jnp.where(m0_2t, g1, g0)
            h['a_ab'] = jnp.where(h['dm']['strict'], own[:t], 0.0)
            h['a_rb'] = jnp.where(h['dm']['incl'], own[t:], 0.0).astype(BF16)
            h['a_k'] = jnp.where(h['dm']['mask_2t'], oth, 0.0).astype(BF16)
        for h in hv:
            v = h['v']
            h['inv'] = eye_f + h['a_ab']
            h['nk'] = _dot(h['a_ab'].astype(BF16), bdiag(h['a_ab']).astype(BF16))
            v_rows = jnp.concatenate([jnp.where(m0, 0.0, v), jnp.where(m0, v, 0.0)], axis=0)
            h['av'] = _dot(h['a_k'], v_rows.astype(BF16))
        for s in range(1, NEUMANN_STEPS + 1):
            for h in hv:
                nk_bd = bdiag(h['nk']).astype(BF16)
                if s < NEUMANN_STEPS:
                    nx = _dot(jnp.concatenate([h['nk'], h['inv']], axis=0).astype(BF16), nk_bd)
                    h['nk'] = nx[:t]
                    h['inv'] = h['inv'] + nx[t:]
                else:
                    h['inv'] = h['inv'] + _dot(h['inv'].astype(BF16), nk_bd)
        for h in hv:
            rhs = jnp.concatenate([bdiag(h['a_0']), bdiag(h['av'][:t])], axis=1)
            h['pp'] = _dot(h['inv'].astype(BF16), rhs.astype(BF16))
        for h in hv:
            pp = h['pp']
            rhs = jnp.concatenate([bdiag(pp[:, :w]), bdiag(pp[:, w:])], axis=1)
            h['qq'] = (_dot(h['a_rb'], rhs.astype(BF16))
                       + jnp.concatenate([h['r_0'], h['av'][t:]], axis=1))
            pv = jnp.concatenate([pp, jnp.concatenate([jnp.zeros_like(h['v']), h['v']], axis=1)], axis=0)
            h['mn'] = _dot_tn(h['bk'], pv.astype(BF16))
        for h in hv:
            sidx = (h['dm']['rev'], h['sq'], h['hp'])
            st = s_ref[sidx]
            st_b = st.astype(BF16)
            h['dm']['y_ref'][h['sq'], :, h['sl']] = (
                _dot(h['qq'][:, :w].astype(BF16), st_b) + h['qq'][:, w:])
            wcol = jnp.sum(jnp.where(eye_w, jnp.broadcast_to(h['wtot'], (w, w)), 0.0),
                           axis=1, keepdims=True)
            s_ref[sidx] = (
                wcol * st + _dot(jnp.where(bd_mask, h['mn'][:, :w], 0.0).astype(BF16), st_b)
                + jnp.where(bd_mask, h['mn'][:, w:], 0.0))
        return carry

    lax.fori_loop(0, RWKV_HEADS // (2 * WKV_PAIRS_PER_STEP), group, 0)


def _wkv(r, kk, v, kdf, bf, lwf, kdb, bb, lwb, nseq):
    n = r.shape[0]
    t = WKV_CHUNK
    seq = n // nseq
    nc = seq // t
    ns = WKV_SEQS if nseq % WKV_SEQS == 0 else 1
    fwd = pl.BlockSpec((ns, t, D_RWKV), lambda b, c: (b, c, 0))
    bwd = pl.BlockSpec((ns, t, D_RWKV), lambda b, c: (b, nc - 1 - c, 0))
    shape = jax.ShapeDtypeStruct((nseq, seq, D_RWKV), F32)
    yf, yb = pl.pallas_call(
        _wkv_kernel,
        out_shape=(shape, shape),
        grid=(nseq // ns, nc),
        in_specs=[fwd] * 6 + [bwd] * 6,
        out_specs=(fwd, bwd),
        scratch_shapes=[pltpu.VMEM((2, ns, RWKV_HEADS // 2, HEAD_PAIR, HEAD_PAIR), F32)],
        compiler_params=_cparams(("parallel", "arbitrary")),
        name="wkv",
    )(*[a.reshape(nseq, seq, D_RWKV) for a in (r, kk, v, kdf, bf, lwf, r, kk, v, kdb, bb, lwb)])
    return yf.reshape(n, D_RWKV), yb.reshape(n, D_RWKV)


def _tail_kernel(ys_ref, sg_ref, ms_ref, yf_ref, yb_ref, bonus_ref, rg_ref, mr_ref, x_ref,
                 wglu_ref, swout_ref, lnw_ref, lnb_ref, e_ref, et_ref, rwout_ref, wo_ref, g_ref,
                 o_ref, ytok_ref):
    f32 = lambda ref: ref[...].astype(F32)
    rows = ys_ref.shape[0]
    octs = S5_GROUPS // S5_OCT_GROUPS
    for o in range(octs):
        for t in range(S5_CHUNK):
            lane = o * S5_OCT + t * 128
            ytok_ref[o, pl.ds(t, rows, stride=S5_CHUNK), :] = ys_ref[:, lane:lane + 128].astype(F32)
    z = _gelu_tanh(jnp.concatenate([ytok_ref[o] for o in range(octs)], axis=1))
    z = z * _sigmoid(_dot(z.astype(BF16), wglu_ref[...]))
    z = z * _silu(f32(sg_ref))
    h = _sigmoid(f32(ms_ref)) * _dot(z.astype(BF16), swout_ref[...])

    e, et = e_ref[...], et_ref[...]
    y = yf_ref[...] + yb_ref[...]
    inv_n = 1.0 / RWKV_HEAD
    mean = _head_sum_bcast(y, e, et) * inv_n
    yc = y - mean
    var = _head_sum_bcast(yc * yc, e, et) * inv_n
    y = yc * lax.rsqrt(var + GN_EPS) * lnw_ref[...] + lnb_ref[...] + f32(bonus_ref)
    y = y * _silu(f32(rg_ref))
    h = h + _sigmoid(f32(mr_ref)) * _dot(y.astype(BF16), rwout_ref[...])

    out = _dot(h.astype(BF16), wo_ref[...])
    ms = jnp.mean(out * out, axis=-1, keepdims=True)
    o_ref[...] = x_ref[...] + out * lax.rsqrt(ms + RMS_EPS) * g_ref[...]


def _tail(y_s5, proj, yf, yb, bonus, x, p, tm):
    n = x.shape[0]
    row = lambda i: (i, 0)
    const = lambda i: (0, 0)
    half = pl.BlockSpec((tm, D_RWKV), row)
    full = pl.BlockSpec((tm, D_MODEL), row)

    def proj_cols(width, blk):
        return pl.BlockSpec((tm, width), lambda i: (i, blk))

    def resident(shape):
        return pl.BlockSpec(shape, const, pipeline_mode=pl.Buffered(1))

    return pl.pallas_call(
        _tail_kernel,
        out_shape=jax.ShapeDtypeStruct((n, D_MODEL), F32),
        grid=(n // tm,),
        in_specs=[pl.BlockSpec((tm // S5_CHUNK, S5_CHUNK * D_S5), row),
                  proj_cols(D_S5, PROJ_SG_BLK), proj_cols(D_MODEL, PROJ_MG_BLK),
                  half, half, half,
                  proj_cols(D_RWKV, PROJ_RG_BLK), proj_cols(D_MODEL, PROJ_MG_BLK + 1),
                  full,
                  resident((D_S5, D_S5)), resident((D_S5, D_MODEL)),
                  resident((1, D_RWKV)), resident((1, D_RWKV)),
                  resident((D_RWKV, 128)), resident((128, D_RWKV)),
                  resident((D_RWKV, D_MODEL)), resident((D_MODEL, D_MODEL)),
                  resident((1, D_MODEL))],
        out_specs=full,
        scratch_shapes=[pltpu.VMEM((D_S5 // 128, tm, 128), F32)],
        compiler_params=_cparams(("parallel",)),
        name="tail",
    )(y_s5, proj, proj, yf, yb, bonus, proj, proj, x,
      p['w_glu'], p['s5_w_out'], p['ln_w'], p['ln_b'], p['e'], p['et'], p['rw_w_out'], p['w_o'],
      p['post_g'])


def _pick(n, pref):
    while n % pref:
        pref //= 2
    return pref


def _prepare(pre_norm_g, post_norm_g, w_in, s5_lam_re, s5_lam_im, s5_log_dt, s5_b_re, s5_b_im,
             s5_c_re, s5_c_im, s5_d, s5_w_glu, s5_w_out, rw_mu, rw_w0, rw_w_up, rw_a0, rw_a_up,
             rw_k_k, rw_k_a, rw_r_k, rw_ln_w, rw_ln_b, rw_w_out, w_o):
    p = {}
    p['pre_g'] = pre_norm_g.astype(F32).reshape(1, D_MODEL)
    p['post_g'] = post_norm_g.astype(F32).reshape(1, D_MODEL)
    w = w_in.astype(BF16)
    rw_end = 2 * D_S5 + D_RW_IN
    p['w_u'] = w[:, :D_S5]
    p['w_proj'] = jnp.concatenate(
        [w[:, 2 * D_S5:rw_end], jnp.zeros((D_MODEL, PROJ_RW_W - D_RW_IN), BF16),
         w[:, D_S5:2 * D_S5], w[:, rw_end:]], axis=1)
    p['s5_m'], p['s5_p'], p['s5_q'], p['s5_a'] = _s5_weights(
        s5_lam_re, s5_lam_im, s5_log_dt, s5_b_re, s5_b_im, s5_c_re, s5_c_im, s5_d)
    p['w_glu'] = s5_w_glu.astype(BF16)
    p['s5_w_out'] = s5_w_out.astype(BF16)
    p['mu'] = rw_mu.astype(F32).reshape(1, D_RW_IN)
    zeros = jnp.zeros((LORA_RANK, D_RWKV), F32)

    def blockdiag(u):
        return jnp.concatenate([jnp.concatenate([u[0], zeros], axis=1),
                                jnp.concatenate([zeros, u[1]], axis=1)], axis=0).astype(BF16)

    p['wup'] = blockdiag(rw_w_up.astype(F32))
    p['aup'] = blockdiag(rw_a_up.astype(F32))
    p['w0'] = rw_w0.astype(F32).reshape(1, 2 * D_RWKV)
    p['a0'] = rw_a0.astype(F32).reshape(1, 2 * D_RWKV)
    p['k_k'] = rw_k_k.astype(F32).reshape(1, D_RWKV)
    p['k_a'] = rw_k_a.astype(F32).reshape(1, D_RWKV)
    p['r_k'] = rw_r_k.astype(F32).reshape(1, D_RWKV)
    p['ln_w'] = rw_ln_w.astype(F32).reshape(1, D_RWKV)
    p['ln_b'] = rw_ln_b.astype(F32).reshape(1, D_RWKV)
    p['rw_w_out'] = rw_w_out.astype(BF16)
    p['w_o'] = w_o.astype(BF16)
    e = _head_indicator()
    p['e'] = e
    p['et'] = e.T
    return p


def _s5_mix(x2, p, nseq):
    r = x2.shape[0]
    rb = _pick(r, 256)
    (y0,) = _s5_mm(x2, p['s5_m'], rb, 1, "s5_chunk_out")
    pfr, pfi, pbr, pbi = _s5_mm(x2, p['s5_p'], rb, 4, "s5_chunk_state")
    cb = _pick(r // nseq, 128)
    hfr, hfi, hbr, hbi = _s5_scan(p['s5_a'], pfr, pfi, pbr, pbi, nseq, cb, 1024)
    return _s5_state_out(y0, hfr, hfi, hbr, hbi, p['s5_q'], rb)


def _layer(x, p):
    bsz, seq, _ = x.shape
    n = bsz * seq
    x2 = x.reshape(n, D_MODEL)
    u_rows = _in_proj_u(x2, p['pre_g'], p['w_u'], _pick(n, 512))
    proj = _in_proj(x2, p['pre_g'], p['w_proj'], _pick(n, 1024))

    y_s5 = _s5_mix(u_rows, p, bsz)

    (r, kk, v, kdf, kdb, bf, bb, lwf, lwb, bonus) = _rw_pre(
        proj, p['mu'], p['wup'], p['aup'], p['w0'], p['a0'], p['k_k'], p['k_a'], p['r_k'],
        p['e'], p['et'], _pick(seq, 256), seq)
    yf, yb = _wkv(r, kk, v, kdf, bf, lwf, kdb, bb, lwb, bsz)

    out = _tail(y_s5, proj, yf, yb, bonus, x2, p, _pick(n, 256))
    return out.reshape(bsz, seq, D_MODEL)


def kernel(x_prompt, x_sample, pre_norm_g, post_norm_g, w_in, s5_lam_re, s5_lam_im, s5_log_dt, s5_b_re, s5_b_im, s5_c_re, s5_c_im, s5_d, s5_w_glu, s5_w_out, rw_mu, rw_w0, rw_w_up, rw_a0, rw_a_up, rw_k_k, rw_k_a, rw_r_k, rw_ln_w, rw_ln_b, rw_w_out, w_o):
    params = (pre_norm_g, post_norm_g, w_in, s5_lam_re, s5_lam_im, s5_log_dt, s5_b_re, s5_b_im,
              s5_c_re, s5_c_im, s5_d, s5_w_glu, s5_w_out, rw_mu, rw_w0, rw_w_up, rw_a0, rw_a_up,
              rw_k_k, rw_k_a, rw_r_k, rw_ln_w, rw_ln_b, rw_w_out, w_o)
    y_prompt, y_sample = x_prompt, x_sample
    for layer in range(w_in.shape[0]):
        p = _prepare(*[w[layer] for w in params])
        y_prompt = _layer(y_prompt, p)
        y_sample = _layer(y_sample, p)
    return (y_prompt, y_sample)
```

```python
import functools
import math

import jax
import jax.numpy as jnp
from jax import lax
from jax.experimental import pallas as pl
from jax.experimental.pallas import tpu as pltpu

F32 = jnp.float32
BF16 = jnp.bfloat16

D_MODEL = 2048
D_S5 = 1024
S5_GROUP = 16
S5_GROUPS = 64
S5_STATE = 64
D_RWKV = 1024
RWKV_HEAD = 64
RWKV_HEADS = 16
LORA_RANK = 64
D_RW_IN = 3 * D_RWKV + 4 * LORA_RANK
RMS_EPS = 1e-6
GN_EPS = 64e-5
L2_EPS = 1e-12

S5_CHUNK = 16
S5_OCT_GROUPS = 8
S5_OCT = S5_OCT_GROUPS * S5_CHUNK * S5_GROUP
WKV_CHUNK = 64
HEAD_PAIR = 2 * RWKV_HEAD
NEUMANN_STEPS = 5
WKV_PAIRS_PER_STEP = 8
WKV_STREAMS_PER_BATCH = 2
WKV_SEQS = 2
HALO = 16

VMEM_LIMIT = 48 * 1024 * 1024


def _cparams(sem):
    return pltpu.CompilerParams(dimension_semantics=sem, vmem_limit_bytes=VMEM_LIMIT)


def _dot(a, b):
    return jnp.dot(a, b, preferred_element_type=F32)


def _dot_nt(a, b):
    return lax.dot_general(a, b, (((1,), (1,)), ((), ())), preferred_element_type=F32)


def _dot_tn(a, b):
    return lax.dot_general(a, b, (((0,), (0,)), ((), ())), preferred_element_type=F32)


def _sigmoid(x):
    return 1.0 / (1.0 + jnp.exp2(x * (-math.log2(math.e))))


def _silu(x):
    return x * _sigmoid(x)


def _gelu_tanh(x):
    c = math.sqrt(2.0 / math.pi)
    return 0.5 * x * (1.0 + jnp.tanh(c * (x + 0.044715 * (x * x * x))))


IN_TN = 1024
PROJ_RW_W = 4096
PROJ_W = PROJ_RW_W + D_S5 + D_RWKV + 2 * D_MODEL
PROJ_SG_BLK = PROJ_RW_W // D_S5
PROJ_RG_BLK = PROJ_SG_BLK + 1
PROJ_MG_BLK = (PROJ_RW_W + D_S5 + D_RWKV) // D_MODEL


def _rms_bf16(x, g):
    ms = jnp.mean(x * x, axis=-1, keepdims=True)
    return (x * lax.rsqrt(ms + RMS_EPS) * g).astype(BF16)


def _in_proj_u_kernel(x_ref, g_ref, w_ref, o_ref, ubuf_ref):
    res = _dot(_rms_bf16(x_ref[...], g_ref[...]), w_ref[...])
    rows = ubuf_ref.shape[1] // S5_CHUNK
    for o in range(ubuf_ref.shape[0]):
        ubuf_ref[o] = res[:, o * 128:(o + 1) * 128]
        for t in range(S5_CHUNK):
            tok = ubuf_ref[o, pl.ds(t, rows, stride=S5_CHUNK), :]
            lane = o * S5_OCT + t * 128
            o_ref[:, lane:lane + 128] = tok.astype(BF16)


def _in_proj_u(x, g, w_u, tm):
    n, d = x.shape
    return pl.pallas_call(
        _in_proj_u_kernel,
        out_shape=jax.ShapeDtypeStruct((n // S5_CHUNK, D_S5 * S5_CHUNK), BF16),
        grid=(n // tm,),
        in_specs=[pl.BlockSpec((tm, d), lambda i: (i, 0)),
                  pl.BlockSpec((1, d), lambda i: (0, 0)),
                  pl.BlockSpec((d, D_S5), lambda i: (0, 0))],
        out_specs=pl.BlockSpec((tm // S5_CHUNK, D_S5 * S5_CHUNK), lambda i: (i, 0)),
        scratch_shapes=[pltpu.VMEM((D_S5 // 128, tm, 128), F32)],
        compiler_params=_cparams(("parallel",)),
        name="in_proj_u",
    )(x, g, w_u)


def _in_proj_kernel(x_ref, g_ref, w_ref, o_ref, hn_ref):
    @pl.when(pl.program_id(1) == 0)
    def _():
        hn_ref[...] = _rms_bf16(x_ref[...], g_ref[...])

    o_ref[...] = _dot(hn_ref[...], w_ref[...]).astype(BF16)


def _in_proj(x, g, w, tm):
    n, d = x.shape
    return pl.pallas_call(
        _in_proj_kernel,
        out_shape=jax.ShapeDtypeStruct((n, w.shape[1]), BF16),
        grid=(n // tm, w.shape[1] // IN_TN),
        in_specs=[pl.BlockSpec((tm, d), lambda i, j: (i, 0)),
                  pl.BlockSpec((1, d), lambda i, j: (0, 0)),
                  pl.BlockSpec((d, IN_TN), lambda i, j: (0, j))],
        out_specs=pl.BlockSpec((tm, IN_TN), lambda i, j: (i, j)),
        scratch_shapes=[pltpu.VMEM((tm, d), BF16)],
        compiler_params=_cparams(("parallel", "arbitrary")),
        name="in_proj",
    )(x, g, w)


def _s5_weights(lam_re, lam_im, log_dt, b_re, b_im, c_re, c_im, d):
    t = S5_CHUNK
    g, p, h = S5_GROUPS, S5_STATE, S5_GROUP
    hp = lax.Precision.HIGHEST
    lam = lax.complex(lam_re.astype(F32), lam_im.astype(F32))
    dt = jnp.exp(log_dt.astype(F32))[..., None]
    lam_dt = lam * dt
    lam_bar = jnp.exp(lam_dt)
    bbar = ((lam_bar - 1.0) / lam)[..., None] * lax.complex(b_re.astype(F32), b_im.astype(F32))
    c = lax.complex(c_re.astype(F32), c_im.astype(F32))
    steps = jnp.arange(t + 1, dtype=F32)
    pw = jnp.exp(lam_dt[None] * steps[:, None, None, None])

    kern = jnp.einsum('dghp,ldgp,dgpj->dglhj', c, pw[:t], bbar, precision=hp).real
    tt = jnp.arange(t)
    lag = tt[None, :] - tt[:, None]
    kf = kern[0][:, jnp.abs(lag)]
    kb = kern[1][:, jnp.abs(lag)]
    k0 = kern[0][:, 0] + kern[1][:, 0] + jnp.eye(h, dtype=F32) * d.astype(F32).reshape(g, h)[:, :, None]
    lag5 = lag[None, :, :, None, None]
    m = jnp.where(lag5 > 0, kf, jnp.where(lag5 < 0, kb, k0[:, None, None]))
    m = m.transpose(0, 1, 4, 2, 3).reshape(g, t * h, t * h)

    pf = pw[t - 1 - tt, 0][..., None] * bbar[0][None]
    pb = pw[tt, 1][..., None] * bbar[1][None]

    def _p_mat(x):
        return x.transpose(1, 0, 3, 2).reshape(g, t * h, p)

    p_parts = [_p_mat(pf.real), _p_mat(pf.imag), _p_mat(pb.real), _p_mat(pb.imag)]

    cf = c[0][None] * pw[tt + 1, 0][:, :, None, :]
    cb = c[1][None] * pw[t - tt, 1][:, :, None, :]

    def _q_mat(x):
        return x.transpose(1, 3, 0, 2).reshape(g, p, t * h)

    q_parts = [_q_mat(cf.real), _q_mat(-cf.imag), _q_mat(cb.real), _q_mat(-cb.imag)]

    k8 = S5_OCT_GROUPS
    no = g // k8
    gi = jnp.arange(k8)

    def _spread(width):
        c = jnp.arange(width)
        tgt = (gi[:, None] * width + c[None, :])[:, :, None]
        return (tgt == jnp.arange(k8 * width)[None, None, :]).astype(BF16)

    def _spread_tok():
        u, hh = jnp.arange(t * h) // h, jnp.arange(t * h) % h
        tgt = (u[None, :] * (k8 * h) + gi[:, None] * h + hh[None, :])[:, :, None]
        return (tgt == jnp.arange(S5_OCT)[None, None, :]).astype(BF16)

    def _cols(x, spread):
        x = x.astype(BF16).reshape(no, k8, x.shape[1], x.shape[2])
        return jnp.einsum('ogrc,gcd->ogrd', x, spread, preferred_element_type=BF16)

    def _rows_tok(x):
        c = x.shape[-1]
        return x.reshape(no, k8, t, h, c).transpose(0, 2, 1, 3, 4).reshape(no, S5_OCT, c)

    sp_tok, sp_state = _spread_tok(), _spread(p)
    m_oct = _rows_tok(_cols(m, sp_tok))
    w_p = jnp.concatenate([_rows_tok(_cols(x, sp_state)) for x in p_parts], axis=2)
    qmat = jnp.concatenate([_cols(x, sp_tok).reshape(no, k8 * p, S5_OCT) for x in q_parts],
                           axis=1)
    at = pw[t]
    a_rows = jnp.stack([at[0].real, at[0].imag, at[1].real, at[1].imag]).reshape(4, g * p)
    return m_oct, w_p, qmat, a_rows


def _s5_mm_kernel(x_ref, w_ref, *o_refs):
    res = _dot(x_ref[...], w_ref[0])
    n = res.shape[1] // len(o_refs)
    for k, ref in enumerate(o_refs):
        ref[...] = res[:, n * k:n * (k + 1)].astype(ref.dtype)


def _s5_mm(x2, w, rb, nout, name):
    r = x2.shape[0]
    no, _, c = w.shape
    cw = c // nout
    return pl.pallas_call(
        _s5_mm_kernel,
        out_shape=tuple(jax.ShapeDtypeStruct((r, no * cw), F32) for _ in range(nout)),
        grid=(no, r // rb),
        in_specs=[pl.BlockSpec((rb, S5_OCT), lambda o, i: (i, o)),
                  pl.BlockSpec((1, S5_OCT, c), lambda o, i: (o, 0, 0))],
        out_specs=tuple(pl.BlockSpec((rb, cw), lambda o, i: (i, o)) for _ in range(nout)),
        compiler_params=_cparams(("parallel", "arbitrary")),
        name=name,
    )(x2, w)


def _s5_scan_kernel(a_ref, pfr_ref, pfi_ref, pbr_ref, pbi_ref,
                    hfr_ref, hfi_ref, hbr_ref, hbi_ref, carry_ref, *, rows):
    @pl.when(pl.program_id(2) == 0)
    def _():
        carry_ref[...] = jnp.zeros_like(carry_ref)

    afr, afi = a_ref[0:1, :], a_ref[1:2, :]
    abr, abi = a_ref[2:3, :], a_ref[3:4, :]

    def tile(i, carry):
        hfr, hfi, hbr, hbi = carry
        r0 = pl.multiple_of(i * 8, 8)
        pfr, pfi = pfr_ref[pl.ds(r0, 8), :], pfi_ref[pl.ds(r0, 8), :]
        rb0 = pl.multiple_of(rows - 8 - i * 8, 8)
        pbr, pbi = pbr_ref[pl.ds(rb0, 8), :], pbi_ref[pl.ds(rb0, 8), :]
        of_r, of_i, ob_r, ob_i = [], [], [None] * 8, [None] * 8
        for s in range(8):
            of_r.append(hfr)
            of_i.append(hfi)
            hfr, hfi = (afr * hfr - afi * hfi + pfr[s:s + 1, :],
                        afr * hfi + afi * hfr + pfi[s:s + 1, :])
            sb = 7 - s
            ob_r[sb] = hbr
            ob_i[sb] = hbi
            hbr, hbi = (abr * hbr - abi * hbi + pbr[sb:sb + 1, :],
                        abr * hbi + abi * hbr + pbi[sb:sb + 1, :])
        hfr_ref[pl.ds(r0, 8), :] = jnp.concatenate(of_r, axis=0)
        hfi_ref[pl.ds(r0, 8), :] = jnp.concatenate(of_i, axis=0)
        hbr_ref[pl.ds(rb0, 8), :] = jnp.concatenate(ob_r, axis=0)
        hbi_ref[pl.ds(rb0, 8), :] = jnp.concatenate(ob_i, axis=0)
        return hfr, hfi, hbr, hbi

    init = tuple(carry_ref[k:k + 1, :] for k in range(4))
    out = lax.fori_loop(0, rows // 8, tile, init)
    for k in range(4):
        carry_ref[k:k + 1, :] = out[k]


def _s5_scan(a_rows, pfr, pfi, pbr, pbi, nseq, cb, lt):
    r, w = pfr.shape
    nblk = r // nseq // cb
    fspec = pl.BlockSpec((cb, lt), lambda b, j, i: (b * nblk + i, j))
    bspec = pl.BlockSpec((cb, lt), lambda b, j, i: (b * nblk + nblk - 1 - i, j))
    st = jax.ShapeDtypeStruct((r, w), F32)
    return pl.pallas_call(
        functools.partial(_s5_scan_kernel, rows=cb),
        out_shape=(st, st, st, st),
        grid=(nseq, w // lt, nblk),
        in_specs=[pl.BlockSpec((4, lt), lambda b, j, i: (0, j)), fspec, fspec, bspec, bspec],
        out_specs=(fspec, fspec, bspec, bspec),
        scratch_shapes=[pltpu.VMEM((4, lt), F32)],
        compiler_params=_cparams(("parallel", "parallel", "arbitrary")),
        name="s5_scan",
    )(a_rows, pfr, pfi, pbr, pbi)


def _s5_state_out_kernel(y0_ref, hfr_ref, hfi_ref, hbr_ref, hbi_ref, q_ref, y_ref):
    h = jnp.concatenate([hfr_ref[...], hfi_ref[...], hbr_ref[...], hbi_ref[...]], axis=1)
    y_ref[...] = (y0_ref[...] + _dot(h.astype(BF16), q_ref[0])).astype(BF16)


def _s5_state_out(y0, hfr, hfi, hbr, hbi, qmat, rb):
    r = y0.shape[0]
    no, c, _ = qmat.shape
    hspec = pl.BlockSpec((rb, c // 4), lambda o, i: (i, o))
    yspec = pl.BlockSpec((rb, S5_OCT), lambda o, i: (i, o))
    return pl.pallas_call(
        _s5_state_out_kernel,
        out_shape=jax.ShapeDtypeStruct(y0.shape, BF16),
        grid=(no, r // rb),
        in_specs=[yspec, hspec, hspec, hspec, hspec,
                  pl.BlockSpec((1, c, S5_OCT), lambda o, i: (o, 0, 0))],
        out_specs=yspec,
        compiler_params=_cparams(("parallel", "arbitrary")),
        name="s5_state_out",
    )(y0, hfr, hfi, hbr, hbi, qmat)


def _head_indicator():
    lane = lax.broadcasted_iota(jnp.int32, (D_RWKV, 128), 0) // RWKV_HEAD
    col = lax.broadcasted_iota(jnp.int32, (D_RWKV, 128), 1)
    return (lane == col).astype(BF16)


def _head_sum_bcast(x, e, et):
    return _dot(_dot(x.astype(BF16), e).astype(BF16), et)


def _rw_pre_kernel(x_ref, prev_ref, next_ref, mu_ref, wup_ref, aup_ref, w0_ref, a0_ref,
                   kk_ref_p, ka_ref_p, rk_ref_p, e_ref, et_ref,
                   r_ref, kk_ref, v_ref, kdf_ref, kdb_ref, bf_ref, bb_ref, lwf_ref, lwb_ref,
                   bonus_ref, *, tm, seq):
    i = pl.program_id(0)
    xb = x_ref[...]
    first = (i * tm) % seq == 0
    last = ((i + 1) * tm) % seq == 0
    zero = jnp.zeros((HALO, xb.shape[1]), BF16)
    x_ext = jnp.concatenate([jnp.where(first, zero, prev_ref[...]), xb,
                             jnp.where(last, zero, next_ref[...])], axis=0)
    ri = lax.broadcasted_iota(jnp.int32, (tm, tm + 2 * HALO), 0) + HALO
    ci = lax.broadcasted_iota(jnp.int32, (tm, tm + 2 * HALO), 1)
    band = ((ci == ri - 1) | (ci == ri + 1)).astype(BF16)
    x = xb.astype(F32)
    z = x + (0.5 * _dot(band, x_ext) - x) * mu_ref[...]

    r = z[:, :D_RWKV]
    k = z[:, D_RWKV:2 * D_RWKV]
    v = z[:, 2 * D_RWKV:3 * D_RWKV]
    xw = z[:, 3 * D_RWKV:3 * D_RWKV + 2 * LORA_RANK]
    xa = z[:, 3 * D_RWKV + 2 * LORA_RANK:]
    e, et = e_ref[...], et_ref[...]

    kk = k * kk_ref_p[...]
    kk = kk * lax.rsqrt(_head_sum_bcast(kk * kk, e, et) + L2_EPS)
    bonus_ref[...] = (_head_sum_bcast(r * k * rk_ref_p[...], e, et) * v).astype(BF16)
    r_ref[...] = r.astype(BF16)
    kk_ref[...] = kk.astype(BF16)
    v_ref[...] = v.astype(BF16)

    wpre = _dot(jnp.tanh(xw).astype(BF16), wup_ref[...]) + w0_ref[...]
    apre = _dot(xa.astype(BF16), aup_ref[...]) + a0_ref[...]
    lw = -math.exp(-0.5) * _sigmoid(wpre)
    a = _sigmoid(apre)
    ka = ka_ref_p[...]
    lwf_ref[...] = lw[:, :D_RWKV]
    lwb_ref[...] = lw[:, D_RWKV:]
    af, ab = a[:, :D_RWKV], a[:, D_RWKV:]
    kdf_ref[...] = (k * (1.0 + (af - 1.0) * ka)).astype(BF16)
    kdb_ref[...] = (k * (1.0 + (ab - 1.0) * ka)).astype(BF16)
    bf_ref[...] = (kk * af).astype(BF16)
    bb_ref[...] = (kk * ab).astype(BF16)


def _rw_pre(rw_in, mu, wup, aup, w0, a0, k_k, k_a, r_k, e, et, tm, seq):
    n = rw_in.shape[0]
    w = D_RW_IN
    nbh = n // HALO
    th = tm // HALO
    row = lambda i: (i, 0)
    const = lambda i: (0, 0)
    ospec = pl.BlockSpec((tm, D_RWKV), row)
    sb = jax.ShapeDtypeStruct((n, D_RWKV), BF16)
    sf = jax.ShapeDtypeStruct((n, D_RWKV), F32)
    return pl.pallas_call(
        functools.partial(_rw_pre_kernel, tm=tm, seq=seq),
        out_shape=(sb,) * 7 + (sf, sf, sb),
        grid=(n // tm,),
        in_specs=[pl.BlockSpec((tm, w), row),
                  pl.BlockSpec((HALO, w), lambda i: (jnp.maximum(i * th - 1, 0), 0)),
                  pl.BlockSpec((HALO, w), lambda i: (jnp.minimum((i + 1) * th, nbh - 1), 0)),
                  pl.BlockSpec((1, w), const),
                  pl.BlockSpec((2 * LORA_RANK, 2 * D_RWKV), const),
                  pl.BlockSpec((2 * LORA_RANK, 2 * D_RWKV), const),
                  pl.BlockSpec((1, 2 * D_RWKV), const),
                  pl.BlockSpec((1, 2 * D_RWKV), const),
                  pl.BlockSpec((1, D_RWKV), const),
                  pl.BlockSpec((1, D_RWKV), const),
                  pl.BlockSpec((1, D_RWKV), const),
                  pl.BlockSpec((D_RWKV, 128), const),
                  pl.BlockSpec((128, D_RWKV), const)],
        out_specs=(ospec,) * 10,
        compiler_params=_cparams(("parallel",)),
        name="rw_pre",
    )(rw_in, rw_in, rw_in, mu, wup, aup, w0, a0, k_k, k_a, r_k, e, et)


def _wkv_kernel(rf_ref, kkf_ref, vf_ref, kdf_ref, bf_ref, lwf_ref,
                rb_ref, kkb_ref, vb_ref, kdb_ref, bb_ref, lwb_ref, yf_ref, yb_ref, s_ref):
    t = WKV_CHUNK
    hd = RWKV_HEAD
    w = HEAD_PAIR

    @pl.when(pl.program_id(1) == 0)
    def _():
        s_ref[...] = jnp.zeros_like(s_ref)

    def iota(shape, dim):
        return lax.broadcasted_iota(jnp.int32, shape, dim)

    rr, cc = iota((t, t), 0), iota((t, t), 1)
    rw_, lane = iota((t, w), 0), iota((t, w), 1)
    col = lane % hd
    m0 = lane < hd
    eye_f = (rw_ == col).astype(F32)
    m0_2t = iota((2 * t, w), 1) < hd
    eye_w = iota((w, w), 0) == iota((w, w), 1)
    bd_mask = (iota((w, w), 0) < hd) == (iota((w, w), 1) < hd)

    def direction(rev, refs, y_ref):
        if rev:
            incl, strict, tri = rw_ <= col, rw_ < col, rr <= cc
        else:
            incl, strict, tri = rw_ >= col, rw_ > col, rr >= cc
        return dict(rev=int(rev), refs=refs, y_ref=y_ref, incl=incl, strict=strict,
                    tri=tri.astype(BF16),
                    mask_2t=jnp.concatenate([strict, incl], axis=0), t_last=0 if rev else t - 1)

    dirs = (direction(False, (rf_ref, kkf_ref, vf_ref, kdf_ref, bf_ref, lwf_ref), yf_ref),
            direction(True, (rb_ref, kkb_ref, vb_ref, kdb_ref, bb_ref, lwb_ref), yb_ref))

    def bdiag(x):
        return jnp.concatenate([jnp.where(m0, x, 0.0), jnp.where(m0, 0.0, x)], axis=0)

    def group(gi, carry):
        streams = [(dm, sq) for dm in dirs for sq in range(rf_ref.shape[0])]
        for k in range(0, len(streams), WKV_STREAMS_PER_BATCH):
            batch(streams[k:k + WKV_STREAMS_PER_BATCH], gi)
        return carry

    def batch(streams, gi):
        hv = []
        for dm, sq, j in [(dm, sq, j) for dm, sq in streams for j in range(WKV_PAIRS_PER_STEP)]:
            hp = gi * WKV_PAIRS_PER_STEP + j
            sl = pl.ds(pl.multiple_of(hp * HEAD_PAIR, HEAD_PAIR), HEAD_PAIR)
            r, kk, v, kd, beta = (ref[sq, :, sl].astype(F32) for ref in dm['refs'][:5])
            lw = dm['refs'][5][sq, :, sl]
            t_last = dm['t_last']
            lcum = lw
            for s in (1, 2, 4, 8, 16, 32):
                if dm['rev']:
                    lcum = lcum + jnp.where(rw_ < t - s, pltpu.roll(lcum, t - s, 0), 0.0)
                else:
                    lcum = lcum + jnp.where(rw_ >= s, pltpu.roll(lcum, s, 0), 0.0)
            lcum_x = lcum - lw
            cref = lcum[t // 2:t // 2 + 1, :]
            ltot = lcum[t_last:t_last + 1, :]
            e1 = jnp.exp(lcum - cref)
            e1x = jnp.exp(lcum_x - cref)
            e2 = jnp.exp(cref - lcum)
            ec = jnp.exp(cref)
            ewt = jnp.exp(ltot - cref)
            wtot = jnp.exp(ltot)
            r_t = r * e1
            a_t = -kk * e1x
            b_t = beta * e2
            k_t = kd * e2
            a_0 = a_t * ec
            r_0 = r_t * ec
            b_h = b_t * ewt
            k_h = k_t * ewt
            hv.append(dict(
                dm=dm, sq=sq, hp=hp, sl=sl, wtot=wtot, a_0=a_0, r_0=r_0,
                lm=jnp.concatenate([a_t, r_t], axis=0),
                rm=jnp.concatenate([b_t, k_t], axis=0).astype(BF16),
                bk=jnp.concatenate([b_h, k_h], axis=0).astype(BF16), v=v))
        for h in hv:
            lm2 = jnp.concatenate([jnp.where(m0_2t, h['lm'], 0.0), jnp.where(m0_2t, 0.0, h['lm'])], axis=0)
            g = _dot_nt(lm2.astype(BF16), h['rm'])
            g0, g1 = g[:2 * t], pltpu.roll(g[2 * t:], hd, 1)
            own = jnp.where(m0_2t, g0, g1)
            oth = jnp.where(m0_2t, g1, g0)
            h['a_ab'] = jnp.where(h['dm']['strict'], own[:t], 0.0)
            h['a_rb'] = jnp.where(h['dm']['incl'], own[t:], 0.0).astype(BF16)
            h['a_k'] = jnp.where(h['dm']['mask_2t'], oth, 0.0).astype(BF16)
        for h in hv:
            v = h['v']
            h['inv'] = eye_f + h['a_ab']
            h['nk'] = _dot(h['a_ab'].astype(BF16), bdiag(h['a_ab']).astype(BF16))
            v_rows = jnp.concatenate([jnp.where(m0, 0.0, v), jnp.where(m0, v, 0.0)], axis=0)
            h['av'] = _dot(h['a_k'], v_rows.astype(BF16))
        for s in range(1, NEUMANN_STEPS + 1):
            for h in hv:
                nk_bd = bdiag(h['nk']).astype(BF16)
                if s < NEUMANN_STEPS:
                    nx = _dot(jnp.concatenate([h['nk'], h['inv']], axis=0).astype(BF16), nk_bd)
                    h['nk'] = nx[:t]
                    h['inv'] = h['inv'] + nx[t:]
                else:
                    h['inv'] = h['inv'] + _dot(h['inv'].astype(BF16), nk_bd)
        for h in hv:
            rhs = jnp.concatenate([bdiag(h['a_0']), bdiag(h['av'][:t])], axis=1)
            h['pp'] = _dot(h['inv'].astype(BF16), rhs.astype(BF16))
        for h in hv:
            pp = h['pp']
            rhs = jnp.concatenate([bdiag(pp[:, :w]), bdiag(pp[:, w:])], axis=1)
            h['qq'] = (_dot(h['a_rb'], rhs.astype(BF16))
                       + jnp.concatenate([h['r_0'], h['av'][t:]], axis=1))
            pv = jnp.concatenate([pp, jnp.concatenate([jnp.zeros_like(h['v']), h['v']], axis=1)], axis=0)
            h['mn'] = _dot_tn(h['bk'], pv.astype(BF16))
        for h in hv:
            sidx = (h['dm']['rev'], h['sq'], h['hp'])
            st = s_ref[sidx]
            lhs = jnp.concatenate([h['qq'][:, :w], jnp.where(bd_mask, h['mn'][:, :w], 0.0)], axis=0)
            res = _dot(lhs.astype(BF16), st.astype(BF16))
            h['dm']['y_ref'][h['sq'], :, h['sl']] = res[:t] + h['qq'][:, w:]
            wcol = jnp.sum(jnp.where(eye_w, jnp.broadcast_to(h['wtot'], (w, w)), 0.0),
                           axis=1, keepdims=True)
            s_ref[sidx] = wcol * st + res[t:] + jnp.where(bd_mask, h['mn'][:, w:], 0.0)

    lax.fori_loop(0, RWKV_HEADS // (2 * WKV_PAIRS_PER_STEP), group, 0)


def _wkv(r, kk, v, kdf, bf, lwf, kdb, bb, lwb, nseq):
    n = r.shape[0]
    t = WKV_CHUNK
    seq = n // nseq
    nc = seq // t
    ns = WKV_SEQS if nseq % WKV_SEQS == 0 else 1
    fwd = pl.BlockSpec((ns, t, D_RWKV), lambda b, c: (b, c, 0))
    bwd = pl.BlockSpec((ns, t, D_RWKV), lambda b, c: (b, nc - 1 - c, 0))
    shape = jax.ShapeDtypeStruct((nseq, seq, D_RWKV), F32)
    yf, yb = pl.pallas_call(
        _wkv_kernel,
        out_shape=(shape, shape),
        grid=(nseq // ns, nc),
        in_specs=[fwd] * 6 + [bwd] * 6,
        out_specs=(fwd, bwd),
        scratch_shapes=[pltpu.VMEM((2, ns, RWKV_HEADS // 2, HEAD_PAIR, HEAD_PAIR), F32)],
        compiler_params=_cparams(("parallel", "arbitrary")),
        name="wkv",
    )(*[a.reshape(nseq, seq, D_RWKV) for a in (r, kk, v, kdf, bf, lwf, r, kk, v, kdb, bb, lwb)])
    return yf.reshape(n, D_RWKV), yb.reshape(n, D_RWKV)


def _tail_kernel(ys_ref, sg_ref, ms_ref, yf_ref, yb_ref, bonus_ref, rg_ref, mr_ref, x_ref,
                 wglu_ref, swout_ref, lnw_ref, lnb_ref, e_ref, et_ref, rwout_ref, wo_ref, g_ref,
                 o_ref, ytok_ref):
    f32 = lambda ref: ref[...].astype(F32)
    rows = ys_ref.shape[0]
    octs = S5_GROUPS // S5_OCT_GROUPS
    for o in range(octs):
        for t in range(S5_CHUNK):
            lane = o * S5_OCT + t * 128
            ytok_ref[o, pl.ds(t, rows, stride=S5_CHUNK), :] = ys_ref[:, lane:lane + 128].astype(F32)
    z = _gelu_tanh(jnp.concatenate([ytok_ref[o] for o in range(octs)], axis=1))
    z = z * _sigmoid(_dot(z.astype(BF16), wglu_ref[...]))
    z = z * _silu(f32(sg_ref))
    h = _sigmoid(f32(ms_ref)) * _dot(z.astype(BF16), swout_ref[...])

    e, et = e_ref[...], et_ref[...]
    y = yf_ref[...] + yb_ref[...]
    inv_n = 1.0 / RWKV_HEAD
    mean = _head_sum_bcast(y, e, et) * inv_n
    yc = y - mean
    var = _head_sum_bcast(yc * yc, e, et) * inv_n
    y = yc * lax.rsqrt(var + GN_EPS) * lnw_ref[...] + lnb_ref[...] + f32(bonus_ref)
    y = y * _silu(f32(rg_ref))
    h = h + _sigmoid(f32(mr_ref)) * _dot(y.astype(BF16), rwout_ref[...])

    out = _dot(h.astype(BF16), wo_ref[...])
    ms = jnp.mean(out * out, axis=-1, keepdims=True)
    o_ref[...] = x_ref[...] + out * lax.rsqrt(ms + RMS_EPS) * g_ref[...]


def _tail(y_s5, proj, yf, yb, bonus, x, p, tm):
    n = x.shape[0]
    row = lambda i: (i, 0)
    const = lambda i: (0, 0)
    half = pl.BlockSpec((tm, D_RWKV), row)
    full = pl.BlockSpec((tm, D_MODEL), row)

    def proj_cols(width, blk):
        return pl.BlockSpec((tm, width), lambda i: (i, blk))

    def resident(shape):
        return pl.BlockSpec(shape, const, pipeline_mode=pl.Buffered(1))

    return pl.pallas_call(
        _tail_kernel,
        out_shape=jax.ShapeDtypeStruct((n, D_MODEL), F32),
        grid=(n // tm,),
        in_specs=[pl.BlockSpec((tm // S5_CHUNK, S5_CHUNK * D_S5), row),
                  proj_cols(D_S5, PROJ_SG_BLK), proj_cols(D_MODEL, PROJ_MG_BLK),
                  half, half, half,
                  proj_cols(D_RWKV, PROJ_RG_BLK), proj_cols(D_MODEL, PROJ_MG_BLK + 1),
                  full,
                  resident((D_S5, D_S5)), resident((D_S5, D_MODEL)),
                  resident((1, D_RWKV)), resident((1, D_RWKV)),
                  resident((D_RWKV, 128)), resident((128, D_RWKV)),
                  resident((D_RWKV, D_MODEL)), resident((D_MODEL, D_MODEL)),
                  resident((1, D_MODEL))],
        out_specs=full,
        scratch_shapes=[pltpu.VMEM((D_S5 // 128, tm, 128), F32)],
        compiler_params=_cparams(("parallel",)),
        name="tail",
    )(y_s5, proj, proj, yf, yb, bonus, proj, proj, x,
      p['w_glu'], p['s5_w_out'], p['ln_w'], p['ln_b'], p['e'], p['et'], p['rw_w_out'], p['w_o'],
      p['post_g'])


def _pick(n, pref):
    while n % pref:
        pref //= 2
    return pref


def _prepare(pre_norm_g, post_norm_g, w_in, s5_lam_re, s5_lam_im, s5_log_dt, s5_b_re, s5_b_im,
             s5_c_re, s5_c_im, s5_d, s5_w_glu, s5_w_out, rw_mu, rw_w0, rw_w_up, rw_a0, rw_a_up,
             rw_k_k, rw_k_a, rw_r_k, rw_ln_w, rw_ln_b, rw_w_out, w_o):
    p = {}
    p['pre_g'] = pre_norm_g.astype(F32).reshape(1, D_MODEL)
    p['post_g'] = post_norm_g.astype(F32).reshape(1, D_MODEL)
    w = w_in.astype(BF16)
    rw_end = 2 * D_S5 + D_RW_IN
    p['w_u'] = w[:, :D_S5]
    p['w_proj'] = jnp.concatenate(
        [w[:, 2 * D_S5:rw_end], jnp.zeros((D_MODEL, PROJ_RW_W - D_RW_IN), BF16),
         w[:, D_S5:2 * D_S5], w[:, rw_end:]], axis=1)
    p['s5_m'], p['s5_p'], p['s5_q'], p['s5_a'] = _s5_weights(
        s5_lam_re, s5_lam_im, s5_log_dt, s5_b_re, s5_b_im, s5_c_re, s5_c_im, s5_d)
    p['w_glu'] = s5_w_glu.astype(BF16)
    p['s5_w_out'] = s5_w_out.astype(BF16)
    p['mu'] = rw_mu.astype(F32).reshape(1, D_RW_IN)
    zeros = jnp.zeros((LORA_RANK, D_RWKV), F32)

    def blockdiag(u):
        return jnp.concatenate([jnp.concatenate([u[0], zeros], axis=1),
                                jnp.concatenate([zeros, u[1]], axis=1)], axis=0).astype(BF16)

    p['wup'] = blockdiag(rw_w_up.astype(F32))
    p['aup'] = blockdiag(rw_a_up.astype(F32))
    p['w0'] = rw_w0.astype(F32).reshape(1, 2 * D_RWKV)
    p['a0'] = rw_a0.astype(F32).reshape(1, 2 * D_RWKV)
    p['k_k'] = rw_k_k.astype(F32).reshape(1, D_RWKV)
    p['k_a'] = rw_k_a.astype(F32).reshape(1, D_RWKV)
    p['r_k'] = rw_r_k.astype(F32).reshape(1, D_RWKV)
    p['ln_w'] = rw_ln_w.astype(F32).reshape(1, D_RWKV)
    p['ln_b'] = rw_ln_b.astype(F32).reshape(1, D_RWKV)
    p['rw_w_out'] = rw_w_out.astype(BF16)
    p['w_o'] = w_o.astype(BF16)
    e = _head_indicator()
    p['e'] = e
    p['et'] = e.T
    return p


def _s5_mix(x2, p, nseq):
    r = x2.shape[0]
    rb = _pick(r, 256)
    (y0,) = _s5_mm(x2, p['s5_m'], rb, 1, "s5_chunk_out")
    pfr, pfi, pbr, pbi = _s5_mm(x2, p['s5_p'], rb, 4, "s5_chunk_state")
    cb = _pick(r // nseq, 128)
    hfr, hfi, hbr, hbi = _s5_scan(p['s5_a'], pfr, pfi, pbr, pbi, nseq, cb, 1024)
    return _s5_state_out(y0, hfr, hfi, hbr, hbi, p['s5_q'], rb)


def _layer(x, p):
    bsz, seq, _ = x.shape
    n = bsz * seq
    x2 = x.reshape(n, D_MODEL)
    u_rows = _in_proj_u(x2, p['pre_g'], p['w_u'], _pick(n, 512))
    proj = _in_proj(x2, p['pre_g'], p['w_proj'], _pick(n, 1024))

    y_s5 = _s5_mix(u_rows, p, bsz)

    (r, kk, v, kdf, kdb, bf, bb, lwf, lwb, bonus) = _rw_pre(
        proj, p['mu'], p['wup'], p['aup'], p['w0'], p['a0'], p['k_k'], p['k_a'], p['r_k'],
        p['e'], p['et'], _pick(seq, 256), seq)
    yf, yb = _wkv(r, kk, v, kdf, bf, lwf, kdb, bb, lwb, bsz)

    out = _tail(y_s5, proj, yf, yb, bonus, x2, p, _pick(n, 256))
    return out.reshape(bsz, seq, D_MODEL)


def kernel(x_prompt, x_sample, pre_norm_g, post_norm_g, w_in, s5_lam_re, s5_lam_im, s5_log_dt, s5_b_re, s5_b_im, s5_c_re, s5_c_im, s5_d, s5_w_glu, s5_w_out, rw_mu, rw_w0, rw_w_up, rw_a0, rw_a_up, rw_k_k, rw_k_a, rw_r_k, rw_ln_w, rw_ln_b, rw_w_out, w_o):
    params = (pre_norm_g, post_norm_g, w_in, s5_lam_re, s5_lam_im, s5_log_dt, s5_b_re, s5_b_im,
              s5_c_re, s5_c_im, s5_d, s5_w_glu, s5_w_out, rw_mu, rw_w0, rw_w_up, rw_a0, rw_a_up,
              rw_k_k, rw_k_a, rw_r_k, rw_ln_w, rw_ln_b, rw_w_out, w_o)
    y_prompt, y_sample = x_prompt, x_sample
    for layer in range(w_in.shape[0]):
        p = _prepare(*[w[layer] for w in params])
        y_prompt = _layer(y_prompt, p)
        y_sample = _layer(y_sample, p)
    return (y_prompt, y_sample)
```

```python
import functools
import math

import jax
import jax.numpy as jnp
from jax import lax
from jax.experimental import pallas as pl
from jax.experimental.pallas import tpu as pltpu

F32 = jnp.float32
BF16 = jnp.bfloat16

D_MODEL = 2048
D_S5 = 1024
S5_GROUP = 16
S5_GROUPS = 64
S5_STATE = 64
D_RWKV = 1024
RWKV_HEAD = 64
RWKV_HEADS = 16
LORA_RANK = 64
D_RW_IN = 3 * D_RWKV + 4 * LORA_RANK
RMS_EPS = 1e-6
GN_EPS = 64e-5
L2_EPS = 1e-12

S5_CHUNK = 16
S5_OCT_GROUPS = 4
S5_OCT = S5_OCT_GROUPS * S5_CHUNK * S5_GROUP
S5_HALF = S5_OCT_GROUPS * S5_GROUP
WKV_CHUNK = 64
HEAD_PAIR = 2 * RWKV_HEAD
NEUMANN_STEPS = 5
WKV_PAIRS_PER_STEP = 8
WKV_STREAMS_PER_BATCH = 2
WKV_SEQS = 2
HALO = 16

VMEM_LIMIT = 48 * 1024 * 1024


def _cparams(sem):
    return pltpu.CompilerParams(dimension_semantics=sem, vmem_limit_bytes=VMEM_LIMIT)


def _dot(a, b):
    return jnp.dot(a, b, preferred_element_type=F32)


def _dot_nt(a, b):
    return lax.dot_general(a, b, (((1,), (1,)), ((), ())), preferred_element_type=F32)


def _dot_tn(a, b):
    return lax.dot_general(a, b, (((0,), (0,)), ((), ())), preferred_element_type=F32)


def _sigmoid(x):
    return 1.0 / (1.0 + jnp.exp2(x * (-math.log2(math.e))))


def _silu(x):
    return x * _sigmoid(x)


def _gelu_tanh(x):
    c = math.sqrt(2.0 / math.pi)
    return 0.5 * x * (1.0 + jnp.tanh(c * (x + 0.044715 * (x * x * x))))


IN_TN = 1024
PROJ_RW_W = 4096
PROJ_W = PROJ_RW_W + D_S5 + D_RWKV + 2 * D_MODEL
PROJ_SG_BLK = PROJ_RW_W // D_S5
PROJ_RG_BLK = PROJ_SG_BLK + 1
PROJ_MG_BLK = (PROJ_RW_W + D_S5 + D_RWKV) // D_MODEL


def _rms_bf16(x, g):
    ms = jnp.mean(x * x, axis=-1, keepdims=True)
    return (x * lax.rsqrt(ms + RMS_EPS) * g).astype(BF16)


def _in_proj_u_kernel(x_ref, g_ref, w_ref, o_ref, ubuf_ref):
    res = _dot(_rms_bf16(x_ref[...], g_ref[...]), w_ref[...])
    rows = ubuf_ref.shape[1] // S5_CHUNK
    low = lax.broadcasted_iota(jnp.int32, (rows, 128), 1) < S5_HALF
    for o in range(ubuf_ref.shape[0]):
        ubuf_ref[o] = res[:, o * 128:(o + 1) * 128]
        for tp in range(S5_CHUNK // 2):
            te = ubuf_ref[o, pl.ds(2 * tp, rows, stride=S5_CHUNK), :]
            to = ubuf_ref[o, pl.ds(2 * tp + 1, rows, stride=S5_CHUNK), :]
            pack_a = jnp.where(low, te, pltpu.roll(to, S5_HALF, 1))
            pack_b = jnp.where(low, pltpu.roll(te, S5_HALF, 1), to)
            lane = 2 * o * S5_OCT + tp * 128
            o_ref[:, lane:lane + 128] = pack_a.astype(BF16)
            o_ref[:, lane + S5_OCT:lane + S5_OCT + 128] = pack_b.astype(BF16)


def _in_proj_u(x, g, w_u, tm):
    n, d = x.shape
    return pl.pallas_call(
        _in_proj_u_kernel,
        out_shape=jax.ShapeDtypeStruct((n // S5_CHUNK, D_S5 * S5_CHUNK), BF16),
        grid=(n // tm,),
        in_specs=[pl.BlockSpec((tm, d), lambda i: (i, 0)),
                  pl.BlockSpec((1, d), lambda i: (0, 0)),
                  pl.BlockSpec((d, D_S5), lambda i: (0, 0))],
        out_specs=pl.BlockSpec((tm // S5_CHUNK, D_S5 * S5_CHUNK), lambda i: (i, 0)),
        scratch_shapes=[pltpu.VMEM((D_S5 // 128, tm, 128), F32)],
        compiler_params=_cparams(("parallel",)),
        name="in_proj_u",
    )(x, g, w_u)


def _in_proj_kernel(x_ref, g_ref, w_ref, o_ref, hn_ref):
    @pl.when(pl.program_id(1) == 0)
    def _():
        hn_ref[...] = _rms_bf16(x_ref[...], g_ref[...])

    o_ref[...] = _dot(hn_ref[...], w_ref[...]).astype(BF16)


def _in_proj(x, g, w, tm):
    n, d = x.shape
    return pl.pallas_call(
        _in_proj_kernel,
        out_shape=jax.ShapeDtypeStruct((n, w.shape[1]), BF16),
        grid=(n // tm, w.shape[1] // IN_TN),
        in_specs=[pl.BlockSpec((tm, d), lambda i, j: (i, 0)),
                  pl.BlockSpec((1, d), lambda i, j: (0, 0)),
                  pl.BlockSpec((d, IN_TN), lambda i, j: (0, j))],
        out_specs=pl.BlockSpec((tm, IN_TN), lambda i, j: (i, j)),
        scratch_shapes=[pltpu.VMEM((tm, d), BF16)],
        compiler_params=_cparams(("parallel", "arbitrary")),
        name="in_proj",
    )(x, g, w)


def _s5_weights(lam_re, lam_im, log_dt, b_re, b_im, c_re, c_im, d):
    t = S5_CHUNK
    g, p, h = S5_GROUPS, S5_STATE, S5_GROUP
    hp = lax.Precision.HIGHEST
    lam = lax.complex(lam_re.astype(F32), lam_im.astype(F32))
    dt = jnp.exp(log_dt.astype(F32))[..., None]
    lam_dt = lam * dt
    lam_bar = jnp.exp(lam_dt)
    bbar = ((lam_bar - 1.0) / lam)[..., None] * lax.complex(b_re.astype(F32), b_im.astype(F32))
    c = lax.complex(c_re.astype(F32), c_im.astype(F32))
    steps = jnp.arange(t + 1, dtype=F32)
    pw = jnp.exp(lam_dt[None] * steps[:, None, None, None])

    kern = jnp.einsum('dghp,ldgp,dgpj->dglhj', c, pw[:t], bbar, precision=hp).real
    tt = jnp.arange(t)
    lag = tt[None, :] - tt[:, None]
    kf = kern[0][:, jnp.abs(lag)]
    kb = kern[1][:, jnp.abs(lag)]
    k0 = kern[0][:, 0] + kern[1][:, 0] + jnp.eye(h, dtype=F32) * d.astype(F32).reshape(g, h)[:, :, None]
    lag5 = lag[None, :, :, None, None]
    m = jnp.where(lag5 > 0, kf, jnp.where(lag5 < 0, kb, k0[:, None, None]))
    m = m.transpose(0, 1, 4, 2, 3).reshape(g, t * h, t * h)

    pf = pw[t - 1 - tt, 0][..., None] * bbar[0][None]
    pb = pw[tt, 1][..., None] * bbar[1][None]

    def _p_mat(x):
        return x.transpose(1, 0, 3, 2).reshape(g, t * h, p)

    p_parts = [_p_mat(pf.real), _p_mat(pf.imag), _p_mat(pb.real), _p_mat(pb.imag)]

    cf = c[0][None] * pw[tt + 1, 0][:, :, None, :]
    cb = c[1][None] * pw[t - tt, 1][:, :, None, :]

    def _q_mat(x):
        return x.transpose(1, 3, 0, 2).reshape(g, p, t * h)

    q_parts = [_q_mat(cf.real), _q_mat(-cf.imag), _q_mat(cb.real), _q_mat(-cb.imag)]

    k8 = S5_OCT_GROUPS
    no = g // k8
    gi = jnp.arange(k8)

    def _spread(width):
        c = jnp.arange(width)
        tgt = (gi[:, None] * width + c[None, :])[:, :, None]
        return (tgt == jnp.arange(k8 * width)[None, None, :]).astype(BF16)

    def _spread_tok():
        u, hh = jnp.arange(t * h) // h, jnp.arange(t * h) % h
        tgt = (u[None, :] * (k8 * h) + gi[:, None] * h + hh[None, :])[:, :, None]
        return (tgt == jnp.arange(S5_OCT)[None, None, :]).astype(BF16)

    def _cols(x, spread):
        x = x.astype(BF16).reshape(no, k8, x.shape[1], x.shape[2])
        return jnp.einsum('ogrc,gcd->ogrd', x, spread, preferred_element_type=BF16)

    def _rows_tok(x):
        c = x.shape[-1]
        return x.reshape(no, k8, t, h, c).transpose(0, 2, 1, 3, 4).reshape(no, S5_OCT, c)

    sp_tok, sp_state = _spread_tok(), _spread(p)
    m_oct = _rows_tok(_cols(m, sp_tok))
    w_p = jnp.concatenate([_rows_tok(_cols(x, sp_state)) for x in p_parts], axis=2)
    qmat = jnp.concatenate([_cols(x, sp_tok).reshape(no, k8 * p, S5_OCT) for x in q_parts],
                           axis=1)
    at = pw[t]
    a_rows = jnp.stack([at[0].real, at[0].imag, at[1].real, at[1].imag]).reshape(4, g * p)
    return m_oct, w_p, qmat, a_rows


def _s5_mm_kernel(x_ref, w_ref, *o_refs):
    res = _dot(x_ref[...], w_ref[0])
    n = res.shape[1] // len(o_refs)
    for k, ref in enumerate(o_refs):
        ref[...] = res[:, n * k:n * (k + 1)].astype(ref.dtype)


def _s5_mm(x2, w, rb, nout, name):
    r = x2.shape[0]
    no, _, c = w.shape
    cw = c // nout
    return pl.pallas_call(
        _s5_mm_kernel,
        out_shape=tuple(jax.ShapeDtypeStruct((r, no * cw), F32) for _ in range(nout)),
        grid=(no, r // rb),
        in_specs=[pl.BlockSpec((rb, S5_OCT), lambda o, i: (i, o)),
                  pl.BlockSpec((1, S5_OCT, c), lambda o, i: (o, 0, 0))],
        out_specs=tuple(pl.BlockSpec((rb, cw), lambda o, i: (i, o)) for _ in range(nout)),
        compiler_params=_cparams(("parallel", "arbitrary")),
        name=name,
    )(x2, w)


def _s5_scan_kernel(a_ref, pfr_ref, pfi_ref, pbr_ref, pbi_ref,
                    hfr_ref, hfi_ref, hbr_ref, hbi_ref, carry_ref, *, rows):
    @pl.when(pl.program_id(2) == 0)
    def _():
        carry_ref[...] = jnp.zeros_like(carry_ref)

    afr, afi = a_ref[0:1, :], a_ref[1:2, :]
    abr, abi = a_ref[2:3, :], a_ref[3:4, :]

    def tile(i, carry):
        hfr, hfi, hbr, hbi = carry
        r0 = pl.multiple_of(i * 8, 8)
        pfr, pfi = pfr_ref[pl.ds(r0, 8), :], pfi_ref[pl.ds(r0, 8), :]
        rb0 = pl.multiple_of(rows - 8 - i * 8, 8)
        pbr, pbi = pbr_ref[pl.ds(rb0, 8), :], pbi_ref[pl.ds(rb0, 8), :]
        of_r, of_i, ob_r, ob_i = [], [], [None] * 8, [None] * 8
        for s in range(8):
            of_r.append(hfr)
            of_i.append(hfi)
            hfr, hfi = (afr * hfr - afi * hfi + pfr[s:s + 1, :],
                        afr * hfi + afi * hfr + pfi[s:s + 1, :])
            sb = 7 - s
            ob_r[sb] = hbr
            ob_i[sb] = hbi
            hbr, hbi = (abr * hbr - abi * hbi + pbr[sb:sb + 1, :],
                        abr * hbi + abi * hbr + pbi[sb:sb + 1, :])
        hfr_ref[pl.ds(r0, 8), :] = jnp.concatenate(of_r, axis=0)
        hfi_ref[pl.ds(r0, 8), :] = jnp.concatenate(of_i, axis=0)
        hbr_ref[pl.ds(rb0, 8), :] = jnp.concatenate(ob_r, axis=0)
        hbi_ref[pl.ds(rb0, 8), :] = jnp.concatenate(ob_i, axis=0)
        return hfr, hfi, hbr, hbi

    init = tuple(carry_ref[k:k + 1, :] for k in range(4))
    out = lax.fori_loop(0, rows // 8, tile, init)
    for k in range(4):
        carry_ref[k:k + 1, :] = out[k]


def _s5_scan(a_rows, pfr, pfi, pbr, pbi, nseq, cb, lt):
    r, w = pfr.shape
    nblk = r // nseq // cb
    fspec = pl.BlockSpec((cb, lt), lambda b, j, i: (b * nblk + i, j))
    bspec = pl.BlockSpec((cb, lt), lambda b, j, i: (b * nblk + nblk - 1 - i, j))
    st = jax.ShapeDtypeStruct((r, w), F32)
    return pl.pallas_call(
        functools.partial(_s5_scan_kernel, rows=cb),
        out_shape=(st, st, st, st),
        grid=(nseq, w // lt, nblk),
        in_specs=[pl.BlockSpec((4, lt), lambda b, j, i: (0, j)), fspec, fspec, bspec, bspec],
        out_specs=(fspec, fspec, bspec, bspec),
        scratch_shapes=[pltpu.VMEM((4, lt), F32)],
        compiler_params=_cparams(("parallel", "parallel", "arbitrary")),
        name="s5_scan",
    )(a_rows, pfr, pfi, pbr, pbi)


def _s5_state_out_kernel(y0_ref, hfr_ref, hfi_ref, hbr_ref, hbi_ref, q_ref, y_ref):
    h = jnp.concatenate([hfr_ref[...], hfi_ref[...], hbr_ref[...], hbi_ref[...]], axis=1)
    y_ref[...] = (y0_ref[...] + _dot(h.astype(BF16), q_ref[0])).astype(BF16)


def _s5_state_out(y0, hfr, hfi, hbr, hbi, qmat, rb):
    r = y0.shape[0]
    no, c, _ = qmat.shape
    hspec = pl.BlockSpec((rb, c // 4), lambda o, i: (i, o))
    yspec = pl.BlockSpec((rb, S5_OCT), lambda o, i: (i, o))
    return pl.pallas_call(
        _s5_state_out_kernel,
        out_shape=jax.ShapeDtypeStruct(y0.shape, BF16),
        grid=(no, r // rb),
        in_specs=[yspec, hspec, hspec, hspec, hspec,
                  pl.BlockSpec((1, c, S5_OCT), lambda o, i: (o, 0, 0))],
        out_specs=yspec,
        compiler_params=_cparams(("parallel", "arbitrary")),
        name="s5_state_out",
    )(y0, hfr, hfi, hbr, hbi, qmat)


def _head_indicator():
    lane = lax.broadcasted_iota(jnp.int32, (D_RWKV, 128), 0) // RWKV_HEAD
    col = lax.broadcasted_iota(jnp.int32, (D_RWKV, 128), 1)
    return (lane == col).astype(BF16)


def _head_sum_bcast(x, e, et):
    return _dot(_dot(x.astype(BF16), e).astype(BF16), et)


def _rw_pre_kernel(x_ref, prev_ref, next_ref, mu_ref, wup_ref, aup_ref, w0_ref, a0_ref,
                   kk_ref_p, ka_ref_p, rk_ref_p, e_ref, et_ref,
                   r_ref, kk_ref, v_ref, kdf_ref, kdb_ref, bf_ref, bb_ref, lwf_ref, lwb_ref,
                   bonus_ref, *, tm, seq):
    i = pl.program_id(0)
    xb = x_ref[...]
    first = (i * tm) % seq == 0
    last = ((i + 1) * tm) % seq == 0
    zero = jnp.zeros((HALO, xb.shape[1]), BF16)
    x_ext = jnp.concatenate([jnp.where(first, zero, prev_ref[...]), xb,
                             jnp.where(last, zero, next_ref[...])], axis=0)
    ri = lax.broadcasted_iota(jnp.int32, (tm, tm + 2 * HALO), 0) + HALO
    ci = lax.broadcasted_iota(jnp.int32, (tm, tm + 2 * HALO), 1)
    band = ((ci == ri - 1) | (ci == ri + 1)).astype(BF16)
    x = xb.astype(F32)
    z = x + (0.5 * _dot(band, x_ext) - x) * mu_ref[...]

    r = z[:, :D_RWKV]
    k = z[:, D_RWKV:2 * D_RWKV]
    v = z[:, 2 * D_RWKV:3 * D_RWKV]
    xw = z[:, 3 * D_RWKV:3 * D_RWKV + 2 * LORA_RANK]
    xa = z[:, 3 * D_RWKV + 2 * LORA_RANK:]
    e, et = e_ref[...], et_ref[...]

    kk = k * kk_ref_p[...]
    kk = kk * lax.rsqrt(_head_sum_bcast(kk * kk, e, et) + L2_EPS)
    bonus_ref[...] = (_head_sum_bcast(r * k * rk_ref_p[...], e, et) * v).astype(BF16)
    r_ref[...] = r.astype(BF16)
    kk_ref[...] = kk.astype(BF16)
    v_ref[...] = v.astype(BF16)

    wpre = _dot(jnp.tanh(xw).astype(BF16), wup_ref[...]) + w0_ref[...]
    apre = _dot(xa.astype(BF16), aup_ref[...]) + a0_ref[...]
    lw = -math.exp(-0.5) * _sigmoid(wpre)
    a = _sigmoid(apre)
    ka = ka_ref_p[...]
    lwf_ref[...] = lw[:, :D_RWKV]
    lwb_ref[...] = lw[:, D_RWKV:]
    af, ab = a[:, :D_RWKV], a[:, D_RWKV:]
    kdf_ref[...] = (k * (1.0 + (af - 1.0) * ka)).astype(BF16)
    kdb_ref[...] = (k * (1.0 + (ab - 1.0) * ka)).astype(BF16)
    bf_ref[...] = (kk * af).astype(BF16)
    bb_ref[...] = (kk * ab).astype(BF16)


def _rw_pre(rw_in, mu, wup, aup, w0, a0, k_k, k_a, r_k, e, et, tm, seq):
    n = rw_in.shape[0]
    w = D_RW_IN
    nbh = n // HALO
    th = tm // HALO
    row = lambda i: (i, 0)
    const = lambda i: (0, 0)
    ospec = pl.BlockSpec((tm, D_RWKV), row)
    sb = jax.ShapeDtypeStruct((n, D_RWKV), BF16)
    sf = jax.ShapeDtypeStruct((n, D_RWKV), F32)
    return pl.pallas_call(
        functools.partial(_rw_pre_kernel, tm=tm, seq=seq),
        out_shape=(sb,) * 7 + (sf, sf, sb),
        grid=(n // tm,),
        in_specs=[pl.BlockSpec((tm, w), row),
                  pl.BlockSpec((HALO, w), lambda i: (jnp.maximum(i * th - 1, 0), 0)),
                  pl.BlockSpec((HALO, w), lambda i: (jnp.minimum((i + 1) * th, nbh - 1), 0)),
                  pl.BlockSpec((1, w), const),
                  pl.BlockSpec((2 * LORA_RANK, 2 * D_RWKV), const),
                  pl.BlockSpec((2 * LORA_RANK, 2 * D_RWKV), const),
                  pl.BlockSpec((1, 2 * D_RWKV), const),
                  pl.BlockSpec((1, 2 * D_RWKV), const),
                  pl.BlockSpec((1, D_RWKV), const),
                  pl.BlockSpec((1, D_RWKV), const),
                  pl.BlockSpec((1, D_RWKV), const),
                  pl.BlockSpec((D_RWKV, 128), const),
                  pl.BlockSpec((128, D_RWKV), const)],
        out_specs=(ospec,) * 10,
        compiler_params=_cparams(("parallel",)),
        name="rw_pre",
    )(rw_in, rw_in, rw_in, mu, wup, aup, w0, a0, k_k, k_a, r_k, e, et)


def _wkv_kernel(rf_ref, kkf_ref, vf_ref, kdf_ref, bf_ref, lwf_ref,
                rb_ref, kkb_ref, vb_ref, kdb_ref, bb_ref, lwb_ref, yf_ref, yb_ref, s_ref):
    t = WKV_CHUNK
    hd = RWKV_HEAD
    w = HEAD_PAIR

    @pl.when(pl.program_id(1) == 0)
    def _():
        s_ref[...] = jnp.zeros_like(s_ref)

    def iota(shape, dim):
        return lax.broadcasted_iota(jnp.int32, shape, dim)

    rr, cc = iota((t, t), 0), iota((t, t), 1)
    rw_, lane = iota((t, w), 0), iota((t, w), 1)
    col = lane % hd
    m0 = lane < hd
    eye_f = (rw_ == col).astype(F32)
    m0_2t = iota((2 * t, w), 1) < hd
    eye_w = iota((w, w), 0) == iota((w, w), 1)
    bd_mask = (iota((w, w), 0) < hd) == (iota((w, w), 1) < hd)

    def direction(rev, refs, y_ref):
        if rev:
            incl, strict, tri = rw_ <= col, rw_ < col, rr <= cc
        else:
            incl, strict, tri = rw_ >= col, rw_ > col, rr >= cc
        return dict(rev=int(rev), refs=refs, y_ref=y_ref, incl=incl, strict=strict,
                    tri=tri.astype(BF16),
                    mask_2t=jnp.concatenate([strict, incl], axis=0), t_last=0 if rev else t - 1)

    dirs = (direction(False, (rf_ref, kkf_ref, vf_ref, kdf_ref, bf_ref, lwf_ref), yf_ref),
            direction(True, (rb_ref, kkb_ref, vb_ref, kdb_ref, bb_ref, lwb_ref), yb_ref))

    def bdiag(x):
        return jnp.concatenate([jnp.where(m0, x, 0.0), jnp.where(m0, 0.0, x)], axis=0)

    def group(gi, carry):
        streams = [(dm, sq) for dm in dirs for sq in range(rf_ref.shape[0])]
        for k in range(0, len(streams), WKV_STREAMS_PER_BATCH):
            batch(streams[k:k + WKV_STREAMS_PER_BATCH], gi)
        return carry

    def batch(streams, gi):
        hv = []
        for dm, sq, j in [(dm, sq, j) for dm, sq in streams for j in range(WKV_PAIRS_PER_STEP)]:
            hp = gi * WKV_PAIRS_PER_STEP + j
            sl = pl.ds(pl.multiple_of(hp * HEAD_PAIR, HEAD_PAIR), HEAD_PAIR)
            r, kk, v, kd, beta = (ref[sq, :, sl].astype(F32) for ref in dm['refs'][:5])
            lw = dm['refs'][5][sq, :, sl]
            t_last = dm['t_last']
            lcum = lw
            for s in (1, 2, 4, 8, 16, 32):
                if dm['rev']:
                    lcum = lcum + jnp.where(rw_ < t - s, pltpu.roll(lcum, t - s, 0), 0.0)
                else:
                    lcum = lcum + jnp.where(rw_ >= s, pltpu.roll(lcum, s, 0), 0.0)
            lcum_x = lcum - lw
            cref = lcum[t // 2:t // 2 + 1, :]
            ltot = lcum[t_last:t_last + 1, :]
            e1 = jnp.exp(lcum - cref)
            e1x = jnp.exp(lcum_x - cref)
            e2 = jnp.exp(cref - lcum)
            ec = jnp.exp(cref)
            ewt = jnp.exp(ltot - cref)
            wtot = jnp.exp(ltot)
            r_t = r * e1
            a_t = -kk * e1x
            b_t = beta * e2
            k_t = kd * e2
            a_0 = a_t * ec
            r_0 = r_t * ec
            b_h = b_t * ewt
            k_h = k_t * ewt
            hv.append(dict(
                dm=dm, sq=sq, hp=hp, sl=sl, wtot=wtot, a_0=a_0, r_0=r_0,
                lm=jnp.concatenate([a_t, r_t], axis=0),
                rm=jnp.concatenate([b_t, k_t], axis=0).astype(BF16),
                bk=jnp.concatenate([b_h, k_h], axis=0).astype(BF16), v=v))
        for h in hv:
            lm2 = jnp.concatenate([jnp.where(m0_2t, h['lm'], 0.0), jnp.where(m0_2t, 0.0, h['lm'])], axis=0)
            g = _dot_nt(lm2.astype(BF16), h['rm'])
            g0, g1 = g[:2 * t], pltpu.roll(g[2 * t:], hd, 1)
            own = jnp.where(m0_2t, g0, g1)
            oth = jnp.where(m0_2t, g1, g0)
            h['a_ab'] = jnp.where(h['dm']['strict'], own[:t], 0.0)
            h['a_rb'] = jnp.where(h['dm']['incl'], own[t:], 0.0).astype(BF16)
            h['a_k'] = jnp.where(h['dm']['mask_2t'], oth, 0.0).astype(BF16)
        for h in hv:
            v = h['v']
            h['inv'] = eye_f + h['a_ab']
            h['nk'] = _dot(h['a_ab'].astype(BF16), bdiag(h['a_ab']).astype(BF16))
            v_rows = jnp.concatenate([jnp.where(m0, 0.0, v), jnp.where(m0, v, 0.0)], axis=0)
            h['av'] = _dot(h['a_k'], v_rows.astype(BF16))
        for s in range(1, NEUMANN_STEPS + 1):
            for h in hv:
                nk_bd = bdiag(h['nk']).astype(BF16)
                if s < NEUMANN_STEPS:
                    nx = _dot(jnp.concatenate([h['nk'], h['inv']], axis=0).astype(BF16), nk_bd)
                    h['nk'] = nx[:t]
                    h['inv'] = h['inv'] + nx[t:]
                else:
                    h['inv'] = h['inv'] + _dot(h['inv'].astype(BF16), nk_bd)
        for h in hv:
            rhs = jnp.concatenate([bdiag(h['a_0']), bdiag(h['av'][:t])], axis=1)
            h['pp'] = _dot(h['inv'].astype(BF16), rhs.astype(BF16))
        for h in hv:
            pp = h['pp']
            rhs = jnp.concatenate([bdiag(pp[:, :w]), bdiag(pp[:, w:])], axis=1)
            h['qq'] = (_dot(h['a_rb'], rhs.astype(BF16))
                       + jnp.concatenate([h['r_0'], h['av'][t:]], axis=1))
            pv = jnp.concatenate([pp, jnp.concatenate([jnp.zeros_like(h['v']), h['v']], axis=1)], axis=0)
            h['mn'] = _dot_tn(h['bk'], pv.astype(BF16))
        for h in hv:
            sidx = (h['dm']['rev'], h['sq'], h['hp'])
            st = s_ref[sidx]
            lhs = jnp.concatenate([h['qq'][:, :w], jnp.where(bd_mask, h['mn'][:, :w], 0.0)], axis=0)
            res = _dot(lhs.astype(BF16), st.astype(BF16))
            h['dm']['y_ref'][h['sq'], :, h['sl']] = res[:t] + h['qq'][:, w:]
            wcol = jnp.sum(jnp.where(eye_w, jnp.broadcast_to(h['wtot'], (w, w)), 0.0),
                           axis=1, keepdims=True)
            s_ref[sidx] = wcol * st + res[t:] + jnp.where(bd_mask, h['mn'][:, w:], 0.0)

    lax.fori_loop(0, RWKV_HEADS // (2 * WKV_PAIRS_PER_STEP), group, 0)


def _wkv(r, kk, v, kdf, bf, lwf, kdb, bb, lwb, nseq):
    n = r.shape[0]
    t = WKV_CHUNK
    seq = n // nseq
    nc = seq // t
    ns = WKV_SEQS if nseq % WKV_SEQS == 0 else 1
    fwd = pl.BlockSpec((ns, t, D_RWKV), lambda b, c: (b, c, 0))
    bwd = pl.BlockSpec((ns, t, D_RWKV), lambda b, c: (b, nc - 1 - c, 0))
    shape = jax.ShapeDtypeStruct((nseq, seq, D_RWKV), F32)
    yf, yb = pl.pallas_call(
        _wkv_kernel,
        out_shape=(shape, shape),
        grid=(nseq // ns, nc),
        in_specs=[fwd] * 6 + [bwd] * 6,
        out_specs=(fwd, bwd),
        scratch_shapes=[pltpu.VMEM((2, ns, RWKV_HEADS // 2, HEAD_PAIR, HEAD_PAIR), F32)],
        compiler_params=_cparams(("parallel", "arbitrary")),
        name="wkv",
    )(*[a.reshape(nseq, seq, D_RWKV) for a in (r, kk, v, kdf, bf, lwf, r, kk, v, kdb, bb, lwb)])
    return yf.reshape(n, D_RWKV), yb.reshape(n, D_RWKV)


def _tail_kernel(ys_ref, sg_ref, ms_ref, yf_ref, yb_ref, bonus_ref, rg_ref, mr_ref, x_ref,
                 wglu_ref, swout_ref, lnw_ref, lnb_ref, e_ref, et_ref, rwout_ref, wo_ref, g_ref,
                 o_ref, ytok_ref):
    f32 = lambda ref: ref[...].astype(F32)
    rows = ys_ref.shape[0]
    slabs = ytok_ref.shape[0]
    low = lax.broadcasted_iota(jnp.int32, (rows, 128), 1) < S5_HALF
    for o in range(slabs):
        for tp in range(S5_CHUNK // 2):
            lane = 2 * o * S5_OCT + tp * 128
            pack_a = ys_ref[:, lane:lane + 128].astype(F32)
            pack_b = ys_ref[:, lane + S5_OCT:lane + S5_OCT + 128].astype(F32)
            ytok_ref[o, pl.ds(2 * tp, rows, stride=S5_CHUNK), :] = (
                jnp.where(low, pack_a, pltpu.roll(pack_b, S5_HALF, 1)))
            ytok_ref[o, pl.ds(2 * tp + 1, rows, stride=S5_CHUNK), :] = (
                jnp.where(low, pltpu.roll(pack_a, S5_HALF, 1), pack_b))
    z = _gelu_tanh(jnp.concatenate([ytok_ref[o] for o in range(slabs)], axis=1))
    z = z * _sigmoid(_dot(z.astype(BF16), wglu_ref[...]))
    z = z * _silu(f32(sg_ref))
    h = _sigmoid(f32(ms_ref)) * _dot(z.astype(BF16), swout_ref[...])

    e, et = e_ref[...], et_ref[...]
    y = yf_ref[...] + yb_ref[...]
    inv_n = 1.0 / RWKV_HEAD
    mean = _head_sum_bcast(y, e, et) * inv_n
    yc = y - mean
    var = _head_sum_bcast(yc * yc, e, et) * inv_n
    y = yc * lax.rsqrt(var + GN_EPS) * lnw_ref[...] + lnb_ref[...] + f32(bonus_ref)
    y = y * _silu(f32(rg_ref))
    h = h + _sigmoid(f32(mr_ref)) * _dot(y.astype(BF16), rwout_ref[...])

    out = _dot(h.astype(BF16), wo_ref[...])
    ms = jnp.mean(out * out, axis=-1, keepdims=True)
    o_ref[...] = x_ref[...] + out * lax.rsqrt(ms + RMS_EPS) * g_ref[...]


def _tail(y_s5, proj, yf, yb, bonus, x, p, tm):
    n = x.shape[0]
    row = lambda i: (i, 0)
    const = lambda i: (0, 0)
    half = pl.BlockSpec((tm, D_RWKV), row)
    full = pl.BlockSpec((tm, D_MODEL), row)

    def proj_cols(width, blk):
        return pl.BlockSpec((tm, width), lambda i: (i, blk))

    def resident(shape):
        return pl.BlockSpec(shape, const, pipeline_mode=pl.Buffered(1))

    return pl.pallas_call(
        _tail_kernel,
        out_shape=jax.ShapeDtypeStruct((n, D_MODEL), F32),
        grid=(n // tm,),
        in_specs=[pl.BlockSpec((tm // S5_CHUNK, S5_CHUNK * D_S5), row),
                  proj_cols(D_S5, PROJ_SG_BLK), proj_cols(D_MODEL, PROJ_MG_BLK),
                  half, half, half,
                  proj_cols(D_RWKV, PROJ_RG_BLK), proj_cols(D_MODEL, PROJ_MG_BLK + 1),
                  full,
                  resident((D_S5, D_S5)), resident((D_S5, D_MODEL)),
                  resident((1, D_RWKV)), resident((1, D_RWKV)),
                  resident((D_RWKV, 128)), resident((128, D_RWKV)),
                  resident((D_RWKV, D_MODEL)), resident((D_MODEL, D_MODEL)),
                  resident((1, D_MODEL))],
        out_specs=full,
        scratch_shapes=[pltpu.VMEM((D_S5 // 128, tm, 128), F32)],
        compiler_params=_cparams(("parallel",)),
        name="tail",
    )(y_s5, proj, proj, yf, yb, bonus, proj, proj, x,
      p['w_glu'], p['s5_w_out'], p['ln_w'], p['ln_b'], p['e'], p['et'], p['rw_w_out'], p['w_o'],
      p['post_g'])


def _pick(n, pref):
    while n % pref:
        pref //= 2
    return pref


def _prepare(pre_norm_g, post_norm_g, w_in, s5_lam_re, s5_lam_im, s5_log_dt, s5_b_re, s5_b_im,
             s5_c_re, s5_c_im, s5_d, s5_w_glu, s5_w_out, rw_mu, rw_w0, rw_w_up, rw_a0, rw_a_up,
             rw_k_k, rw_k_a, rw_r_k, rw_ln_w, rw_ln_b, rw_w_out, w_o):
    p = {}
    p['pre_g'] = pre_norm_g.astype(F32).reshape(1, D_MODEL)
    p['post_g'] = post_norm_g.astype(F32).reshape(1, D_MODEL)
    w = w_in.astype(BF16)
    rw_end = 2 * D_S5 + D_RW_IN
    p['w_u'] = w[:, :D_S5]
    p['w_proj'] = jnp.concatenate(
        [w[:, 2 * D_S5:rw_end], jnp.zeros((D_MODEL, PROJ_RW_W - D_RW_IN), BF16),
         w[:, D_S5:2 * D_S5], w[:, rw_end:]], axis=1)
    p['s5_m'], p['s5_p'], p['s5_q'], p['s5_a'] = _s5_weights(
        s5_lam_re, s5_lam_im, s5_log_dt, s5_b_re, s5_b_im, s5_c_re, s5_c_im, s5_d)
    p['w_glu'] = s5_w_glu.astype(BF16)
    p['s5_w_out'] = s5_w_out.astype(BF16)
    p['mu'] = rw_mu.astype(F32).reshape(1, D_RW_IN)
    zeros = jnp.zeros((LORA_RANK, D_RWKV), F32)

    def blockdiag(u):
        return jnp.concatenate([jnp.concatenate([u[0], zeros], axis=1),
                                jnp.concatenate([zeros, u[1]], axis=1)], axis=0).astype(BF16)

    p['wup'] = blockdiag(rw_w_up.astype(F32))
    p['aup'] = blockdiag(rw_a_up.astype(F32))
    p['w0'] = rw_w0.astype(F32).reshape(1, 2 * D_RWKV)
    p['a0'] = rw_a0.astype(F32).reshape(1, 2 * D_RWKV)
    p['k_k'] = rw_k_k.astype(F32).reshape(1, D_RWKV)
    p['k_a'] = rw_k_a.astype(F32).reshape(1, D_RWKV)
    p['r_k'] = rw_r_k.astype(F32).reshape(1, D_RWKV)
    p['ln_w'] = rw_ln_w.astype(F32).reshape(1, D_RWKV)
    p['ln_b'] = rw_ln_b.astype(F32).reshape(1, D_RWKV)
    p['rw_w_out'] = rw_w_out.astype(BF16)
    p['w_o'] = w_o.astype(BF16)
    e = _head_indicator()
    p['e'] = e
    p['et'] = e.T
    return p


def _s5_mix(x2, p, nseq):
    r = x2.shape[0]
    rb = _pick(r, 512)
    (y0,) = _s5_mm(x2, p['s5_m'], rb, 1, "s5_chunk_out")
    pfr, pfi, pbr, pbi = _s5_mm(x2, p['s5_p'], rb, 4, "s5_chunk_state")
    cb = _pick(r // nseq, 128)
    hfr, hfi, hbr, hbi = _s5_scan(p['s5_a'], pfr, pfi, pbr, pbi, nseq, cb, 1024)
    return _s5_state_out(y0, hfr, hfi, hbr, hbi, p['s5_q'], rb)


def _layer(x, p):
    bsz, seq, _ = x.shape
    n = bsz * seq
    x2 = x.reshape(n, D_MODEL)
    u_rows = _in_proj_u(x2, p['pre_g'], p['w_u'], _pick(n, 512))
    proj = _in_proj(x2, p['pre_g'], p['w_proj'], _pick(n, 1024))

    y_s5 = _s5_mix(u_rows, p, bsz)

    (r, kk, v, kdf, kdb, bf, bb, lwf, lwb, bonus) = _rw_pre(
        proj, p['mu'], p['wup'], p['aup'], p['w0'], p['a0'], p['k_k'], p['k_a'], p['r_k'],
        p['e'], p['et'], _pick(seq, 256), seq)
    yf, yb = _wkv(r, kk, v, kdf, bf, lwf, kdb, bb, lwb, bsz)

    out = _tail(y_s5, proj, yf, yb, bonus, x2, p, _pick(n, 256))
    return out.reshape(bsz, seq, D_MODEL)


def kernel(x_prompt, x_sample, pre_norm_g, post_norm_g, w_in, s5_lam_re, s5_lam_im, s5_log_dt, s5_b_re, s5_b_im, s5_c_re, s5_c_im, s5_d, s5_w_glu, s5_w_out, rw_mu, rw_w0, rw_w_up, rw_a0, rw_a_up, rw_k_k, rw_k_a, rw_r_k, rw_ln_w, rw_ln_b, rw_w_out, w_o):
    params = (pre_norm_g, post_norm_g, w_in, s5_lam_re, s5_lam_im, s5_log_dt, s5_b_re, s5_b_im,
              s5_c_re, s5_c_im, s5_d, s5_w_glu, s5_w_out, rw_mu, rw_w0, rw_w_up, rw_a0, rw_a_up,
              rw_k_k, rw_k_a, rw_r_k, rw_ln_w, rw_ln_b, rw_w_out, w_o)
    y_prompt, y_sample = x_prompt, x_sample
    for layer in range(w_in.shape[0]):
        p = _prepare(*[w[layer] for w in params])
        y_prompt = _layer(y_prompt, p)
        y_sample = _layer(y_sample, p)
    return (y_prompt, y_sample)
```

```python
import functools
import math

import jax
import jax.numpy as jnp
from jax import lax
from jax.experimental import pallas as pl
from jax.experimental.pallas import tpu as pltpu

F32 = jnp.float32
BF16 = jnp.bfloat16

D_MODEL = 2048
D_S5 = 1024
S5_GROUP = 16
S5_GROUPS = 64
S5_STATE = 64
D_RWKV = 1024
RWKV_HEAD = 64
RWKV_HEADS = 16
LORA_RANK = 64
D_RW_IN = 3 * D_RWKV + 4 * LORA_RANK
RMS_EPS = 1e-6
GN_EPS = 64e-5
L2_EPS = 1e-12

S5_CHUNK = 16
S5_OCT_GROUPS = 4
S5_OCT = S5_OCT_GROUPS * S5_CHUNK * S5_GROUP
S5_HALF = S5_OCT_GROUPS * S5_GROUP
WKV_CHUNK = 64
HEAD_PAIR = 2 * RWKV_HEAD
NEUMANN_STEPS = 5
WKV_PAIRS_PER_STEP = 8
WKV_STREAMS_PER_BATCH = 2
WKV_SEQS = 2
HALO = 16

VMEM_LIMIT = 48 * 1024 * 1024


def _cparams(sem):
    return pltpu.CompilerParams(dimension_semantics=sem, vmem_limit_bytes=VMEM_LIMIT)


def _dot(a, b):
    return jnp.dot(a, b, preferred_element_type=F32)


def _dot_nt(a, b):
    return lax.dot_general(a, b, (((1,), (1,)), ((), ())), preferred_element_type=F32)


def _dot_tn(a, b):
    return lax.dot_general(a, b, (((0,), (0,)), ((), ())), preferred_element_type=F32)


def _sigmoid(x):
    return 1.0 / (1.0 + jnp.exp2(x * (-math.log2(math.e))))


def _silu(x):
    return x * _sigmoid(x)


def _gelu_tanh(x):
    c = math.sqrt(2.0 / math.pi)
    return 0.5 * x * (1.0 + jnp.tanh(c * (x + 0.044715 * (x * x * x))))


IN_TN = 1024
PROJ_RW_W = 4096
PROJ_W = PROJ_RW_W + D_S5 + D_RWKV + 2 * D_MODEL
PROJ_SG_BLK = PROJ_RW_W // D_S5
PROJ_RG_BLK = PROJ_SG_BLK + 1
PROJ_MG_BLK = (PROJ_RW_W + D_S5 + D_RWKV) // D_MODEL


def _rms_bf16(x, g):
    ms = jnp.mean(x * x, axis=-1, keepdims=True)
    return (x * lax.rsqrt(ms + RMS_EPS) * g).astype(BF16)


def _in_proj_u_kernel(x_ref, g_ref, w_ref, o_ref, ubuf_ref):
    res = _dot(_rms_bf16(x_ref[...], g_ref[...]), w_ref[...])
    rows = ubuf_ref.shape[1] // S5_CHUNK
    low = lax.broadcasted_iota(jnp.int32, (rows, 128), 1) < S5_HALF
    for o in range(ubuf_ref.shape[0]):
        ubuf_ref[o] = res[:, o * 128:(o + 1) * 128]
        for tp in range(S5_CHUNK // 2):
            te = ubuf_ref[o, pl.ds(2 * tp, rows, stride=S5_CHUNK), :]
            to = ubuf_ref[o, pl.ds(2 * tp + 1, rows, stride=S5_CHUNK), :]
            pack_a = jnp.where(low, te, pltpu.roll(to, S5_HALF, 1))
            pack_b = jnp.where(low, pltpu.roll(te, S5_HALF, 1), to)
            lane = 2 * o * S5_OCT + tp * 128
            o_ref[:, lane:lane + 128] = pack_a.astype(BF16)
            o_ref[:, lane + S5_OCT:lane + S5_OCT + 128] = pack_b.astype(BF16)


def _in_proj_u(x, g, w_u, tm):
    n, d = x.shape
    return pl.pallas_call(
        _in_proj_u_kernel,
        out_shape=jax.ShapeDtypeStruct((n // S5_CHUNK, D_S5 * S5_CHUNK), BF16),
        grid=(n // tm,),
        in_specs=[pl.BlockSpec((tm, d), lambda i: (i, 0)),
                  pl.BlockSpec((1, d), lambda i: (0, 0)),
                  pl.BlockSpec((d, D_S5), lambda i: (0, 0))],
        out_specs=pl.BlockSpec((tm // S5_CHUNK, D_S5 * S5_CHUNK), lambda i: (i, 0)),
        scratch_shapes=[pltpu.VMEM((D_S5 // 128, tm, 128), F32)],
        compiler_params=_cparams(("parallel",)),
        name="in_proj_u",
    )(x, g, w_u)


def _in_proj_kernel(x_ref, g_ref, w_ref, o_ref, hn_ref):
    @pl.when(pl.program_id(1) == 0)
    def _():
        hn_ref[...] = _rms_bf16(x_ref[...], g_ref[...])

    o_ref[...] = _dot(hn_ref[...], w_ref[...]).astype(BF16)


def _in_proj(x, g, w, tm):
    n, d = x.shape
    return pl.pallas_call(
        _in_proj_kernel,
        out_shape=jax.ShapeDtypeStruct((n, w.shape[1]), BF16),
        grid=(n // tm, w.shape[1] // IN_TN),
        in_specs=[pl.BlockSpec((tm, d), lambda i, j: (i, 0)),
                  pl.BlockSpec((1, d), lambda i, j: (0, 0)),
                  pl.BlockSpec((d, IN_TN), lambda i, j: (0, j))],
        out_specs=pl.BlockSpec((tm, IN_TN), lambda i, j: (i, j)),
        scratch_shapes=[pltpu.VMEM((tm, d), BF16)],
        compiler_params=_cparams(("parallel", "arbitrary")),
        name="in_proj",
    )(x, g, w)


def _s5_weights(lam_re, lam_im, log_dt, b_re, b_im, c_re, c_im, d):
    t = S5_CHUNK
    g, p, h = S5_GROUPS, S5_STATE, S5_GROUP
    hp = lax.Precision.HIGHEST
    lam = lax.complex(lam_re.astype(F32), lam_im.astype(F32))
    dt = jnp.exp(log_dt.astype(F32))[..., None]
    lam_dt = lam * dt
    lam_bar = jnp.exp(lam_dt)
    bbar = ((lam_bar - 1.0) / lam)[..., None] * lax.complex(b_re.astype(F32), b_im.astype(F32))
    c = lax.complex(c_re.astype(F32), c_im.astype(F32))
    steps = jnp.arange(t + 1, dtype=F32)
    pw = jnp.exp(lam_dt[None] * steps[:, None, None, None])

    kern = jnp.einsum('dghp,ldgp,dgpj->dglhj', c, pw[:t], bbar, precision=hp).real
    tt = jnp.arange(t)
    k0 = kern[0][:, 0] + kern[1][:, 0] + jnp.eye(h, dtype=F32) * d.astype(F32).reshape(g, h)[:, :, None]
    by_lag = jnp.concatenate([kern[1][:, :0:-1], k0[:, None], kern[0][:, 1:]], axis=1)
    wide = by_lag.transpose(0, 3, 1, 2).reshape(g, h, (2 * t - 1) * h).astype(BF16)
    m = jnp.stack([wide[:, :, (t - 1 - ti) * h:(t - 1 - ti) * h + t * h] for ti in range(t)], axis=1)
    m = m.reshape(g, t * h, t * h)

    pf = pw[t - 1 - tt, 0][..., None] * bbar[0][None]
    pb = pw[tt, 1][..., None] * bbar[1][None]

    def _p_mat(x):
        return x.transpose(1, 0, 3, 2).reshape(g, t * h, p)

    p_parts = [_p_mat(pf.real), _p_mat(pf.imag), _p_mat(pb.real), _p_mat(pb.imag)]

    cf = c[0][None] * pw[tt + 1, 0][:, :, None, :]
    cb = c[1][None] * pw[t - tt, 1][:, :, None, :]

    def _q_mat(x):
        return x.transpose(1, 3, 0, 2).reshape(g, p, t * h)

    q_parts = [_q_mat(cf.real), _q_mat(-cf.imag), _q_mat(cb.real), _q_mat(-cb.imag)]

    k8 = S5_OCT_GROUPS
    no = g // k8
    gi = jnp.arange(k8)

    def _spread(width):
        c = jnp.arange(width)
        tgt = (gi[:, None] * width + c[None, :])[:, :, None]
        return (tgt == jnp.arange(k8 * width)[None, None, :]).astype(BF16)

    def _spread_tok():
        u, hh = jnp.arange(t * h) // h, jnp.arange(t * h) % h
        tgt = (u[None, :] * (k8 * h) + gi[:, None] * h + hh[None, :])[:, :, None]
        return (tgt == jnp.arange(S5_OCT)[None, None, :]).astype(BF16)

    def _cols(x, spread):
        x = x.astype(BF16).reshape(no, k8, x.shape[1], x.shape[2])
        return jnp.einsum('ogrc,gcd->ogrd', x, spread, preferred_element_type=BF16)

    def _rows_tok(x):
        c = x.shape[-1]
        return x.reshape(no, k8, t, h, c).transpose(0, 2, 1, 3, 4).reshape(no, S5_OCT, c)

    sp_tok, sp_state = _spread_tok(), _spread(p)
    m_oct = _rows_tok(_cols(m, sp_tok))
    w_p = jnp.concatenate([_rows_tok(_cols(x, sp_state)) for x in p_parts], axis=2)
    qmat = jnp.concatenate([_cols(x, sp_tok).reshape(no, k8 * p, S5_OCT) for x in q_parts],
                           axis=1)
    at = pw[t]
    a_rows = jnp.stack([at[0].real, at[0].imag, at[1].real, at[1].imag]).reshape(4, g * p)
    return m_oct, w_p, qmat, a_rows


def _s5_mm_kernel(x_ref, w_ref, *o_refs):
    res = _dot(x_ref[...], w_ref[0])
    n = res.shape[1] // len(o_refs)
    for k, ref in enumerate(o_refs):
        ref[...] = res[:, n * k:n * (k + 1)].astype(ref.dtype)


def _s5_mm(x2, w, rb, nout, name):
    r = x2.shape[0]
    no, _, c = w.shape
    cw = c // nout
    return pl.pallas_call(
        _s5_mm_kernel,
        out_shape=tuple(jax.ShapeDtypeStruct((r, no * cw), F32) for _ in range(nout)),
        grid=(no, r // rb),
        in_specs=[pl.BlockSpec((rb, S5_OCT), lambda o, i: (i, o)),
                  pl.BlockSpec((1, S5_OCT, c), lambda o, i: (o, 0, 0))],
        out_specs=tuple(pl.BlockSpec((rb, cw), lambda o, i: (i, o)) for _ in range(nout)),
        compiler_params=_cparams(("parallel", "arbitrary")),
        name=name,
    )(x2, w)


def _s5_scan_kernel(a_ref, pfr_ref, pfi_ref, pbr_ref, pbi_ref,
                    hfr_ref, hfi_ref, hbr_ref, hbi_ref, carry_ref, *, rows):
    @pl.when(pl.program_id(2) == 0)
    def _():
        carry_ref[...] = jnp.zeros_like(carry_ref)

    afr, afi = a_ref[0:1, :], a_ref[1:2, :]
    abr, abi = a_ref[2:3, :], a_ref[3:4, :]

    def tile(i, carry):
        hfr, hfi, hbr, hbi = carry
        r0 = pl.multiple_of(i * 8, 8)
        pfr, pfi = pfr_ref[pl.ds(r0, 8), :], pfi_ref[pl.ds(r0, 8), :]
        rb0 = pl.multiple_of(rows - 8 - i * 8, 8)
        pbr, pbi = pbr_ref[pl.ds(rb0, 8), :], pbi_ref[pl.ds(rb0, 8), :]
        of_r, of_i, ob_r, ob_i = [], [], [None] * 8, [None] * 8
        for s in range(8):
            of_r.append(hfr)
            of_i.append(hfi)
            hfr, hfi = (afr * hfr - afi * hfi + pfr[s:s + 1, :],
                        afr * hfi + afi * hfr + pfi[s:s + 1, :])
            sb = 7 - s
            ob_r[sb] = hbr
            ob_i[sb] = hbi
            hbr, hbi = (abr * hbr - abi * hbi + pbr[sb:sb + 1, :],
                        abr * hbi + abi * hbr + pbi[sb:sb + 1, :])
        hfr_ref[pl.ds(r0, 8), :] = jnp.concatenate(of_r, axis=0)
        hfi_ref[pl.ds(r0, 8), :] = jnp.concatenate(of_i, axis=0)
        hbr_ref[pl.ds(rb0, 8), :] = jnp.concatenate(ob_r, axis=0)
        hbi_ref[pl.ds(rb0, 8), :] = jnp.concatenate(ob_i, axis=0)
        return hfr, hfi, hbr, hbi

    init = tuple(carry_ref[k:k + 1, :] for k in range(4))
    out = lax.fori_loop(0, rows // 8, tile, init)
    for k in range(4):
        carry_ref[k:k + 1, :] = out[k]


def _s5_scan(a_rows, pfr, pfi, pbr, pbi, nseq, cb, lt):
    r, w = pfr.shape
    nblk = r // nseq // cb
    fspec = pl.BlockSpec((cb, lt), lambda b, j, i: (b * nblk + i, j))
    bspec = pl.BlockSpec((cb, lt), lambda b, j, i: (b * nblk + nblk - 1 - i, j))
    st = jax.ShapeDtypeStruct((r, w), F32)
    return pl.pallas_call(
        functools.partial(_s5_scan_kernel, rows=cb),
        out_shape=(st, st, st, st),
        grid=(nseq, w // lt, nblk),
        in_specs=[pl.BlockSpec((4, lt), lambda b, j, i: (0, j)), fspec, fspec, bspec, bspec],
        out_specs=(fspec, fspec, bspec, bspec),
        scratch_shapes=[pltpu.VMEM((4, lt), F32)],
        compiler_params=_cparams(("parallel", "parallel", "arbitrary")),
        name="s5_scan",
    )(a_rows, pfr, pfi, pbr, pbi)


def _s5_state_out_kernel(y0_ref, hfr_ref, hfi_ref, hbr_ref, hbi_ref, q_ref, y_ref):
    h = jnp.concatenate([hfr_ref[...], hfi_ref[...], hbr_ref[...], hbi_ref[...]], axis=1)
    y_ref[...] = (y0_ref[...] + _dot(h.astype(BF16), q_ref[0])).astype(BF16)


def _s5_state_out(y0, hfr, hfi, hbr, hbi, qmat, rb):
    r = y0.shape[0]
    no, c, _ = qmat.shape
    hspec = pl.BlockSpec((rb, c // 4), lambda o, i: (i, o))
    yspec = pl.BlockSpec((rb, S5_OCT), lambda o, i: (i, o))
    return pl.pallas_call(
        _s5_state_out_kernel,
        out_shape=jax.ShapeDtypeStruct(y0.shape, BF16),
        grid=(no, r // rb),
        in_specs=[yspec, hspec, hspec, hspec, hspec,
                  pl.BlockSpec((1, c, S5_OCT), lambda o, i: (o, 0, 0))],
        out_specs=yspec,
        compiler_params=_cparams(("parallel", "arbitrary")),
        name="s5_state_out",
    )(y0, hfr, hfi, hbr, hbi, qmat)


def _head_indicator():
    lane = lax.broadcasted_iota(jnp.int32, (D_RWKV, 128), 0) // RWKV_HEAD
    col = lax.broadcasted_iota(jnp.int32, (D_RWKV, 128), 1)
    return (lane == col).astype(BF16)


def _head_sum_bcast(x, e, et):
    return _dot(_dot(x.astype(BF16), e).astype(BF16), et)


def _rw_pre_kernel(x_ref, prev_ref, next_ref, mu_ref, wup_ref, aup_ref, w0_ref, a0_ref,
                   kk_ref_p, ka_ref_p, rk_ref_p, e_ref, et_ref,
                   r_ref, kk_ref, v_ref, kdf_ref, kdb_ref, bf_ref, bb_ref, lwf_ref, lwb_ref,
                   bonus_ref, *, tm, seq):
    i = pl.program_id(0)
    xb = x_ref[...]
    first = (i * tm) % seq == 0
    last = ((i + 1) * tm) % seq == 0
    zero = jnp.zeros((HALO, xb.shape[1]), BF16)
    x_ext = jnp.concatenate([jnp.where(first, zero, prev_ref[...]), xb,
                             jnp.where(last, zero, next_ref[...])], axis=0)
    ri = lax.broadcasted_iota(jnp.int32, (tm, tm + 2 * HALO), 0) + HALO
    ci = lax.broadcasted_iota(jnp.int32, (tm, tm + 2 * HALO), 1)
    band = ((ci == ri - 1) | (ci == ri + 1)).astype(BF16)
    x = xb.astype(F32)
    z = x + (0.5 * _dot(band, x_ext) - x) * mu_ref[...]

    r = z[:, :D_RWKV]
    k = z[:, D_RWKV:2 * D_RWKV]
    v = z[:, 2 * D_RWKV:3 * D_RWKV]
    xw = z[:, 3 * D_RWKV:3 * D_RWKV + 2 * LORA_RANK]
    xa = z[:, 3 * D_RWKV + 2 * LORA_RANK:]
    e, et = e_ref[...], et_ref[...]

    kk = k * kk_ref_p[...]
    kk = kk * lax.rsqrt(_head_sum_bcast(kk * kk, e, et) + L2_EPS)
    bonus_ref[...] = (_head_sum_bcast(r * k * rk_ref_p[...], e, et) * v).astype(BF16)
    r_ref[...] = r.astype(BF16)
    kk_ref[...] = kk.astype(BF16)
    v_ref[...] = v.astype(BF16)

    wpre = _dot(jnp.tanh(xw).astype(BF16), wup_ref[...]) + w0_ref[...]
    apre = _dot(xa.astype(BF16), aup_ref[...]) + a0_ref[...]
    lw = -math.exp(-0.5) * _sigmoid(wpre)
    a = _sigmoid(apre)
    ka = ka_ref_p[...]
    lwf_ref[...] = lw[:, :D_RWKV]
    lwb_ref[...] = lw[:, D_RWKV:]
    af, ab = a[:, :D_RWKV], a[:, D_RWKV:]
    kdf_ref[...] = (k * (1.0 + (af - 1.0) * ka)).astype(BF16)
    kdb_ref[...] = (k * (1.0 + (ab - 1.0) * ka)).astype(BF16)
    bf_ref[...] = (kk * af).astype(BF16)
    bb_ref[...] = (kk * ab).astype(BF16)


def _rw_pre(rw_in, mu, wup, aup, w0, a0, k_k, k_a, r_k, e, et, tm, seq):
    n = rw_in.shape[0]
    w = D_RW_IN
    nbh = n // HALO
    th = tm // HALO
    row = lambda i: (i, 0)
    const = lambda i: (0, 0)
    ospec = pl.BlockSpec((tm, D_RWKV), row)
    sb = jax.ShapeDtypeStruct((n, D_RWKV), BF16)
    sf = jax.ShapeDtypeStruct((n, D_RWKV), F32)
    return pl.pallas_call(
        functools.partial(_rw_pre_kernel, tm=tm, seq=seq),
        out_shape=(sb,) * 7 + (sf, sf, sb),
        grid=(n // tm,),
        in_specs=[pl.BlockSpec((tm, w), row),
                  pl.BlockSpec((HALO, w), lambda i: (jnp.maximum(i * th - 1, 0), 0)),
                  pl.BlockSpec((HALO, w), lambda i: (jnp.minimum((i + 1) * th, nbh - 1), 0)),
                  pl.BlockSpec((1, w), const),
                  pl.BlockSpec((2 * LORA_RANK, 2 * D_RWKV), const),
                  pl.BlockSpec((2 * LORA_RANK, 2 * D_RWKV), const),
                  pl.BlockSpec((1, 2 * D_RWKV), const),
                  pl.BlockSpec((1, 2 * D_RWKV), const),
                  pl.BlockSpec((1, D_RWKV), const),
                  pl.BlockSpec((1, D_RWKV), const),
                  pl.BlockSpec((1, D_RWKV), const),
                  pl.BlockSpec((D_RWKV, 128), const),
                  pl.BlockSpec((128, D_RWKV), const)],
        out_specs=(ospec,) * 10,
        compiler_params=_cparams(("parallel",)),
        name="rw_pre",
    )(rw_in, rw_in, rw_in, mu, wup, aup, w0, a0, k_k, k_a, r_k, e, et)


def _wkv_kernel(rf_ref, kkf_ref, vf_ref, kdf_ref, bf_ref, lwf_ref,
                rb_ref, kkb_ref, vb_ref, kdb_ref, bb_ref, lwb_ref, yf_ref, yb_ref, s_ref):
    t = WKV_CHUNK
    hd = RWKV_HEAD
    w = HEAD_PAIR

    @pl.when(pl.program_id(1) == 0)
    def _():
        s_ref[...] = jnp.zeros_like(s_ref)

    def iota(shape, dim):
        return lax.broadcasted_iota(jnp.int32, shape, dim)

    rr, cc = iota((t, t), 0), iota((t, t), 1)
    rw_, lane = iota((t, w), 0), iota((t, w), 1)
    col = lane % hd
    m0 = lane < hd
    eye_f = (rw_ == col).astype(F32)
    m0_2t = iota((2 * t, w), 1) < hd
    eye_w = iota((w, w), 0) == iota((w, w), 1)
    bd_mask = (iota((w, w), 0) < hd) == (iota((w, w), 1) < hd)

    def direction(rev, refs, y_ref):
        if rev:
            incl, strict, tri = rw_ <= col, rw_ < col, rr <= cc
        else:
            incl, strict, tri = rw_ >= col, rw_ > col, rr >= cc
        return dict(rev=int(rev), refs=refs, y_ref=y_ref, incl=incl, strict=strict,
                    tri=tri.astype(BF16),
                    mask_2t=jnp.concatenate([strict, incl], axis=0), t_last=0 if rev else t - 1)

    dirs = (direction(False, (rf_ref, kkf_ref, vf_ref, kdf_ref, bf_ref, lwf_ref), yf_ref),
            direction(True, (rb_ref, kkb_ref, vb_ref, kdb_ref, bb_ref, lwb_ref), yb_ref))

    def bdiag(x):
        return jnp.concatenate([jnp.where(m0, x, 0.0), jnp.where(m0, 0.0, x)], axis=0)

    def group(gi, carry):
        streams = [(dm, sq) for dm in dirs for sq in range(rf_ref.shape[0])]
        for k in range(0, len(streams), WKV_STREAMS_PER_BATCH):
            batch(streams[k:k + WKV_STREAMS_PER_BATCH], gi)
        return carry

    def batch(streams, gi):
        hv = []
        for dm, sq, j in [(dm, sq, j) for dm, sq in streams for j in range(WKV_PAIRS_PER_STEP)]:
            hp = gi * WKV_PAIRS_PER_STEP + j
            sl = pl.ds(pl.multiple_of(hp * HEAD_PAIR, HEAD_PAIR), HEAD_PAIR)
            r, kk, v, kd, beta = (ref[sq, :, sl].astype(F32) for ref in dm['refs'][:5])
            lw = dm['refs'][5][sq, :, sl]
            t_last = dm['t_last']
            lcum = lw
            for s in (1, 2, 4, 8, 16, 32):
                if dm['rev']:
                    lcum = lcum + jnp.where(rw_ < t - s, pltpu.roll(lcum, t - s, 0), 0.0)
                else:
                    lcum = lcum + jnp.where(rw_ >= s, pltpu.roll(lcum, s, 0), 0.0)
            lcum_x = lcum - lw
            cref = lcum[t // 2:t // 2 + 1, :]
            ltot = lcum[t_last:t_last + 1, :]
            e1 = jnp.exp(lcum - cref)
            e1x = jnp.exp(lcum_x - cref)
            e2 = jnp.exp(cref - lcum)
            ec = jnp.exp(cref)
            ewt = jnp.exp(ltot - cref)
            wtot = jnp.exp(ltot)
            r_t = r * e1
            a_t = -kk * e1x
            b_t = beta * e2
            k_t = kd * e2
            a_0 = a_t * ec
            r_0 = r_t * ec
            b_h = b_t * ewt
            k_h = k_t * ewt
            hv.append(dict(
                dm=dm, sq=sq, hp=hp, sl=sl, wtot=wtot, a_0=a_0, r_0=r_0,
                lm=jnp.concatenate([a_t, r_t], axis=0),
                rm=jnp.concatenate([b_t, k_t], axis=0).astype(BF16),
                bk=jnp.concatenate([b_h, k_h], axis=0).astype(BF16), v=v))
        for h in hv:
            lm2 = jnp.concatenate([jnp.where(m0_2t, h['lm'], 0.0), jnp.where(m0_2t, 0.0, h['lm'])], axis=0)
            g = _dot_nt(lm2.astype(BF16), h['rm'])
            g0, g1 = g[:2 * t], pltpu.roll(g[2 * t:], hd, 1)
            own = jnp.where(m0_2t, g0, g1)
            oth = jnp.where(m0_2t, g1, g0)
            h['a_ab'] = jnp.where(h['dm']['strict'], own[:t], 0.0)
            h['a_rb'] = jnp.where(h['dm']['incl'], own[t:], 0.0).astype(BF16)
            h['a_k'] = jnp.where(h['dm']['mask_2t'], oth, 0.0).astype(BF16)
        for h in hv:
            v = h['v']
            h['inv'] = eye_f + h['a_ab']
            h['nk'] = _dot(h['a_ab'].astype(BF16), bdiag(h['a_ab']).astype(BF16))
            v_rows = jnp.concatenate([jnp.where(m0, 0.0, v), jnp.where(m0, v, 0.0)], axis=0)
            h['av'] = _dot(h['a_k'], v_rows.astype(BF16))
        for s in range(1, NEUMANN_STEPS + 1):
            for h in hv:
                nk_bd = bdiag(h['nk']).astype(BF16)
                if s < NEUMANN_STEPS:
                    nx = _dot(jnp.concatenate([h['nk'], h['inv']], axis=0).astype(BF16), nk_bd)
                    h['nk'] = nx[:t]
                    h['inv'] = h['inv'] + nx[t:]
                else:
                    h['inv'] = h['inv'] + _dot(h['inv'].astype(BF16), nk_bd)
        for h in hv:
            rhs = jnp.concatenate([bdiag(h['a_0']), bdiag(h['av'][:t])], axis=1)
            h['pp'] = _dot(h['inv'].astype(BF16), rhs.astype(BF16))
        for h in hv:
            pp = h['pp']
            rhs = jnp.concatenate([bdiag(pp[:, :w]), bdiag(pp[:, w:])], axis=1)
            h['qq'] = (_dot(h['a_rb'], rhs.astype(BF16))
                       + jnp.concatenate([h['r_0'], h['av'][t:]], axis=1))
            pv = jnp.concatenate([pp, jnp.concatenate([jnp.zeros_like(h['v']), h['v']], axis=1)], axis=0)
            h['mn'] = _dot_tn(h['bk'], pv.astype(BF16))
        for h in hv:
            sidx = (h['dm']['rev'], h['sq'], h['hp'])
            st = s_ref[sidx]
            lhs = jnp.concatenate([h['qq'][:, :w], jnp.where(bd_mask, h['mn'][:, :w], 0.0)], axis=0)
            res = _dot(lhs.astype(BF16), st.astype(BF16))
            h['dm']['y_ref'][h['sq'], :, h['sl']] = res[:t] + h['qq'][:, w:]
            wcol = jnp.sum(jnp.where(eye_w, jnp.broadcast_to(h['wtot'], (w, w)), 0.0),
                           axis=1, keepdims=True)
            s_ref[sidx] = wcol * st + res[t:] + jnp.where(bd_mask, h['mn'][:, w:], 0.0)

    lax.fori_loop(0, RWKV_HEADS // (2 * WKV_PAIRS_PER_STEP), group, 0)


def _wkv(r, kk, v, kdf, bf, lwf, kdb, bb, lwb, nseq):
    n = r.shape[0]
    t = WKV_CHUNK
    seq = n // nseq
    nc = seq // t
    ns = WKV_SEQS if nseq % WKV_SEQS == 0 else 1
    fwd = pl.BlockSpec((ns, t, D_RWKV), lambda b, c: (b, c, 0))
    bwd = pl.BlockSpec((ns, t, D_RWKV), lambda b, c: (b, nc - 1 - c, 0))
    shape = jax.ShapeDtypeStruct((nseq, seq, D_RWKV), F32)
    yf, yb = pl.pallas_call(
        _wkv_kernel,
        out_shape=(shape, shape),
        grid=(nseq // ns, nc),
        in_specs=[fwd] * 6 + [bwd] * 6,
        out_specs=(fwd, bwd),
        scratch_shapes=[pltpu.VMEM((2, ns, RWKV_HEADS // 2, HEAD_PAIR, HEAD_PAIR), F32)],
        compiler_params=_cparams(("parallel", "arbitrary")),
        name="wkv",
    )(*[a.reshape(nseq, seq, D_RWKV) for a in (r, kk, v, kdf, bf, lwf, r, kk, v, kdb, bb, lwb)])
    return yf.reshape(n, D_RWKV), yb.reshape(n, D_RWKV)


def _tail_kernel(ys_ref, sg_ref, ms_ref, yf_ref, yb_ref, bonus_ref, rg_ref, mr_ref, x_ref,
                 wglu_ref, swout_ref, lnw_ref, lnb_ref, e_ref, et_ref, rwout_ref, wo_ref, g_ref,
                 o_ref, ytok_ref):
    f32 = lambda ref: ref[...].astype(F32)
    rows = ys_ref.shape[0]
    slabs = ytok_ref.shape[0]
    low = lax.broadcasted_iota(jnp.int32, (rows, 128), 1) < S5_HALF
    for o in range(slabs):
        for tp in range(S5_CHUNK // 2):
            lane = 2 * o * S5_OCT + tp * 128
            pack_a = ys_ref[:, lane:lane + 128].astype(F32)
            pack_b = ys_ref[:, lane + S5_OCT:lane + S5_OCT + 128].astype(F32)
            ytok_ref[o, pl.ds(2 * tp, rows, stride=S5_CHUNK), :] = (
                jnp.where(low, pack_a, pltpu.roll(pack_b, S5_HALF, 1)))
            ytok_ref[o, pl.ds(2 * tp + 1, rows, stride=S5_CHUNK), :] = (
                jnp.where(low, pltpu.roll(pack_a, S5_HALF, 1), pack_b))
    z = _gelu_tanh(jnp.concatenate([ytok_ref[o] for o in range(slabs)], axis=1))
    z = z * _sigmoid(_dot(z.astype(BF16), wglu_ref[...]))
    z = z * _silu(f32(sg_ref))
    h = _sigmoid(f32(ms_ref)) * _dot(z.astype(BF16), swout_ref[...])

    e, et = e_ref[...], et_ref[...]
    y = yf_ref[...] + yb_ref[...]
    inv_n = 1.0 / RWKV_HEAD
    mean = _head_sum_bcast(y, e, et) * inv_n
    yc = y - mean
    var = _head_sum_bcast(yc * yc, e, et) * inv_n
    y = yc * lax.rsqrt(var + GN_EPS) * lnw_ref[...] + lnb_ref[...] + f32(bonus_ref)
    y = y * _silu(f32(rg_ref))
    h = h + _sigmoid(f32(mr_ref)) * _dot(y.astype(BF16), rwout_ref[...])

    out = _dot(h.astype(BF16), wo_ref[...])
    ms = jnp.mean(out * out, axis=-1, keepdims=True)
    o_ref[...] = x_ref[...] + out * lax.rsqrt(ms + RMS_EPS) * g_ref[...]


def _tail(y_s5, proj, yf, yb, bonus, x, p, tm):
    n = x.shape[0]
    row = lambda i: (i, 0)
    const = lambda i: (0, 0)
    half = pl.BlockSpec((tm, D_RWKV), row)
    full = pl.BlockSpec((tm, D_MODEL), row)

    def proj_cols(width, blk):
        return pl.BlockSpec((tm, width), lambda i: (i, blk))

    def resident(shape):
        return pl.BlockSpec(shape, const, pipeline_mode=pl.Buffered(1))

    return pl.pallas_call(
        _tail_kernel,
        out_shape=jax.ShapeDtypeStruct((n, D_MODEL), F32),
        grid=(n // tm,),
        in_specs=[pl.BlockSpec((tm // S5_CHUNK, S5_CHUNK * D_S5), row),
                  proj_cols(D_S5, PROJ_SG_BLK), proj_cols(D_MODEL, PROJ_MG_BLK),
                  half, half, half,
                  proj_cols(D_RWKV, PROJ_RG_BLK), proj_cols(D_MODEL, PROJ_MG_BLK + 1),
                  full,
                  resident((D_S5, D_S5)), resident((D_S5, D_MODEL)),
                  resident((1, D_RWKV)), resident((1, D_RWKV)),
                  resident((D_RWKV, 128)), resident((128, D_RWKV)),
                  resident((D_RWKV, D_MODEL)), resident((D_MODEL, D_MODEL)),
                  resident((1, D_MODEL))],
        out_specs=full,
        scratch_shapes=[pltpu.VMEM((D_S5 // 128, tm, 128), F32)],
        compiler_params=_cparams(("parallel",)),
        name="tail",
    )(y_s5, proj, proj, yf, yb, bonus, proj, proj, x,
      p['w_glu'], p['s5_w_out'], p['ln_w'], p['ln_b'], p['e'], p['et'], p['rw_w_out'], p['w_o'],
      p['post_g'])


def _pick(n, pref):
    while n % pref:
        pref //= 2
    return pref


def _prepare(pre_norm_g, post_norm_g, w_in, s5_lam_re, s5_lam_im, s5_log_dt, s5_b_re, s5_b_im,
             s5_c_re, s5_c_im, s5_d, s5_w_glu, s5_w_out, rw_mu, rw_w0, rw_w_up, rw_a0, rw_a_up,
             rw_k_k, rw_k_a, rw_r_k, rw_ln_w, rw_ln_b, rw_w_out, w_o):
    p = {}
    p['pre_g'] = pre_norm_g.astype(F32).reshape(1, D_MODEL)
    p['post_g'] = post_norm_g.astype(F32).reshape(1, D_MODEL)
    w = w_in
    rw_end = 2 * D_S5 + D_RW_IN
    p['w_u'] = w[:, :D_S5].astype(BF16)
    p['w_proj'] = jnp.concatenate(
        [w[:, 2 * D_S5:rw_end], jnp.zeros((D_MODEL, PROJ_RW_W - D_RW_IN), w.dtype),
         w[:, D_S5:2 * D_S5], w[:, rw_end:]], axis=1).astype(BF16)
    p['s5_m'], p['s5_p'], p['s5_q'], p['s5_a'] = _s5_weights(
        s5_lam_re, s5_lam_im, s5_log_dt, s5_b_re, s5_b_im, s5_c_re, s5_c_im, s5_d)
    p['w_glu'] = s5_w_glu.astype(BF16)
    p['s5_w_out'] = s5_w_out.astype(BF16)
    p['mu'] = rw_mu.astype(F32).reshape(1, D_RW_IN)
    zeros = jnp.zeros((LORA_RANK, D_RWKV), F32)

    def blockdiag(u):
        return jnp.concatenate([jnp.concatenate([u[0], zeros], axis=1),
                                jnp.concatenate([zeros, u[1]], axis=1)], axis=0).astype(BF16)

    p['wup'] = blockdiag(rw_w_up.astype(F32))
    p['aup'] = blockdiag(rw_a_up.astype(F32))
    p['w0'] = rw_w0.astype(F32).reshape(1, 2 * D_RWKV)
    p['a0'] = rw_a0.astype(F32).reshape(1, 2 * D_RWKV)
    p['k_k'] = rw_k_k.astype(F32).reshape(1, D_RWKV)
    p['k_a'] = rw_k_a.astype(F32).reshape(1, D_RWKV)
    p['r_k'] = rw_r_k.astype(F32).reshape(1, D_RWKV)
    p['ln_w'] = rw_ln_w.astype(F32).reshape(1, D_RWKV)
    p['ln_b'] = rw_ln_b.astype(F32).reshape(1, D_RWKV)
    p['rw_w_out'] = rw_w_out.astype(BF16)
    p['w_o'] = w_o.astype(BF16)
    e = _head_indicator()
    p['e'] = e
    p['et'] = e.T
    return p


def _s5_mix(x2, p, nseq):
    r = x2.shape[0]
    rb = _pick(r, 512)
    (y0,) = _s5_mm(x2, p['s5_m'], rb, 1, "s5_chunk_out")
    pfr, pfi, pbr, pbi = _s5_mm(x2, p['s5_p'], rb, 4, "s5_chunk_state")
    cb = _pick(r // nseq, 128)
    hfr, hfi, hbr, hbi = _s5_scan(p['s5_a'], pfr, pfi, pbr, pbi, nseq, cb, 1024)
    return _s5_state_out(y0, hfr, hfi, hbr, hbi, p['s5_q'], rb)


def _layer(x, p):
    bsz, seq, _ = x.shape
    n = bsz * seq
    x2 = x.reshape(n, D_MODEL)
    u_rows = _in_proj_u(x2, p['pre_g'], p['w_u'], _pick(n, 512))
    proj = _in_proj(x2, p['pre_g'], p['w_proj'], _pick(n, 1024))

    y_s5 = _s5_mix(u_rows, p, bsz)

    (r, kk, v, kdf, kdb, bf, bb, lwf, lwb, bonus) = _rw_pre(
        proj, p['mu'], p['wup'], p['aup'], p['w0'], p['a0'], p['k_k'], p['k_a'], p['r_k'],
        p['e'], p['et'], _pick(seq, 256), seq)
    yf, yb = _wkv(r, kk, v, kdf, bf, lwf, kdb, bb, lwb, bsz)

    out = _tail(y_s5, proj, yf, yb, bonus, x2, p, _pick(n, 256))
    return out.reshape(bsz, seq, D_MODEL)


def kernel(x_prompt, x_sample, pre_norm_g, post_norm_g, w_in, s5_lam_re, s5_lam_im, s5_log_dt, s5_b_re, s5_b_im, s5_c_re, s5_c_im, s5_d, s5_w_glu, s5_w_out, rw_mu, rw_w0, rw_w_up, rw_a0, rw_a_up, rw_k_k, rw_k_a, rw_r_k, rw_ln_w, rw_ln_b, rw_w_out, w_o):
    params = (pre_norm_g, post_norm_g, w_in, s5_lam_re, s5_lam_im, s5_log_dt, s5_b_re, s5_b_im,
              s5_c_re, s5_c_im, s5_d, s5_w_glu, s5_w_out, rw_mu, rw_w0, rw_w_up, rw_a0, rw_a_up,
              rw_k_k, rw_k_a, rw_r_k, rw_ln_w, rw_ln_b, rw_w_out, w_o)
    y_prompt, y_sample = x_prompt, x_sample
    for layer in range(w_in.shape[0]):
        p = _prepare(*[w[layer] for w in params])
        y_prompt = _layer(y_prompt, p)
        y_sample = _layer(y_sample, p)
    return (y_prompt, y_sample)
```

```python
import functools
import math

import jax
import jax.numpy as jnp
from jax import lax
from jax.experimental import pallas as pl
from jax.experimental.pallas import tpu as pltpu

F32 = jnp.float32
BF16 = jnp.bfloat16

D_MODEL = 2048
D_S5 = 1024
S5_GROUP = 16
S5_GROUPS = 64
S5_STATE = 64
D_RWKV = 1024
RWKV_HEAD = 64
RWKV_HEADS = 16
LORA_RANK = 64
D_RW_IN = 3 * D_RWKV + 4 * LORA_RANK
RMS_EPS = 1e-6
GN_EPS = 64e-5
L2_EPS = 1e-12

S5_CHUNK = 16
S5_OCT_GROUPS = 4
S5_OCT = S5_OCT_GROUPS * S5_CHUNK * S5_GROUP
S5_HALF = S5_OCT_GROUPS * S5_GROUP
WKV_CHUNK = 64
HEAD_PAIR = 2 * RWKV_HEAD
NEUMANN_STEPS = 5
WKV_PAIRS_PER_STEP = 8
SCAN_ROWS = 16
WKV_STREAMS_PER_BATCH = 2
WKV_SEQS = 2
HALO = 16

VMEM_LIMIT = 48 * 1024 * 1024


def _cparams(sem):
    return pltpu.CompilerParams(dimension_semantics=sem, vmem_limit_bytes=VMEM_LIMIT)


def _dot(a, b):
    return jnp.dot(a, b, preferred_element_type=F32)


def _dot_nt(a, b):
    return lax.dot_general(a, b, (((1,), (1,)), ((), ())), preferred_element_type=F32)


def _dot_tn(a, b):
    return lax.dot_general(a, b, (((0,), (0,)), ((), ())), preferred_element_type=F32)


def _sigmoid(x):
    return 1.0 / (1.0 + jnp.exp2(x * (-math.log2(math.e))))


def _silu(x):
    return x * _sigmoid(x)


def _gelu_tanh(x):
    c = math.sqrt(2.0 / math.pi)
    return 0.5 * x * (1.0 + jnp.tanh(c * (x + 0.044715 * (x * x * x))))


IN_TN = 1024
PROJ_RW_W = 4096
PROJ_W = PROJ_RW_W + D_S5 + D_RWKV + 2 * D_MODEL
PROJ_SG_BLK = PROJ_RW_W // D_S5
PROJ_RG_BLK = PROJ_SG_BLK + 1
PROJ_MG_BLK = (PROJ_RW_W + D_S5 + D_RWKV) // D_MODEL


def _rms_bf16(x, g):
    ms = jnp.mean(x * x, axis=-1, keepdims=True)
    return (x * lax.rsqrt(ms + RMS_EPS) * g).astype(BF16)


def _in_proj_u_kernel(x_ref, g_ref, w_ref, o_ref, ubuf_ref):
    res = _dot(_rms_bf16(x_ref[...], g_ref[...]), w_ref[...])
    rows = ubuf_ref.shape[1] // S5_CHUNK
    low = lax.broadcasted_iota(jnp.int32, (rows, 128), 1) < S5_HALF
    for o in range(ubuf_ref.shape[0]):
        ubuf_ref[o] = res[:, o * 128:(o + 1) * 128]
        for tp in range(S5_CHUNK // 2):
            te = ubuf_ref[o, pl.ds(2 * tp, rows, stride=S5_CHUNK), :]
            to = ubuf_ref[o, pl.ds(2 * tp + 1, rows, stride=S5_CHUNK), :]
            pack_a = jnp.where(low, te, pltpu.roll(to, S5_HALF, 1))
            pack_b = jnp.where(low, pltpu.roll(te, S5_HALF, 1), to)
            lane = 2 * o * S5_OCT + tp * 128
            o_ref[:, lane:lane + 128] = pack_a.astype(BF16)
            o_ref[:, lane + S5_OCT:lane + S5_OCT + 128] = pack_b.astype(BF16)


def _in_proj_u(x, g, w_u, tm):
    n, d = x.shape
    return pl.pallas_call(
        _in_proj_u_kernel,
        out_shape=jax.ShapeDtypeStruct((n // S5_CHUNK, D_S5 * S5_CHUNK), BF16),
        grid=(n // tm,),
        in_specs=[pl.BlockSpec((tm, d), lambda i: (i, 0)),
                  pl.BlockSpec((1, d), lambda i: (0, 0)),
                  pl.BlockSpec((d, D_S5), lambda i: (0, 0))],
        out_specs=pl.BlockSpec((tm // S5_CHUNK, D_S5 * S5_CHUNK), lambda i: (i, 0)),
        scratch_shapes=[pltpu.VMEM((D_S5 // 128, tm, 128), F32)],
        compiler_params=_cparams(("parallel",)),
        name="in_proj_u",
    )(x, g, w_u)


def _in_proj_kernel(x_ref, g_ref, w_ref, o_ref, hn_ref):
    @pl.when(pl.program_id(1) == 0)
    def _():
        hn_ref[...] = _rms_bf16(x_ref[...], g_ref[...])

    o_ref[...] = _dot(hn_ref[...], w_ref[...]).astype(BF16)


def _in_proj(x, g, w, tm):
    n, d = x.shape
    return pl.pallas_call(
        _in_proj_kernel,
        out_shape=jax.ShapeDtypeStruct((n, w.shape[1]), BF16),
        grid=(n // tm, w.shape[1] // IN_TN),
        in_specs=[pl.BlockSpec((tm, d), lambda i, j: (i, 0)),
                  pl.BlockSpec((1, d), lambda i, j: (0, 0)),
                  pl.BlockSpec((d, IN_TN), lambda i, j: (0, j))],
        out_specs=pl.BlockSpec((tm, IN_TN), lambda i, j: (i, j)),
        scratch_shapes=[pltpu.VMEM((tm, d), BF16)],
        compiler_params=_cparams(("parallel", "arbitrary")),
        name="in_proj",
    )(x, g, w)


def _s5_weights(lam_re, lam_im, log_dt, b_re, b_im, c_re, c_im, d):
    t = S5_CHUNK
    g, p, h = S5_GROUPS, S5_STATE, S5_GROUP
    hp = lax.Precision.HIGHEST
    lam = lax.complex(lam_re.astype(F32), lam_im.astype(F32))
    dt = jnp.exp(log_dt.astype(F32))[..., None]
    lam_dt = lam * dt
    lam_bar = jnp.exp(lam_dt)
    bbar = ((lam_bar - 1.0) / lam)[..., None] * lax.complex(b_re.astype(F32), b_im.astype(F32))
    c = lax.complex(c_re.astype(F32), c_im.astype(F32))
    steps = jnp.arange(t + 1, dtype=F32)
    pw = jnp.exp(lam_dt[None] * steps[:, None, None, None])

    kern = jnp.einsum('dghp,ldgp,dgpj->dglhj', c, pw[:t], bbar, precision=hp).real
    tt = jnp.arange(t)
    k0 = kern[0][:, 0] + kern[1][:, 0] + jnp.eye(h, dtype=F32) * d.astype(F32).reshape(g, h)[:, :, None]
    by_lag = jnp.concatenate([kern[1][:, :0:-1], k0[:, None], kern[0][:, 1:]], axis=1)
    wide = by_lag.transpose(0, 3, 1, 2).reshape(g, h, (2 * t - 1) * h).astype(BF16)
    m = jnp.stack([wide[:, :, (t - 1 - ti) * h:(t - 1 - ti) * h + t * h] for ti in range(t)], axis=1)
    m = m.reshape(g, t * h, t * h)

    pf = pw[t - 1 - tt, 0][..., None] * bbar[0][None]
    pb = pw[tt, 1][..., None] * bbar[1][None]

    def _p_mat(x):
        return x.transpose(1, 0, 3, 2).reshape(g, t * h, p)

    p_parts = [_p_mat(pf.real), _p_mat(pf.imag), _p_mat(pb.real), _p_mat(pb.imag)]

    cf = c[0][None] * pw[tt + 1, 0][:, :, None, :]
    cb = c[1][None] * pw[t - tt, 1][:, :, None, :]

    def _q_mat(x):
        return x.transpose(1, 3, 0, 2).reshape(g, p, t * h)

    q_parts = [_q_mat(cf.real), _q_mat(-cf.imag), _q_mat(cb.real), _q_mat(-cb.imag)]

    k8 = S5_OCT_GROUPS
    no = g // k8
    gi = jnp.arange(k8)

    def _spread(width):
        c = jnp.arange(width)
        tgt = (gi[:, None] * width + c[None, :])[:, :, None]
        return (tgt == jnp.arange(k8 * width)[None, None, :]).astype(BF16)

    def _spread_tok():
        u, hh = jnp.arange(t * h) // h, jnp.arange(t * h) % h
        tgt = (u[None, :] * (k8 * h) + gi[:, None] * h + hh[None, :])[:, :, None]
        return (tgt == jnp.arange(S5_OCT)[None, None, :]).astype(BF16)

    def _cols(x, spread):
        x = x.astype(BF16).reshape(no, k8, x.shape[1], x.shape[2])
        return jnp.einsum('ogrc,gcd->ogrd', x, spread, preferred_element_type=BF16)

    def _rows_tok(x):
        c = x.shape[-1]
        return x.reshape(no, k8, t, h, c).transpose(0, 2, 1, 3, 4).reshape(no, S5_OCT, c)

    sp_tok, sp_state = _spread_tok(), _spread(p)
    w_p = jnp.concatenate([_rows_tok(_cols(x, sp_state)) for x in p_parts], axis=2)
    w_mq = jnp.concatenate(
        [_rows_tok(_cols(m, sp_tok))] + [_cols(x, sp_tok).reshape(no, k8 * p, S5_OCT) for x in q_parts],
        axis=1)
    at = pw[t]
    a_rows = jnp.stack([at[0].real, at[0].imag, at[1].real, at[1].imag]).reshape(4, g * p)
    return w_p, w_mq, a_rows


def _s5_mm_kernel(x_ref, w_ref, *o_refs):
    res = _dot(x_ref[...], w_ref[0])
    n = res.shape[1] // len(o_refs)
    for k, ref in enumerate(o_refs):
        ref[...] = res[:, n * k:n * (k + 1)].astype(ref.dtype)


def _s5_mm(x2, w, rb, nout, name):
    r = x2.shape[0]
    no, _, c = w.shape
    cw = c // nout
    return pl.pallas_call(
        _s5_mm_kernel,
        out_shape=tuple(jax.ShapeDtypeStruct((r, no * cw), BF16) for _ in range(nout)),
        grid=(no, r // rb),
        in_specs=[pl.BlockSpec((rb, S5_OCT), lambda o, i: (i, o)),
                  pl.BlockSpec((1, S5_OCT, c), lambda o, i: (o, 0, 0))],
        out_specs=tuple(pl.BlockSpec((rb, cw), lambda o, i: (i, o)) for _ in range(nout)),
        compiler_params=_cparams(("parallel", "arbitrary")),
        name=name,
    )(x2, w)


def _s5_scan_kernel(a_ref, pfr_ref, pfi_ref, pbr_ref, pbi_ref,
                    hfr_ref, hfi_ref, hbr_ref, hbi_ref, carry_ref, *, rows):
    @pl.when(pl.program_id(2) == 0)
    def _():
        carry_ref[...] = jnp.zeros_like(carry_ref)

    afr, afi = a_ref[0:1, :], a_ref[1:2, :]
    abr, abi = a_ref[2:3, :], a_ref[3:4, :]

    nt = SCAN_ROWS

    def tile(i, carry):
        hfr, hfi, hbr, hbi = carry
        r0 = pl.multiple_of(i * nt, nt)
        pfr, pfi = pfr_ref[pl.ds(r0, nt), :].astype(F32), pfi_ref[pl.ds(r0, nt), :].astype(F32)
        rb0 = pl.multiple_of(rows - nt - i * nt, nt)
        pbr, pbi = pbr_ref[pl.ds(rb0, nt), :].astype(F32), pbi_ref[pl.ds(rb0, nt), :].astype(F32)
        of_r, of_i, ob_r, ob_i = [], [], [None] * nt, [None] * nt
        for s in range(nt):
            of_r.append(hfr)
            of_i.append(hfi)
            hfr, hfi = (afr * hfr - afi * hfi + pfr[s:s + 1, :],
                        afr * hfi + afi * hfr + pfi[s:s + 1, :])
            sb = nt - 1 - s
            ob_r[sb] = hbr
            ob_i[sb] = hbi
            hbr, hbi = (abr * hbr - abi * hbi + pbr[sb:sb + 1, :],
                        abr * hbi + abi * hbr + pbi[sb:sb + 1, :])
        hfr_ref[pl.ds(r0, nt), :] = jnp.concatenate(of_r, axis=0).astype(BF16)
        hfi_ref[pl.ds(r0, nt), :] = jnp.concatenate(of_i, axis=0).astype(BF16)
        hbr_ref[pl.ds(rb0, nt), :] = jnp.concatenate(ob_r, axis=0).astype(BF16)
        hbi_ref[pl.ds(rb0, nt), :] = jnp.concatenate(ob_i, axis=0).astype(BF16)
        return hfr, hfi, hbr, hbi

    init = tuple(carry_ref[k:k + 1, :] for k in range(4))
    out = lax.fori_loop(0, rows // nt, tile, init)
    for k in range(4):
        carry_ref[k:k + 1, :] = out[k]


def _s5_scan(a_rows, pfr, pfi, pbr, pbi, nseq, cb, lt):
    r, w = pfr.shape
    nblk = r // nseq // cb
    fspec = pl.BlockSpec((cb, lt), lambda b, j, i: (b * nblk + i, j))
    bspec = pl.BlockSpec((cb, lt), lambda b, j, i: (b * nblk + nblk - 1 - i, j))
    st = jax.ShapeDtypeStruct((r, w), BF16)
    return pl.pallas_call(
        functools.partial(_s5_scan_kernel, rows=cb),
        out_shape=(st, st, st, st),
        grid=(nseq, w // lt, nblk),
        in_specs=[pl.BlockSpec((4, lt), lambda b, j, i: (0, j)), fspec, fspec, bspec, bspec],
        out_specs=(fspec, fspec, bspec, bspec),
        scratch_shapes=[pltpu.VMEM((4, lt), F32)],
        compiler_params=_cparams(("parallel", "parallel", "arbitrary")),
        name="s5_scan",
    )(a_rows, pfr, pfi, pbr, pbi)


def _s5_out_kernel(x_ref, hfr_ref, hfi_ref, hbr_ref, hbi_ref, w_ref, y_ref):
    xh = jnp.concatenate([x_ref[...], hfr_ref[...], hfi_ref[...], hbr_ref[...], hbi_ref[...]], axis=1)
    y_ref[...] = _dot(xh, w_ref[0]).astype(BF16)


def _s5_out(x2, hfr, hfi, hbr, hbi, w_mq, rb):
    r = x2.shape[0]
    no, c, _ = w_mq.shape
    hspec = pl.BlockSpec((rb, (c - S5_OCT) // 4), lambda o, i: (i, o))
    yspec = pl.BlockSpec((rb, S5_OCT), lambda o, i: (i, o))
    return pl.pallas_call(
        _s5_out_kernel,
        out_shape=jax.ShapeDtypeStruct(x2.shape, BF16),
        grid=(no, r // rb),
        in_specs=[yspec, hspec, hspec, hspec, hspec,
                  pl.BlockSpec((1, c, S5_OCT), lambda o, i: (o, 0, 0))],
        out_specs=yspec,
        compiler_params=_cparams(("parallel", "arbitrary")),
        name="s5_out",
    )(x2, hfr, hfi, hbr, hbi, w_mq)


def _head_indicator():
    lane = lax.broadcasted_iota(jnp.int32, (D_RWKV, 128), 0) // RWKV_HEAD
    col = lax.broadcasted_iota(jnp.int32, (D_RWKV, 128), 1)
    return (lane == col).astype(BF16)


def _head_sum_bcast(x, e, et):
    return _dot(_dot(x.astype(BF16), e).astype(BF16), et)


def _rw_pre_kernel(x_ref, prev_ref, next_ref, mu_ref, wup_ref, aup_ref, w0_ref, a0_ref,
                   kk_ref_p, ka_ref_p, rk_ref_p, e_ref, et_ref,
                   r_ref, kk_ref, v_ref, kdf_ref, kdb_ref, bf_ref, bb_ref, lwf_ref, lwb_ref,
                   bonus_ref, *, tm, seq):
    i = pl.program_id(0)
    xb = x_ref[...]
    first = (i * tm) % seq == 0
    last = ((i + 1) * tm) % seq == 0
    zero = jnp.zeros((HALO, xb.shape[1]), BF16)
    x_ext = jnp.concatenate([jnp.where(first, zero, prev_ref[...]), xb,
                             jnp.where(last, zero, next_ref[...])], axis=0)
    ri = lax.broadcasted_iota(jnp.int32, (tm, tm + 2 * HALO), 0) + HALO
    ci = lax.broadcasted_iota(jnp.int32, (tm, tm + 2 * HALO), 1)
    band = ((ci == ri - 1) | (ci == ri + 1)).astype(BF16)
    x = xb.astype(F32)
    z = x + (0.5 * _dot(band, x_ext) - x) * mu_ref[...]

    r = z[:, :D_RWKV]
    k = z[:, D_RWKV:2 * D_RWKV]
    v = z[:, 2 * D_RWKV:3 * D_RWKV]
    xw = z[:, 3 * D_RWKV:3 * D_RWKV + 2 * LORA_RANK]
    xa = z[:, 3 * D_RWKV + 2 * LORA_RANK:]
    e, et = e_ref[...], et_ref[...]

    kk = k * kk_ref_p[...]
    kk = kk * lax.rsqrt(_head_sum_bcast(kk * kk, e, et) + L2_EPS)
    bonus_ref[...] = (_head_sum_bcast(r * k * rk_ref_p[...], e, et) * v).astype(BF16)
    r_ref[...] = r.astype(BF16)
    kk_ref[...] = kk.astype(BF16)
    v_ref[...] = v.astype(BF16)

    wpre = _dot(jnp.tanh(xw).astype(BF16), wup_ref[...]) + w0_ref[...]
    apre = _dot(xa.astype(BF16), aup_ref[...]) + a0_ref[...]
    lw = -math.exp(-0.5) * _sigmoid(wpre)
    a = _sigmoid(apre)
    ka = ka_ref_p[...]
    lwf_ref[...] = lw[:, :D_RWKV]
    lwb_ref[...] = lw[:, D_RWKV:]
    af, ab = a[:, :D_RWKV], a[:, D_RWKV:]
    kdf_ref[...] = (k * (1.0 + (af - 1.0) * ka)).astype(BF16)
    kdb_ref[...] = (k * (1.0 + (ab - 1.0) * ka)).astype(BF16)
    bf_ref[...] = (kk * af).astype(BF16)
    bb_ref[...] = (kk * ab).astype(BF16)


def _rw_pre(rw_in, mu, wup, aup, w0, a0, k_k, k_a, r_k, e, et, tm, seq):
    n = rw_in.shape[0]
    w = D_RW_IN
    nbh = n // HALO
    th = tm // HALO
    row = lambda i: (i, 0)
    const = lambda i: (0, 0)
    ospec = pl.BlockSpec((tm, D_RWKV), row)
    sb = jax.ShapeDtypeStruct((n, D_RWKV), BF16)
    sf = jax.ShapeDtypeStruct((n, D_RWKV), F32)
    return pl.pallas_call(
        functools.partial(_rw_pre_kernel, tm=tm, seq=seq),
        out_shape=(sb,) * 7 + (sf, sf, sb),
        grid=(n // tm,),
        in_specs=[pl.BlockSpec((tm, w), row),
                  pl.BlockSpec((HALO, w), lambda i: (jnp.maximum(i * th - 1, 0), 0)),
                  pl.BlockSpec((HALO, w), lambda i: (jnp.minimum((i + 1) * th, nbh - 1), 0)),
                  pl.BlockSpec((1, w), const),
                  pl.BlockSpec((2 * LORA_RANK, 2 * D_RWKV), const),
                  pl.BlockSpec((2 * LORA_RANK, 2 * D_RWKV), const),
                  pl.BlockSpec((1, 2 * D_RWKV), const),
                  pl.BlockSpec((1, 2 * D_RWKV), const),
                  pl.BlockSpec((1, D_RWKV), const),
                  pl.BlockSpec((1, D_RWKV), const),
                  pl.BlockSpec((1, D_RWKV), const),
                  pl.BlockSpec((D_RWKV, 128), const),
                  pl.BlockSpec((128, D_RWKV), const)],
        out_specs=(ospec,) * 10,
        compiler_params=_cparams(("parallel",)),
        name="rw_pre",
    )(rw_in, rw_in, rw_in, mu, wup, aup, w0, a0, k_k, k_a, r_k, e, et)


def _wkv_kernel(rf_ref, kkf_ref, vf_ref, kdf_ref, bf_ref, lwf_ref,
                rb_ref, kkb_ref, vb_ref, kdb_ref, bb_ref, lwb_ref, yf_ref, yb_ref, s_ref):
    t = WKV_CHUNK
    hd = RWKV_HEAD
    w = HEAD_PAIR

    @pl.when(pl.program_id(1) == 0)
    def _():
        s_ref[...] = jnp.zeros_like(s_ref)

    def iota(shape, dim):
        return lax.broadcasted_iota(jnp.int32, shape, dim)

    rr, cc = iota((t, t), 0), iota((t, t), 1)
    rw_, lane = iota((t, w), 0), iota((t, w), 1)
    col = lane % hd
    m0 = lane < hd
    eye_f = (rw_ == col).astype(F32)
    m0_2t = iota((2 * t, w), 1) < hd
    eye_w = iota((w, w), 0) == iota((w, w), 1)
    bd_mask = (iota((w, w), 0) < hd) == (iota((w, w), 1) < hd)

    def direction(rev, refs, y_ref):
        if rev:
            incl, strict, tri = rw_ <= col, rw_ < col, rr <= cc
        else:
            incl, strict, tri = rw_ >= col, rw_ > col, rr >= cc
        return dict(rev=int(rev), refs=refs, y_ref=y_ref, incl=incl, strict=strict,
                    tri=tri.astype(BF16),
                    mask_2t=jnp.concatenate([strict, incl], axis=0), t_last=0 if rev else t - 1)

    dirs = (direction(False, (rf_ref, kkf_ref, vf_ref, kdf_ref, bf_ref, lwf_ref), yf_ref),
            direction(True, (rb_ref, kkb_ref, vb_ref, kdb_ref, bb_ref, lwb_ref), yb_ref))

    def bdiag(x):
        return jnp.concatenate([jnp.where(m0, x, 0.0), jnp.where(m0, 0.0, x)], axis=0)

    def group(gi, carry):
        streams = [(dm, sq) for dm in dirs for sq in range(rf_ref.shape[0])]
        for k in range(0, len(streams), WKV_STREAMS_PER_BATCH):
            batch(streams[k:k + WKV_STREAMS_PER_BATCH], gi)
        return carry

    def batch(streams, gi):
        hv = []
        for dm, sq, j in [(dm, sq, j) for dm, sq in streams for j in range(WKV_PAIRS_PER_STEP)]:
            hp = gi * WKV_PAIRS_PER_STEP + j
            sl = pl.ds(pl.multiple_of(hp * HEAD_PAIR, HEAD_PAIR), HEAD_PAIR)
            r, kk, v, kd, beta = (ref[sq, :, sl].astype(F32) for ref in dm['refs'][:5])
            lw = dm['refs'][5][sq, :, sl]
            t_last = dm['t_last']
            lcum = lw
            for s in (1, 2, 4, 8, 16, 32):
                if dm['rev']:
                    lcum = lcum + jnp.where(rw_ < t - s, pltpu.roll(lcum, t - s, 0), 0.0)
                else:
                    lcum = lcum + jnp.where(rw_ >= s, pltpu.roll(lcum, s, 0), 0.0)
            lcum_x = lcum - lw
            cref = lcum[t // 2:t // 2 + 1, :]
            ltot = lcum[t_last:t_last + 1, :]
            e1 = jnp.exp(lcum - cref)
            e1x = jnp.exp(lcum_x - cref)
            e2 = jnp.exp(cref - lcum)
            ec = jnp.exp(cref)
            ewt = jnp.exp(ltot - cref)
            wtot = jnp.exp(ltot)
            r_t = r * e1
            a_t = -kk * e1x
            b_t = beta * e2
            k_t = kd * e2
            a_0 = a_t * ec
            r_0 = r_t * ec
            b_h = b_t * ewt
            k_h = k_t * ewt
            hv.append(dict(
                dm=dm, sq=sq, hp=hp, sl=sl, wtot=wtot, a_0=a_0, r_0=r_0,
                lm=jnp.concatenate([a_t, r_t], axis=0),
                rm=jnp.concatenate([b_t, k_t], axis=0).astype(BF16),
                bk=jnp.concatenate([b_h, k_h], axis=0).astype(BF16), v=v))
        for h in hv:
            lm2 = jnp.concatenate([jnp.where(m0_2t, h['lm'], 0.0), jnp.where(m0_2t, 0.0, h['lm'])], axis=0)
            g = _dot_nt(lm2.astype(BF16), h['rm'])
            g0, g1 = g[:2 * t], pltpu.roll(g[2 * t:], hd, 1)
            own = jnp.where(m0_2t, g0, g1)
            oth = jnp.where(m0_2t, g1, g0)
            h['a_ab'] = jnp.where(h['dm']['strict'], own[:t], 0.0)
            h['a_rb'] = jnp.where(h['dm']['incl'], own[t:], 0.0).astype(BF16)
            h['a_k'] = jnp.where(h['dm']['mask_2t'], oth, 0.0).astype(BF16)
        for h in hv:
            v = h['v']
            h['inv'] = eye_f + h['a_ab']
            h['nk'] = _dot(h['a_ab'].astype(BF16), bdiag(h['a_ab']).astype(BF16))
            v_rows = jnp.concatenate([jnp.where(m0, 0.0, v), jnp.where(m0, v, 0.0)], axis=0)
            h['av'] = _dot(h['a_k'], v_rows.astype(BF16))
        for s in range(1, NEUMANN_STEPS + 1):
            for h in hv:
                nk_bd = bdiag(h['nk']).astype(BF16)
                if s < NEUMANN_STEPS:
                    nx = _dot(jnp.concatenate([h['nk'], h['inv']], axis=0).astype(BF16), nk_bd)
                    h['nk'] = nx[:t]
                    h['inv'] = h['inv'] + nx[t:]
                else:
                    h['inv'] = h['inv'] + _dot(h['inv'].astype(BF16), nk_bd)
        for h in hv:
            rhs = jnp.concatenate([bdiag(h['a_0']), bdiag(h['av'][:t])], axis=1)
            h['pp'] = _dot(h['inv'].astype(BF16), rhs.astype(BF16))
        for h in hv:
            pp = h['pp']
            rhs = jnp.concatenate([bdiag(pp[:, :w]), bdiag(pp[:, w:])], axis=1)
            h['qq'] = (_dot(h['a_rb'], rhs.astype(BF16))
                       + jnp.concatenate([h['r_0'], h['av'][t:]], axis=1))
            pv = jnp.concatenate([pp, jnp.concatenate([jnp.zeros_like(h['v']), h['v']], axis=1)], axis=0)
            h['mn'] = _dot_tn(h['bk'], pv.astype(BF16))
        for h in hv:
            sidx = (h['dm']['rev'], h['sq'], h['hp'])
            st = s_ref[sidx]
            lhs = jnp.concatenate([h['qq'][:, :w], jnp.where(bd_mask, h['mn'][:, :w], 0.0)], axis=0)
            res = _dot(lhs.astype(BF16), st.astype(BF16))
            h['dm']['y_ref'][h['sq'], :, h['sl']] = res[:t] + h['qq'][:, w:]
            wcol = jnp.sum(jnp.where(eye_w, jnp.broadcast_to(h['wtot'], (w, w)), 0.0),
                           axis=1, keepdims=True)
            s_ref[sidx] = wcol * st + res[t:] + jnp.where(bd_mask, h['mn'][:, w:], 0.0)

    lax.fori_loop(0, RWKV_HEADS // (2 * WKV_PAIRS_PER_STEP), group, 0)


def _wkv(r, kk, v, kdf, bf, lwf, kdb, bb, lwb, nseq):
    n = r.shape[0]
    t = WKV_CHUNK
    seq = n // nseq
    nc = seq // t
    ns = WKV_SEQS if nseq % WKV_SEQS == 0 else 1
    fwd = pl.BlockSpec((ns, t, D_RWKV), lambda b, c: (b, c, 0))
    bwd = pl.BlockSpec((ns, t, D_RWKV), lambda b, c: (b, nc - 1 - c, 0))
    shape = jax.ShapeDtypeStruct((nseq, seq, D_RWKV), F32)
    yf, yb = pl.pallas_call(
        _wkv_kernel,
        out_shape=(shape, shape),
        grid=(nseq // ns, nc),
        in_specs=[fwd] * 6 + [bwd] * 6,
        out_specs=(fwd, bwd),
        scratch_shapes=[pltpu.VMEM((2, ns, RWKV_HEADS // 2, HEAD_PAIR, HEAD_PAIR), F32)],
        compiler_params=_cparams(("parallel", "arbitrary")),
        name="wkv",
    )(*[a.reshape(nseq, seq, D_RWKV) for a in (r, kk, v, kdf, bf, lwf, r, kk, v, kdb, bb, lwb)])
    return yf.reshape(n, D_RWKV), yb.reshape(n, D_RWKV)


def _tail_kernel(ys_ref, sg_ref, ms_ref, yf_ref, yb_ref, bonus_ref, rg_ref, mr_ref, x_ref,
                 wglu_ref, swout_ref, lnw_ref, lnb_ref, e_ref, et_ref, rwout_ref, wo_ref, g_ref,
                 o_ref, ytok_ref):
    f32 = lambda ref: ref[...].astype(F32)
    rows = ys_ref.shape[0]
    slabs = ytok_ref.shape[0]
    low = lax.broadcasted_iota(jnp.int32, (rows, 128), 1) < S5_HALF
    for o in range(slabs):
        for tp in range(S5_CHUNK // 2):
            lane = 2 * o * S5_OCT + tp * 128
            pack_a = ys_ref[:, lane:lane + 128].astype(F32)
            pack_b = ys_ref[:, lane + S5_OCT:lane + S5_OCT + 128].astype(F32)
            ytok_ref[o, pl.ds(2 * tp, rows, stride=S5_CHUNK), :] = (
                jnp.where(low, pack_a, pltpu.roll(pack_b, S5_HALF, 1)))
            ytok_ref[o, pl.ds(2 * tp + 1, rows, stride=S5_CHUNK), :] = (
                jnp.where(low, pltpu.roll(pack_a, S5_HALF, 1), pack_b))
    z = _gelu_tanh(jnp.concatenate([ytok_ref[o] for o in range(slabs)], axis=1))
    z = z * _sigmoid(_dot(z.astype(BF16), wglu_ref[...]))
    z = z * _silu(f32(sg_ref))
    h = _sigmoid(f32(ms_ref)) * _dot(z.astype(BF16), swout_ref[...])

    e, et = e_ref[...], et_ref[...]
    y = yf_ref[...] + yb_ref[...]
    inv_n = 1.0 / RWKV_HEAD
    mean = _head_sum_bcast(y, e, et) * inv_n
    yc = y - mean
    var = _head_sum_bcast(yc * yc, e, et) * inv_n
    y = yc * lax.rsqrt(var + GN_EPS) * lnw_ref[...] + lnb_ref[...] + f32(bonus_ref)
    y = y * _silu(f32(rg_ref))
    h = h + _sigmoid(f32(mr_ref)) * _dot(y.astype(BF16), rwout_ref[...])

    out = _dot(h.astype(BF16), wo_ref[...])
    ms = jnp.mean(out * out, axis=-1, keepdims=True)
    o_ref[...] = x_ref[...] + out * lax.rsqrt(ms + RMS_EPS) * g_ref[...]


def _tail(y_s5, proj, yf, yb, bonus, x, p, tm):
    n = x.shape[0]
    row = lambda i: (i, 0)
    const = lambda i: (0, 0)
    half = pl.BlockSpec((tm, D_RWKV), row)
    full = pl.BlockSpec((tm, D_MODEL), row)

    def proj_cols(width, blk):
        return pl.BlockSpec((tm, width), lambda i: (i, blk))

    def resident(shape):
        return pl.BlockSpec(shape, const, pipeline_mode=pl.Buffered(1))

    return pl.pallas_call(
        _tail_kernel,
        out_shape=jax.ShapeDtypeStruct((n, D_MODEL), F32),
        grid=(n // tm,),
        in_specs=[pl.BlockSpec((tm // S5_CHUNK, S5_CHUNK * D_S5), row),
                  proj_cols(D_S5, PROJ_SG_BLK), proj_cols(D_MODEL, PROJ_MG_BLK),
                  half, half, half,
                  proj_cols(D_RWKV, PROJ_RG_BLK), proj_cols(D_MODEL, PROJ_MG_BLK + 1),
                  full,
                  resident((D_S5, D_S5)), resident((D_S5, D_MODEL)),
                  resident((1, D_RWKV)), resident((1, D_RWKV)),
                  resident((D_RWKV, 128)), resident((128, D_RWKV)),
                  resident((D_RWKV, D_MODEL)), resident((D_MODEL, D_MODEL)),
                  resident((1, D_MODEL))],
        out_specs=full,
        scratch_shapes=[pltpu.VMEM((D_S5 // 128, tm, 128), F32)],
        compiler_params=_cparams(("parallel",)),
        name="tail",
    )(y_s5, proj, proj, yf, yb, bonus, proj, proj, x,
      p['w_glu'], p['s5_w_out'], p['ln_w'], p['ln_b'], p['e'], p['et'], p['rw_w_out'], p['w_o'],
      p['post_g'])


def _pick(n, pref):
    while n % pref:
        pref //= 2
    return pref


def _prepare(pre_norm_g, post_norm_g, w_in, s5_lam_re, s5_lam_im, s5_log_dt, s5_b_re, s5_b_im,
             s5_c_re, s5_c_im, s5_d, s5_w_glu, s5_w_out, rw_mu, rw_w0, rw_w_up, rw_a0, rw_a_up,
             rw_k_k, rw_k_a, rw_r_k, rw_ln_w, rw_ln_b, rw_w_out, w_o):
    p = {}
    p['pre_g'] = pre_norm_g.astype(F32).reshape(1, D_MODEL)
    p['post_g'] = post_norm_g.astype(F32).reshape(1, D_MODEL)
    w = w_in
    rw_end = 2 * D_S5 + D_RW_IN
    p['w_u'] = w[:, :D_S5].astype(BF16)
    p['w_proj'] = jnp.concatenate(
        [w[:, 2 * D_S5:rw_end], jnp.zeros((D_MODEL, PROJ_RW_W - D_RW_IN), w.dtype),
         w[:, D_S5:2 * D_S5], w[:, rw_end:]], axis=1).astype(BF16)
    p['s5_p'], p['s5_mq'], p['s5_a'] = _s5_weights(
        s5_lam_re, s5_lam_im, s5_log_dt, s5_b_re, s5_b_im, s5_c_re, s5_c_im, s5_d)
    p['w_glu'] = s5_w_glu.astype(BF16)
    p['s5_w_out'] = s5_w_out.astype(BF16)
    p['mu'] = rw_mu.astype(F32).reshape(1, D_RW_IN)
    zeros = jnp.zeros((LORA_RANK, D_RWKV), F32)

    def blockdiag(u):
        return jnp.concatenate([jnp.concatenate([u[0], zeros], axis=1),
                                jnp.concatenate([zeros, u[1]], axis=1)], axis=0).astype(BF16)

    p['wup'] = blockdiag(rw_w_up.astype(F32))
    p['aup'] = blockdiag(rw_a_up.astype(F32))
    p['w0'] = rw_w0.astype(F32).reshape(1, 2 * D_RWKV)
    p['a0'] = rw_a0.astype(F32).reshape(1, 2 * D_RWKV)
    p['k_k'] = rw_k_k.astype(F32).reshape(1, D_RWKV)
    p['k_a'] = rw_k_a.astype(F32).reshape(1, D_RWKV)
    p['r_k'] = rw_r_k.astype(F32).reshape(1, D_RWKV)
    p['ln_w'] = rw_ln_w.astype(F32).reshape(1, D_RWKV)
    p['ln_b'] = rw_ln_b.astype(F32).reshape(1, D_RWKV)
    p['rw_w_out'] = rw_w_out.astype(BF16)
    p['w_o'] = w_o.astype(BF16)
    e = _head_indicator()
    p['e'] = e
    p['et'] = e.T
    return p


def _s5_mix(x2, p, nseq):
    r = x2.shape[0]
    rb = _pick(r, 512)
    pfr, pfi, pbr, pbi = _s5_mm(x2, p['s5_p'], rb, 4, "s5_chunk_state")
    cb = _pick(r // nseq, 128)
    hfr, hfi, hbr, hbi = _s5_scan(p['s5_a'], pfr, pfi, pbr, pbi, nseq, cb, 1024)
    return _s5_out(x2, hfr, hfi, hbr, hbi, p['s5_mq'], rb)


def _layer(x, p):
    bsz, seq, _ = x.shape
    n = bsz * seq
    x2 = x.reshape(n, D_MODEL)
    u_rows = _in_proj_u(x2, p['pre_g'], p['w_u'], _pick(n, 512))
    proj = _in_proj(x2, p['pre_g'], p['w_proj'], _pick(n, 1024))

    y_s5 = _s5_mix(u_rows, p, bsz)

    (r, kk, v, kdf, kdb, bf, bb, lwf, lwb, bonus) = _rw_pre(
        proj, p['mu'], p['wup'], p['aup'], p['w0'], p['a0'], p['k_k'], p['k_a'], p['r_k'],
        p['e'], p['et'], _pick(seq, 256), seq)
    yf, yb = _wkv(r, kk, v, kdf, bf, lwf, kdb, bb, lwb, bsz)

    out = _tail(y_s5, proj, yf, yb, bonus, x2, p, _pick(n, 256))
    return out.reshape(bsz, seq, D_MODEL)


def kernel(x_prompt, x_sample, pre_norm_g, post_norm_g, w_in, s5_lam_re, s5_lam_im, s5_log_dt, s5_b_re, s5_b_im, s5_c_re, s5_c_im, s5_d, s5_w_glu, s5_w_out, rw_mu, rw_w0, rw_w_up, rw_a0, rw_a_up, rw_k_k, rw_k_a, rw_r_k, rw_ln_w, rw_ln_b, rw_w_out, w_o):
    params = (pre_norm_g, post_norm_g, w_in, s5_lam_re, s5_lam_im, s5_log_dt, s5_b_re, s5_b_im,
              s5_c_re, s5_c_im, s5_d, s5_w_glu, s5_w_out, rw_mu, rw_w0, rw_w_up, rw_a0, rw_a_up,
              rw_k_k, rw_k_a, rw_r_k, rw_ln_w, rw_ln_b, rw_w_out, w_o)
    y_prompt, y_sample = x_prompt, x_sample
    for layer in range(w_in.shape[0]):
        p = _prepare(*[w[layer] for w in params])
        y_prompt = _layer(y_prompt, p)
        y_sample = _layer(y_sample, p)
    return (y_prompt, y_sample)
```

```python
import functools
import math

import jax
import jax.numpy as jnp
from jax import lax
from jax.experimental import pallas as pl
from jax.experimental.pallas import tpu as pltpu

F32 = jnp.float32
BF16 = jnp.bfloat16

D_MODEL = 2048
D_S5 = 1024
S5_GROUP = 16
S5_GROUPS = 64
S5_STATE = 64
D_RWKV = 1024
RWKV_HEAD = 64
RWKV_HEADS = 16
LORA_RANK = 64
D_RW_IN = 3 * D_RWKV + 4 * LORA_RANK
RMS_EPS = 1e-6
GN_EPS = 64e-5
L2_EPS = 1e-12

LANES = 128
S5_CHUNK = 16
S5_PACK_GROUPS = 4
S5_PACK = S5_PACK_GROUPS * S5_CHUNK * S5_GROUP
S5_HALF = S5_PACK_GROUPS * S5_GROUP
WKV_CHUNK = 64
HEAD_PAIR = 2 * RWKV_HEAD
NEUMANN_STEPS = 5
WKV_PAIRS_PER_STEP = 8
SCAN_ROWS = 16
WKV_STREAMS_PER_BATCH = 2
WKV_SEQS = 2
HALO = 16

VMEM_LIMIT = 48 * 1024 * 1024


def _cparams(sem):
    return pltpu.CompilerParams(dimension_semantics=sem, vmem_limit_bytes=VMEM_LIMIT)


def _dot(a, b):
    return jnp.dot(a, b, preferred_element_type=F32)


def _dot_nt(a, b):
    return lax.dot_general(a, b, (((1,), (1,)), ((), ())), preferred_element_type=F32)


def _dot_tn(a, b):
    return lax.dot_general(a, b, (((0,), (0,)), ((), ())), preferred_element_type=F32)


def _sigmoid(x):
    return 1.0 / (1.0 + jnp.exp2(x * (-math.log2(math.e))))


def _silu(x):
    return x * _sigmoid(x)


def _gelu_tanh(x):
    c = math.sqrt(2.0 / math.pi)
    return 0.5 * x * (1.0 + jnp.tanh(c * (x + 0.044715 * (x * x * x))))


IN_TN = 1024
PROJ_RW_W = 4096
PROJ_W = PROJ_RW_W + D_S5 + D_RWKV + 2 * D_MODEL
PROJ_SG_BLK = PROJ_RW_W // D_S5
PROJ_RG_BLK = PROJ_SG_BLK + 1
PROJ_MG_BLK = (PROJ_RW_W + D_S5 + D_RWKV) // D_MODEL


def _rms_bf16(x, g):
    ms = jnp.mean(x * x, axis=-1, keepdims=True)
    return (x * lax.rsqrt(ms + RMS_EPS) * g).astype(BF16)


def _in_proj_u_kernel(x_ref, g_ref, w_ref, o_ref, ubuf_ref):
    res = _dot(_rms_bf16(x_ref[...], g_ref[...]), w_ref[...])
    rows = ubuf_ref.shape[1] // S5_CHUNK
    low = lax.broadcasted_iota(jnp.int32, (rows, LANES), 1) < S5_HALF
    for o in range(ubuf_ref.shape[0]):
        ubuf_ref[o] = res[:, o * LANES:(o + 1) * LANES]
        for tp in range(S5_CHUNK // 2):
            te = ubuf_ref[o, pl.ds(2 * tp, rows, stride=S5_CHUNK), :]
            to = ubuf_ref[o, pl.ds(2 * tp + 1, rows, stride=S5_CHUNK), :]
            pack_a = jnp.where(low, te, pltpu.roll(to, S5_HALF, 1))
            pack_b = jnp.where(low, pltpu.roll(te, S5_HALF, 1), to)
            lane = 2 * o * S5_PACK + tp * LANES
            o_ref[:, lane:lane + LANES] = pack_a.astype(BF16)
            o_ref[:, lane + S5_PACK:lane + S5_PACK + LANES] = pack_b.astype(BF16)


def _in_proj_u(x, g, w_u, tm):
    n, d = x.shape
    return pl.pallas_call(
        _in_proj_u_kernel,
        out_shape=jax.ShapeDtypeStruct((n // S5_CHUNK, D_S5 * S5_CHUNK), BF16),
        grid=(n // tm,),
        in_specs=[pl.BlockSpec((tm, d), lambda i: (i, 0)),
                  pl.BlockSpec((1, d), lambda i: (0, 0)),
                  pl.BlockSpec((d, D_S5), lambda i: (0, 0))],
        out_specs=pl.BlockSpec((tm // S5_CHUNK, D_S5 * S5_CHUNK), lambda i: (i, 0)),
        scratch_shapes=[pltpu.VMEM((D_S5 // LANES, tm, LANES), F32)],
        compiler_params=_cparams(("parallel",)),
        name="in_proj_u",
    )(x, g, w_u)


def _in_proj_kernel(x_ref, g_ref, w_ref, o_ref, hn_ref):
    @pl.when(pl.program_id(1) == 0)
    def _():
        hn_ref[...] = _rms_bf16(x_ref[...], g_ref[...])

    o_ref[...] = _dot(hn_ref[...], w_ref[...]).astype(BF16)


def _in_proj(x, g, w, tm):
    n, d = x.shape
    return pl.pallas_call(
        _in_proj_kernel,
        out_shape=jax.ShapeDtypeStruct((n, w.shape[1]), BF16),
        grid=(n // tm, w.shape[1] // IN_TN),
        in_specs=[pl.BlockSpec((tm, d), lambda i, j: (i, 0)),
                  pl.BlockSpec((1, d), lambda i, j: (0, 0)),
                  pl.BlockSpec((d, IN_TN), lambda i, j: (0, j))],
        out_specs=pl.BlockSpec((tm, IN_TN), lambda i, j: (i, j)),
        scratch_shapes=[pltpu.VMEM((tm, d), BF16)],
        compiler_params=_cparams(("parallel", "arbitrary")),
        name="in_proj",
    )(x, g, w)


def _s5_weights(lam_re, lam_im, log_dt, b_re, b_im, c_re, c_im, d):
    t = S5_CHUNK
    g, p, h = S5_GROUPS, S5_STATE, S5_GROUP
    hp = lax.Precision.HIGHEST
    lam = lax.complex(lam_re.astype(F32), lam_im.astype(F32))
    dt = jnp.exp(log_dt.astype(F32))[..., None]
    lam_dt = lam * dt
    lam_bar = jnp.exp(lam_dt)
    bbar = ((lam_bar - 1.0) / lam)[..., None] * lax.complex(b_re.astype(F32), b_im.astype(F32))
    c = lax.complex(c_re.astype(F32), c_im.astype(F32))
    steps = jnp.arange(t + 1, dtype=F32)
    pw = jnp.exp(lam_dt[None] * steps[:, None, None, None])

    kern = jnp.einsum('dghp,ldgp,dgpj->dglhj', c, pw[:t], bbar, precision=hp).real
    tt = jnp.arange(t)
    k0 = kern[0][:, 0] + kern[1][:, 0] + jnp.eye(h, dtype=F32) * d.astype(F32).reshape(g, h)[:, :, None]
    by_lag = jnp.concatenate([kern[1][:, :0:-1], k0[:, None], kern[0][:, 1:]], axis=1)
    wide = by_lag.transpose(0, 3, 1, 2).reshape(g, h, (2 * t - 1) * h).astype(BF16)
    m = jnp.stack([wide[:, :, (t - 1 - ti) * h:(t - 1 - ti) * h + t * h] for ti in range(t)], axis=1)
    m = m.reshape(g, t * h, t * h)

    pf = pw[t - 1 - tt, 0][..., None] * bbar[0][None]
    pb = pw[tt, 1][..., None] * bbar[1][None]

    def _p_mat(x):
        return x.transpose(1, 0, 3, 2).reshape(g, t * h, p)

    p_parts = [_p_mat(pf.real), _p_mat(pf.imag), _p_mat(pb.real), _p_mat(pb.imag)]

    cf = c[0][None] * pw[tt + 1, 0][:, :, None, :]
    cb = c[1][None] * pw[t - tt, 1][:, :, None, :]

    def _q_mat(x):
        return x.transpose(1, 3, 0, 2).reshape(g, p, t * h)

    q_parts = [_q_mat(cf.real), _q_mat(-cf.imag), _q_mat(cb.real), _q_mat(-cb.imag)]

    kg = S5_PACK_GROUPS
    no = g // kg
    gi = jnp.arange(kg)

    def _spread(width):
        c = jnp.arange(width)
        tgt = (gi[:, None] * width + c[None, :])[:, :, None]
        return (tgt == jnp.arange(kg * width)[None, None, :]).astype(BF16)

    def _spread_tok():
        u, hh = jnp.arange(t * h) // h, jnp.arange(t * h) % h
        tgt = (u[None, :] * (kg * h) + gi[:, None] * h + hh[None, :])[:, :, None]
        return (tgt == jnp.arange(S5_PACK)[None, None, :]).astype(BF16)

    def _cols(x, spread):
        x = x.astype(BF16).reshape(no, kg, x.shape[1], x.shape[2])
        return jnp.einsum('ogrc,gcd->ogrd', x, spread, preferred_element_type=BF16)

    def _rows_tok(x):
        c = x.shape[-1]
        return x.reshape(no, kg, t, h, c).transpose(0, 2, 1, 3, 4).reshape(no, S5_PACK, c)

    sp_tok, sp_state = _spread_tok(), _spread(p)
    w_p = jnp.concatenate([_rows_tok(_cols(x, sp_state)) for x in p_parts], axis=2)
    w_mq = jnp.concatenate(
        [_rows_tok(_cols(m, sp_tok))] + [_cols(x, sp_tok).reshape(no, kg * p, S5_PACK) for x in q_parts],
        axis=1)
    at = pw[t]
    a_rows = jnp.stack([at[0].real, at[0].imag, at[1].real, at[1].imag]).reshape(4, g * p)
    return w_p, w_mq, a_rows


def _s5_mm_kernel(x_ref, w_ref, *o_refs):
    res = _dot(x_ref[...], w_ref[0])
    n = res.shape[1] // len(o_refs)
    for k, ref in enumerate(o_refs):
        ref[...] = res[:, n * k:n * (k + 1)].astype(ref.dtype)


def _s5_mm(x2, w, rb, nout, name):
    r = x2.shape[0]
    no, _, c = w.shape
    cw = c // nout
    return pl.pallas_call(
        _s5_mm_kernel,
        out_shape=tuple(jax.ShapeDtypeStruct((r, no * cw), BF16) for _ in range(nout)),
        grid=(no, r // rb),
        in_specs=[pl.BlockSpec((rb, S5_PACK), lambda o, i: (i, o)),
                  pl.BlockSpec((1, S5_PACK, c), lambda o, i: (o, 0, 0))],
        out_specs=tuple(pl.BlockSpec((rb, cw), lambda o, i: (i, o)) for _ in range(nout)),
        compiler_params=_cparams(("parallel", "arbitrary")),
        name=name,
    )(x2, w)


def _s5_scan_kernel(a_ref, pfr_ref, pfi_ref, pbr_ref, pbi_ref,
                    hfr_ref, hfi_ref, hbr_ref, hbi_ref, carry_ref, *, rows):
    @pl.when(pl.program_id(2) == 0)
    def _():
        carry_ref[...] = jnp.zeros_like(carry_ref)

    afr, afi = a_ref[0:1, :], a_ref[1:2, :]
    abr, abi = a_ref[2:3, :], a_ref[3:4, :]

    nt = SCAN_ROWS

    def tile(i, carry):
        hfr, hfi, hbr, hbi = carry
        r0 = pl.multiple_of(i * nt, nt)
        pfr, pfi = pfr_ref[pl.ds(r0, nt), :].astype(F32), pfi_ref[pl.ds(r0, nt), :].astype(F32)
        rb0 = pl.multiple_of(rows - nt - i * nt, nt)
        pbr, pbi = pbr_ref[pl.ds(rb0, nt), :].astype(F32), pbi_ref[pl.ds(rb0, nt), :].astype(F32)
        of_r, of_i, ob_r, ob_i = [], [], [None] * nt, [None] * nt
        for s in range(nt):
            of_r.append(hfr)
            of_i.append(hfi)
            hfr, hfi = (afr * hfr - afi * hfi + pfr[s:s + 1, :],
                        afr * hfi + afi * hfr + pfi[s:s + 1, :])
            sb = nt - 1 - s
            ob_r[sb] = hbr
            ob_i[sb] = hbi
            hbr, hbi = (abr * hbr - abi * hbi + pbr[sb:sb + 1, :],
                        abr * hbi + abi * hbr + pbi[sb:sb + 1, :])
        hfr_ref[pl.ds(r0, nt), :] = jnp.concatenate(of_r, axis=0).astype(BF16)
        hfi_ref[pl.ds(r0, nt), :] = jnp.concatenate(of_i, axis=0).astype(BF16)
        hbr_ref[pl.ds(rb0, nt), :] = jnp.concatenate(ob_r, axis=0).astype(BF16)
        hbi_ref[pl.ds(rb0, nt), :] = jnp.concatenate(ob_i, axis=0).astype(BF16)
        return hfr, hfi, hbr, hbi

    init = tuple(carry_ref[k:k + 1, :] for k in range(4))
    out = lax.fori_loop(0, rows // nt, tile, init)
    for k in range(4):
        carry_ref[k:k + 1, :] = out[k]


def _s5_scan(a_rows, pfr, pfi, pbr, pbi, nseq, cb, lt):
    r, w = pfr.shape
    nblk = r // nseq // cb
    fspec = pl.BlockSpec((cb, lt), lambda b, j, i: (b * nblk + i, j))
    bspec = pl.BlockSpec((cb, lt), lambda b, j, i: (b * nblk + nblk - 1 - i, j))
    st = jax.ShapeDtypeStruct((r, w), BF16)
    return pl.pallas_call(
        functools.partial(_s5_scan_kernel, rows=cb),
        out_shape=(st, st, st, st),
        grid=(nseq, w // lt, nblk),
        in_specs=[pl.BlockSpec((4, lt), lambda b, j, i: (0, j)), fspec, fspec, bspec, bspec],
        out_specs=(fspec, fspec, bspec, bspec),
        scratch_shapes=[pltpu.VMEM((4, lt), F32)],
        compiler_params=_cparams(("parallel", "parallel", "arbitrary")),
        name="s5_scan",
    )(a_rows, pfr, pfi, pbr, pbi)


def _s5_out_kernel(x_ref, hfr_ref, hfi_ref, hbr_ref, hbi_ref, w_ref, y_ref):
    xh = jnp.concatenate([x_ref[...], hfr_ref[...], hfi_ref[...], hbr_ref[...], hbi_ref[...]], axis=1)
    y_ref[...] = _dot(xh, w_ref[0]).astype(BF16)


def _s5_out(x2, hfr, hfi, hbr, hbi, w_mq, rb):
    r = x2.shape[0]
    no, c, _ = w_mq.shape
    hspec = pl.BlockSpec((rb, (c - S5_PACK) // 4), lambda o, i: (i, o))
    yspec = pl.BlockSpec((rb, S5_PACK), lambda o, i: (i, o))
    return pl.pallas_call(
        _s5_out_kernel,
        out_shape=jax.ShapeDtypeStruct(x2.shape, BF16),
        grid=(no, r // rb),
        in_specs=[yspec, hspec, hspec, hspec, hspec,
                  pl.BlockSpec((1, c, S5_PACK), lambda o, i: (o, 0, 0))],
        out_specs=yspec,
        compiler_params=_cparams(("parallel", "arbitrary")),
        name="s5_out",
    )(x2, hfr, hfi, hbr, hbi, w_mq)


def _head_indicator():
    lane = lax.broadcasted_iota(jnp.int32, (D_RWKV, LANES), 0) // RWKV_HEAD
    col = lax.broadcasted_iota(jnp.int32, (D_RWKV, LANES), 1)
    return (lane == col).astype(BF16)


def _head_sum_bcast(x, e, et):
    return _dot(_dot(x.astype(BF16), e).astype(BF16), et)


def _rw_pre_kernel(x_ref, prev_ref, next_ref, mu_ref, wup_ref, aup_ref, w0_ref, a0_ref,
                   kk_ref_p, ka_ref_p, rk_ref_p, e_ref, et_ref,
                   r_ref, kk_ref, v_ref, kdf_ref, kdb_ref, bf_ref, bb_ref, lwf_ref, lwb_ref,
                   bonus_ref, *, tm, seq):
    i = pl.program_id(0)
    xb = x_ref[...]
    first = (i * tm) % seq == 0
    last = ((i + 1) * tm) % seq == 0
    zero = jnp.zeros((HALO, xb.shape[1]), BF16)
    x_ext = jnp.concatenate([jnp.where(first, zero, prev_ref[...]), xb,
                             jnp.where(last, zero, next_ref[...])], axis=0)
    ri = lax.broadcasted_iota(jnp.int32, (tm, tm + 2 * HALO), 0) + HALO
    ci = lax.broadcasted_iota(jnp.int32, (tm, tm + 2 * HALO), 1)
    band = ((ci == ri - 1) | (ci == ri + 1)).astype(BF16)
    x = xb.astype(F32)
    z = x * mu_ref[0:1, :] + _dot(band, x_ext) * mu_ref[1:2, :]

    r = z[:, :D_RWKV]
    k = z[:, D_RWKV:2 * D_RWKV]
    v = z[:, 2 * D_RWKV:3 * D_RWKV]
    xw = z[:, 3 * D_RWKV:3 * D_RWKV + 2 * LORA_RANK]
    xa = z[:, 3 * D_RWKV + 2 * LORA_RANK:]
    e, et = e_ref[...], et_ref[...]

    kk = k * kk_ref_p[...]
    kk = kk * lax.rsqrt(_head_sum_bcast(kk * kk, e, et) + L2_EPS)
    bonus_ref[...] = (_head_sum_bcast(r * k * rk_ref_p[...], e, et) * v).astype(BF16)
    r_ref[...] = r.astype(BF16)
    kk_ref[...] = kk.astype(BF16)
    v_ref[...] = v.astype(BF16)

    wpre = _dot(jnp.tanh(xw).astype(BF16), wup_ref[...]) + w0_ref[...]
    apre = _dot(xa.astype(BF16), aup_ref[...]) + a0_ref[...]
    lw = -math.exp(-0.5) * _sigmoid(wpre)
    a = _sigmoid(apre)
    ka = ka_ref_p[...]
    lwf_ref[...] = lw[:, :D_RWKV]
    lwb_ref[...] = lw[:, D_RWKV:]
    af, ab = a[:, :D_RWKV], a[:, D_RWKV:]
    kdf_ref[...] = (k * (1.0 + (af - 1.0) * ka)).astype(BF16)
    kdb_ref[...] = (k * (1.0 + (ab - 1.0) * ka)).astype(BF16)
    bf_ref[...] = (kk * af).astype(BF16)
    bb_ref[...] = (kk * ab).astype(BF16)


def _rw_pre(rw_in, mu, wup, aup, w0, a0, k_k, k_a, r_k, e, et, tm, seq):
    n = rw_in.shape[0]
    w = D_RW_IN
    nbh = n // HALO
    th = tm // HALO
    row = lambda i: (i, 0)
    const = lambda i: (0, 0)
    ospec = pl.BlockSpec((tm, D_RWKV), row)
    sb = jax.ShapeDtypeStruct((n, D_RWKV), BF16)
    sf = jax.ShapeDtypeStruct((n, D_RWKV), F32)
    return pl.pallas_call(
        functools.partial(_rw_pre_kernel, tm=tm, seq=seq),
        out_shape=(sb,) * 7 + (sf, sf, sb),
        grid=(n // tm,),
        in_specs=[pl.BlockSpec((tm, w), row),
                  pl.BlockSpec((HALO, w), lambda i: (jnp.maximum(i * th - 1, 0), 0)),
                  pl.BlockSpec((HALO, w), lambda i: (jnp.minimum((i + 1) * th, nbh - 1), 0)),
                  pl.BlockSpec((2, w), const),
                  pl.BlockSpec((2 * LORA_RANK, 2 * D_RWKV), const),
                  pl.BlockSpec((2 * LORA_RANK, 2 * D_RWKV), const),
                  pl.BlockSpec((1, 2 * D_RWKV), const),
                  pl.BlockSpec((1, 2 * D_RWKV), const),
                  pl.BlockSpec((1, D_RWKV), const),
                  pl.BlockSpec((1, D_RWKV), const),
                  pl.BlockSpec((1, D_RWKV), const),
                  pl.BlockSpec((D_RWKV, LANES), const),
                  pl.BlockSpec((LANES, D_RWKV), const)],
        out_specs=(ospec,) * 10,
        compiler_params=_cparams(("parallel",)),
        name="rw_pre",
    )(rw_in, rw_in, rw_in, mu, wup, aup, w0, a0, k_k, k_a, r_k, e, et)


def _wkv_kernel(rf_ref, kkf_ref, vf_ref, kdf_ref, bf_ref, lwf_ref,
                rb_ref, kkb_ref, vb_ref, kdb_ref, bb_ref, lwb_ref, yf_ref, yb_ref, s_ref):
    t = WKV_CHUNK
    hd = RWKV_HEAD
    w = HEAD_PAIR

    @pl.when(pl.program_id(1) == 0)
    def _():
        s_ref[...] = jnp.zeros_like(s_ref)

    def iota(shape, dim):
        return lax.broadcasted_iota(jnp.int32, shape, dim)

    rr, cc = iota((t, t), 0), iota((t, t), 1)
    rw_, lane = iota((t, w), 0), iota((t, w), 1)
    col = lane % hd
    m0 = lane < hd
    eye_f = (rw_ == col).astype(F32)
    m0_2t = iota((2 * t, w), 1) < hd
    eye_w = iota((w, w), 0) == iota((w, w), 1)
    bd_mask = (iota((w, w), 0) < hd) == (iota((w, w), 1) < hd)

    def direction(rev, refs, y_ref):
        if rev:
            incl, strict, tri = rw_ <= col, rw_ < col, rr <= cc
        else:
            incl, strict, tri = rw_ >= col, rw_ > col, rr >= cc
        return dict(rev=int(rev), refs=refs, y_ref=y_ref, incl=incl, strict=strict,
                    tri=tri.astype(BF16),
                    mask_2t=jnp.concatenate([strict, incl], axis=0), t_last=0 if rev else t - 1)

    dirs = (direction(False, (rf_ref, kkf_ref, vf_ref, kdf_ref, bf_ref, lwf_ref), yf_ref),
            direction(True, (rb_ref, kkb_ref, vb_ref, kdb_ref, bb_ref, lwb_ref), yb_ref))

    def bdiag(x):
        return jnp.concatenate([jnp.where(m0, x, 0.0), jnp.where(m0, 0.0, x)], axis=0)

    def group(gi, carry):
        streams = [(dm, sq) for dm in dirs for sq in range(rf_ref.shape[0])]
        for k in range(0, len(streams), WKV_STREAMS_PER_BATCH):
            batch(streams[k:k + WKV_STREAMS_PER_BATCH], gi)
        return carry

    def batch(streams, gi):
        hv = []
        for dm, sq, j in [(dm, sq, j) for dm, sq in streams for j in range(WKV_PAIRS_PER_STEP)]:
            hp = gi * WKV_PAIRS_PER_STEP + j
            sl = pl.ds(pl.multiple_of(hp * HEAD_PAIR, HEAD_PAIR), HEAD_PAIR)
            r, kk, v, kd, beta = (ref[sq, :, sl].astype(F32) for ref in dm['refs'][:5])
            lw = dm['refs'][5][sq, :, sl]
            t_last = dm['t_last']
            lcum = lw
            for s in (1, 2, 4, 8, 16, 32):
                if dm['rev']:
                    lcum = lcum + jnp.where(rw_ < t - s, pltpu.roll(lcum, t - s, 0), 0.0)
                else:
                    lcum = lcum + jnp.where(rw_ >= s, pltpu.roll(lcum, s, 0), 0.0)
            lcum_x = lcum - lw
            cref = lcum[t // 2:t // 2 + 1, :]
            ltot = lcum[t_last:t_last + 1, :]
            e1 = jnp.exp(lcum - cref)
            e1x = jnp.exp(lcum_x - cref)
            e2 = jnp.exp(cref - lcum)
            ec = jnp.exp(cref)
            ewt = jnp.exp(ltot - cref)
            wtot = jnp.exp(ltot)
            r_t = r * e1
            a_t = -kk * e1x
            b_t = beta * e2
            k_t = kd * e2
            a_0 = a_t * ec
            r_0 = r_t * ec
            b_h = b_t * ewt
            k_h = k_t * ewt
            hv.append(dict(
                dm=dm, sq=sq, hp=hp, sl=sl, wtot=wtot, a_0=a_0, r_0=r_0,
                lm=jnp.concatenate([a_t, r_t], axis=0),
                rm=jnp.concatenate([b_t, k_t], axis=0).astype(BF16),
                bk=jnp.concatenate([b_h, k_h], axis=0).astype(BF16), v=v))
        for h in hv:
            lm2 = jnp.concatenate([jnp.where(m0_2t, h['lm'], 0.0), jnp.where(m0_2t, 0.0, h['lm'])], axis=0)
            g = _dot_nt(lm2.astype(BF16), h['rm'])
            g0, g1 = g[:2 * t], pltpu.roll(g[2 * t:], hd, 1)
            own = jnp.where(m0_2t, g0, g1)
            oth = jnp.where(m0_2t, g1, g0)
            h['a_ab'] = jnp.where(h['dm']['strict'], own[:t], 0.0)
            h['a_rb'] = jnp.where(h['dm']['incl'], own[t:], 0.0).astype(BF16)
            h['a_k'] = jnp.where(h['dm']['mask_2t'], oth, 0.0).astype(BF16)
        for h in hv:
            v = h['v']
            h['inv'] = eye_f + h['a_ab']
            h['nk'] = _dot(h['a_ab'].astype(BF16), bdiag(h['a_ab']).astype(BF16))
            v_rows = jnp.concatenate([jnp.where(m0, 0.0, v), jnp.where(m0, v, 0.0)], axis=0)
            h['av'] = _dot(h['a_k'], v_rows.astype(BF16))
        for s in range(1, NEUMANN_STEPS + 1):
            for h in hv:
                nk_bd = bdiag(h['nk']).astype(BF16)
                if s < NEUMANN_STEPS:
                    nx = _dot(jnp.concatenate([h['nk'], h['inv']], axis=0).astype(BF16), nk_bd)
                    h['nk'] = nx[:t]
                    h['inv'] = h['inv'] + nx[t:]
                else:
                    h['inv'] = h['inv'] + _dot(h['inv'].astype(BF16), nk_bd)
        for h in hv:
            rhs = jnp.concatenate([bdiag(h['a_0']), bdiag(h['av'][:t])], axis=1)
            h['pp'] = _dot(h['inv'].astype(BF16), rhs.astype(BF16))
        for h in hv:
            pp = h['pp']
            rhs = jnp.concatenate([bdiag(pp[:, :w]), bdiag(pp[:, w:])], axis=1)
            h['qq'] = (_dot(h['a_rb'], rhs.astype(BF16))
                       + jnp.concatenate([h['r_0'], h['av'][t:]], axis=1))
            pv = jnp.concatenate([pp, jnp.concatenate([jnp.zeros_like(h['v']), h['v']], axis=1)], axis=0)
            h['mn'] = _dot_tn(h['bk'], pv.astype(BF16))
        for h in hv:
            sidx = (h['dm']['rev'], h['sq'], h['hp'])
            st = s_ref[sidx]
            lhs = jnp.concatenate([h['qq'][:, :w], jnp.where(bd_mask, h['mn'][:, :w], 0.0)], axis=0)
            res = _dot(lhs.astype(BF16), st.astype(BF16))
            h['dm']['y_ref'][h['sq'], :, h['sl']] = res[:t] + h['qq'][:, w:]
            wcol = jnp.sum(jnp.where(eye_w, jnp.broadcast_to(h['wtot'], (w, w)), 0.0),
                           axis=1, keepdims=True)
            s_ref[sidx] = wcol * st + res[t:] + jnp.where(bd_mask, h['mn'][:, w:], 0.0)

    lax.fori_loop(0, RWKV_HEADS // (2 * WKV_PAIRS_PER_STEP), group, 0)


def _wkv(r, kk, v, kdf, bf, lwf, kdb, bb, lwb, nseq):
    n = r.shape[0]
    t = WKV_CHUNK
    seq = n // nseq
    nc = seq // t
    ns = WKV_SEQS if nseq % WKV_SEQS == 0 else 1
    fwd = pl.BlockSpec((ns, t, D_RWKV), lambda b, c: (b, c, 0))
    bwd = pl.BlockSpec((ns, t, D_RWKV), lambda b, c: (b, nc - 1 - c, 0))
    shape = jax.ShapeDtypeStruct((nseq, seq, D_RWKV), F32)
    yf, yb = pl.pallas_call(
        _wkv_kernel,
        out_shape=(shape, shape),
        grid=(nseq // ns, nc),
        in_specs=[fwd] * 6 + [bwd] * 6,
        out_specs=(fwd, bwd),
        scratch_shapes=[pltpu.VMEM((2, ns, RWKV_HEADS // 2, HEAD_PAIR, HEAD_PAIR), F32)],
        compiler_params=_cparams(("parallel", "arbitrary")),
        name="wkv",
    )(*[a.reshape(nseq, seq, D_RWKV) for a in (r, kk, v, kdf, bf, lwf, r, kk, v, kdb, bb, lwb)])
    return yf.reshape(n, D_RWKV), yb.reshape(n, D_RWKV)


def _tail_kernel(ys_ref, sg_ref, ms_ref, yf_ref, yb_ref, bonus_ref, rg_ref, mr_ref, x_ref,
                 wglu_ref, swout_ref, lnw_ref, lnb_ref, e_ref, et_ref, rwout_ref, wo_ref, g_ref,
                 o_ref, ytok_ref):
    f32 = lambda ref: ref[...].astype(F32)
    rows = ys_ref.shape[0]
    slabs = ytok_ref.shape[0]
    low = lax.broadcasted_iota(jnp.int32, (rows, LANES), 1) < S5_HALF
    for o in range(slabs):
        for tp in range(S5_CHUNK // 2):
            lane = 2 * o * S5_PACK + tp * LANES
            pack_a = ys_ref[:, lane:lane + LANES].astype(F32)
            pack_b = ys_ref[:, lane + S5_PACK:lane + S5_PACK + LANES].astype(F32)
            ytok_ref[o, pl.ds(2 * tp, rows, stride=S5_CHUNK), :] = (
                jnp.where(low, pack_a, pltpu.roll(pack_b, S5_HALF, 1)))
            ytok_ref[o, pl.ds(2 * tp + 1, rows, stride=S5_CHUNK), :] = (
                jnp.where(low, pltpu.roll(pack_a, S5_HALF, 1), pack_b))
    z = _gelu_tanh(jnp.concatenate([ytok_ref[o] for o in range(slabs)], axis=1))
    z = z * _sigmoid(_dot(z.astype(BF16), wglu_ref[...]))
    z = z * _silu(f32(sg_ref))
    h = _sigmoid(f32(ms_ref)) * _dot(z.astype(BF16), swout_ref[...])

    e, et = e_ref[...], et_ref[...]
    y = yf_ref[...] + yb_ref[...]
    inv_n = 1.0 / RWKV_HEAD
    mean = _head_sum_bcast(y, e, et) * inv_n
    yc = y - mean
    var = _head_sum_bcast(yc * yc, e, et) * inv_n
    y = yc * lax.rsqrt(var + GN_EPS) * lnw_ref[...] + lnb_ref[...] + f32(bonus_ref)
    y = y * _silu(f32(rg_ref))
    h = h + _sigmoid(f32(mr_ref)) * _dot(y.astype(BF16), rwout_ref[...])

    out = _dot(h.astype(BF16), wo_ref[...])
    ms = jnp.mean(out * out, axis=-1, keepdims=True)
    o_ref[...] = x_ref[...] + out * lax.rsqrt(ms + RMS_EPS) * g_ref[...]


def _tail(y_s5, proj, yf, yb, bonus, x, p, tm):
    n = x.shape[0]
    row = lambda i: (i, 0)
    const = lambda i: (0, 0)
    half = pl.BlockSpec((tm, D_RWKV), row)
    full = pl.BlockSpec((tm, D_MODEL), row)

    def proj_cols(width, blk):
        return pl.BlockSpec((tm, width), lambda i: (i, blk))

    def resident(shape):
        return pl.BlockSpec(shape, const, pipeline_mode=pl.Buffered(1))

    return pl.pallas_call(
        _tail_kernel,
        out_shape=jax.ShapeDtypeStruct((n, D_MODEL), F32),
        grid=(n // tm,),
        in_specs=[pl.BlockSpec((tm // S5_CHUNK, S5_CHUNK * D_S5), row),
                  proj_cols(D_S5, PROJ_SG_BLK), proj_cols(D_MODEL, PROJ_MG_BLK),
                  half, half, half,
                  proj_cols(D_RWKV, PROJ_RG_BLK), proj_cols(D_MODEL, PROJ_MG_BLK + 1),
                  full,
                  resident((D_S5, D_S5)), resident((D_S5, D_MODEL)),
                  resident((1, D_RWKV)), resident((1, D_RWKV)),
                  resident((D_RWKV, LANES)), resident((LANES, D_RWKV)),
                  resident((D_RWKV, D_MODEL)), resident((D_MODEL, D_MODEL)),
                  resident((1, D_MODEL))],
        out_specs=full,
        scratch_shapes=[pltpu.VMEM((D_S5 // LANES, tm, LANES), F32)],
        compiler_params=_cparams(("parallel",)),
        name="tail",
    )(y_s5, proj, proj, yf, yb, bonus, proj, proj, x,
      p['w_glu'], p['s5_w_out'], p['ln_w'], p['ln_b'], p['e'], p['et'], p['rw_w_out'], p['w_o'],
      p['post_g'])


def _pick(n, pref):
    while n % pref:
        pref //= 2
    return pref


def _prepare(pre_norm_g, post_norm_g, w_in, s5_lam_re, s5_lam_im, s5_log_dt, s5_b_re, s5_b_im,
             s5_c_re, s5_c_im, s5_d, s5_w_glu, s5_w_out, rw_mu, rw_w0, rw_w_up, rw_a0, rw_a_up,
             rw_k_k, rw_k_a, rw_r_k, rw_ln_w, rw_ln_b, rw_w_out, w_o):
    p = {}
    p['pre_g'] = pre_norm_g.astype(F32).reshape(1, D_MODEL)
    p['post_g'] = post_norm_g.astype(F32).reshape(1, D_MODEL)
    w = w_in
    rw_end = 2 * D_S5 + D_RW_IN
    p['w_u'] = w[:, :D_S5].astype(BF16)
    p['w_proj'] = jnp.concatenate(
        [w[:, 2 * D_S5:rw_end], jnp.zeros((D_MODEL, PROJ_RW_W - D_RW_IN), w.dtype),
         w[:, D_S5:2 * D_S5], w[:, rw_end:]], axis=1).astype(BF16)
    p['s5_p'], p['s5_mq'], p['s5_a'] = _s5_weights(
        s5_lam_re, s5_lam_im, s5_log_dt, s5_b_re, s5_b_im, s5_c_re, s5_c_im, s5_d)
    p['w_glu'] = s5_w_glu.astype(BF16)
    p['s5_w_out'] = s5_w_out.astype(BF16)
    mu = rw_mu.astype(F32).reshape(1, D_RW_IN)
    p['mu'] = jnp.concatenate([1.0 - mu, 0.5 * mu], axis=0)
    zeros = jnp.zeros((LORA_RANK, D_RWKV), F32)

    def blockdiag(u):
        return jnp.concatenate([jnp.concatenate([u[0], zeros], axis=1),
                                jnp.concatenate([zeros, u[1]], axis=1)], axis=0).astype(BF16)

    p['wup'] = blockdiag(rw_w_up.astype(F32))
    p['aup'] = blockdiag(rw_a_up.astype(F32))
    p['w0'] = rw_w0.astype(F32).reshape(1, 2 * D_RWKV)
    p['a0'] = rw_a0.astype(F32).reshape(1, 2 * D_RWKV)
    p['k_k'] = rw_k_k.astype(F32).reshape(1, D_RWKV)
    p['k_a'] = rw_k_a.astype(F32).reshape(1, D_RWKV)
    p['r_k'] = rw_r_k.astype(F32).reshape(1, D_RWKV)
    p['ln_w'] = rw_ln_w.astype(F32).reshape(1, D_RWKV)
    p['ln_b'] = rw_ln_b.astype(F32).reshape(1, D_RWKV)
    p['rw_w_out'] = rw_w_out.astype(BF16)
    p['w_o'] = w_o.astype(BF16)
    e = _head_indicator()
    p['e'] = e
    p['et'] = e.T
    return p


def _s5_mix(x2, p, nseq):
    r = x2.shape[0]
    rb = _pick(r, 512)
    pfr, pfi, pbr, pbi = _s5_mm(x2, p['s5_p'], rb, 4, "s5_chunk_state")
    cb = _pick(r // nseq, 128)
    hfr, hfi, hbr, hbi = _s5_scan(p['s5_a'], pfr, pfi, pbr, pbi, nseq, cb, 1024)
    return _s5_out(x2, hfr, hfi, hbr, hbi, p['s5_mq'], rb)


def _layer(x, p):
    bsz, seq, _ = x.shape
    n = bsz * seq
    x2 = x.reshape(n, D_MODEL)
    u_rows = _in_proj_u(x2, p['pre_g'], p['w_u'], _pick(n, 512))
    proj = _in_proj(x2, p['pre_g'], p['w_proj'], _pick(n, 1024))

    y_s5 = _s5_mix(u_rows, p, bsz)

    (r, kk, v, kdf, kdb, bf, bb, lwf, lwb, bonus) = _rw_pre(
        proj, p['mu'], p['wup'], p['aup'], p['w0'], p['a0'], p['k_k'], p['k_a'], p['r_k'],
        p['e'], p['et'], _pick(seq, 256), seq)
    yf, yb = _wkv(r, kk, v, kdf, bf, lwf, kdb, bb, lwb, bsz)

    out = _tail(y_s5, proj, yf, yb, bonus, x2, p, _pick(n, 256))
    return out.reshape(bsz, seq, D_MODEL)


def kernel(x_prompt, x_sample, pre_norm_g, post_norm_g, w_in, s5_lam_re, s5_lam_im, s5_log_dt, s5_b_re, s5_b_im, s5_c_re, s5_c_im, s5_d, s5_w_glu, s5_w_out, rw_mu, rw_w0, rw_w_up, rw_a0, rw_a_up, rw_k_k, rw_k_a, rw_r_k, rw_ln_w, rw_ln_b, rw_w_out, w_o):
    params = (pre_norm_g, post_norm_g, w_in, s5_lam_re, s5_lam_im, s5_log_dt, s5_b_re, s5_b_im,
              s5_c_re, s5_c_im, s5_d, s5_w_glu, s5_w_out, rw_mu, rw_w0, rw_w_up, rw_a0, rw_a_up,
              rw_k_k, rw_k_a, rw_r_k, rw_ln_w, rw_ln_b, rw_w_out, w_o)
    y_prompt, y_sample = x_prompt, x_sample
    for layer in range(w_in.shape[0]):
        p = _prepare(*[w[layer] for w in params])
        y_prompt = _layer(y_prompt, p)
        y_sample = _layer(y_sample, p)
    return (y_prompt, y_sample)
```

```python
import functools
import math

import jax
import jax.numpy as jnp
from jax import lax
from jax.experimental import pallas as pl
from jax.experimental.pallas import tpu as pltpu

F32 = jnp.float32
BF16 = jnp.bfloat16

D_MODEL = 2048
D_S5 = 1024
S5_GROUP = 16
S5_GROUPS = 64
S5_STATE = 64
D_RWKV = 1024
RWKV_HEAD = 64
RWKV_HEADS = 16
LORA_RANK = 64
D_RW_IN = 3 * D_RWKV + 4 * LORA_RANK
RMS_EPS = 1e-6
GN_EPS = 64e-5
L2_EPS = 1e-12

LANES = 128
S5_CHUNK = 16
S5_PACK_GROUPS = 4
S5_PACK = S5_PACK_GROUPS * S5_CHUNK * S5_GROUP
S5_HALF = S5_PACK_GROUPS * S5_GROUP
WKV_CHUNK = 64
HEAD_PAIR = 2 * RWKV_HEAD
NEUMANN_STEPS = 5
WKV_PAIRS_PER_STEP = 8
SCAN_ROWS = 16
WKV_STREAMS_PER_BATCH = 2
WKV_SEQS = (1, 2, 4)
HALO = 16

VMEM_LIMIT = 48 * 1024 * 1024


def _cparams(sem):
    return pltpu.CompilerParams(dimension_semantics=sem, vmem_limit_bytes=VMEM_LIMIT)


def _dot(a, b):
    return jnp.dot(a, b, preferred_element_type=F32)


def _dot_nt(a, b):
    return lax.dot_general(a, b, (((1,), (1,)), ((), ())), preferred_element_type=F32)


def _dot_tn(a, b):
    return lax.dot_general(a, b, (((0,), (0,)), ((), ())), preferred_element_type=F32)


def _sigmoid(x):
    return 1.0 / (1.0 + jnp.exp2(x * (-math.log2(math.e))))


def _silu(x):
    return x * _sigmoid(x)


def _gelu_tanh(x):
    c = math.sqrt(2.0 / math.pi)
    return 0.5 * x * (1.0 + jnp.tanh(c * (x + 0.044715 * (x * x * x))))


IN_TN = 1024
PROJ_RW_W = 4096
PROJ_W = PROJ_RW_W + D_S5 + D_RWKV + 2 * D_MODEL
PROJ_SG_BLK = PROJ_RW_W // D_S5
PROJ_RG_BLK = PROJ_SG_BLK + 1
PROJ_MG_BLK = (PROJ_RW_W + D_S5 + D_RWKV) // D_MODEL


def _rms_bf16(x, g):
    ms = jnp.mean(x * x, axis=-1, keepdims=True)
    return (x * lax.rsqrt(ms + RMS_EPS) * g).astype(BF16)


def _in_proj_u_kernel(x_ref, g_ref, w_ref, o_ref, ubuf_ref):
    res = _dot(_rms_bf16(x_ref[...], g_ref[...]), w_ref[...])
    rows = ubuf_ref.shape[1] // S5_CHUNK
    low = lax.broadcasted_iota(jnp.int32, (rows, LANES), 1) < S5_HALF
    for o in range(ubuf_ref.shape[0]):
        ubuf_ref[o] = res[:, o * LANES:(o + 1) * LANES]
        for tp in range(S5_CHUNK // 2):
            te = ubuf_ref[o, pl.ds(2 * tp, rows, stride=S5_CHUNK), :]
            to = ubuf_ref[o, pl.ds(2 * tp + 1, rows, stride=S5_CHUNK), :]
            pack_a = jnp.where(low, te, pltpu.roll(to, S5_HALF, 1))
            pack_b = jnp.where(low, pltpu.roll(te, S5_HALF, 1), to)
            lane = 2 * o * S5_PACK + tp * LANES
            o_ref[:, lane:lane + LANES] = pack_a.astype(BF16)
            o_ref[:, lane + S5_PACK:lane + S5_PACK + LANES] = pack_b.astype(BF16)


def _in_proj_u(x, g, w_u, tm):
    n, d = x.shape
    return pl.pallas_call(
        _in_proj_u_kernel,
        out_shape=jax.ShapeDtypeStruct((n // S5_CHUNK, D_S5 * S5_CHUNK), BF16),
        grid=(n // tm,),
        in_specs=[pl.BlockSpec((tm, d), lambda i: (i, 0)),
                  pl.BlockSpec((1, d), lambda i: (0, 0)),
                  pl.BlockSpec((d, D_S5), lambda i: (0, 0))],
        out_specs=pl.BlockSpec((tm // S5_CHUNK, D_S5 * S5_CHUNK), lambda i: (i, 0)),
        scratch_shapes=[pltpu.VMEM((D_S5 // LANES, tm, LANES), F32)],
        compiler_params=_cparams(("parallel",)),
        name="in_proj_u",
    )(x, g, w_u)


def _in_proj_kernel(x_ref, g_ref, w_ref, o_ref, hn_ref):
    @pl.when(pl.program_id(1) == 0)
    def _():
        hn_ref[...] = _rms_bf16(x_ref[...], g_ref[...])

    o_ref[...] = _dot(hn_ref[...], w_ref[...]).astype(BF16)


def _in_proj(x, g, w, tm):
    n, d = x.shape
    return pl.pallas_call(
        _in_proj_kernel,
        out_shape=jax.ShapeDtypeStruct((n, w.shape[1]), BF16),
        grid=(n // tm, w.shape[1] // IN_TN),
        in_specs=[pl.BlockSpec((tm, d), lambda i, j: (i, 0)),
                  pl.BlockSpec((1, d), lambda i, j: (0, 0)),
                  pl.BlockSpec((d, IN_TN), lambda i, j: (0, j))],
        out_specs=pl.BlockSpec((tm, IN_TN), lambda i, j: (i, j)),
        scratch_shapes=[pltpu.VMEM((tm, d), BF16)],
        compiler_params=_cparams(("parallel", "arbitrary")),
        name="in_proj",
    )(x, g, w)


def _s5_weights(lam_re, lam_im, log_dt, b_re, b_im, c_re, c_im, d):
    t = S5_CHUNK
    g, p, h = S5_GROUPS, S5_STATE, S5_GROUP
    hp = lax.Precision.HIGHEST
    lam = lax.complex(lam_re.astype(F32), lam_im.astype(F32))
    dt = jnp.exp(log_dt.astype(F32))[..., None]
    lam_dt = lam * dt
    lam_bar = jnp.exp(lam_dt)
    bbar = ((lam_bar - 1.0) / lam)[..., None] * lax.complex(b_re.astype(F32), b_im.astype(F32))
    c = lax.complex(c_re.astype(F32), c_im.astype(F32))
    steps = jnp.arange(t + 1, dtype=F32)
    pw = jnp.exp(lam_dt[None] * steps[:, None, None, None])

    kern = jnp.einsum('dghp,ldgp,dgpj->dglhj', c, pw[:t], bbar, precision=hp).real
    tt = jnp.arange(t)
    k0 = kern[0][:, 0] + kern[1][:, 0] + jnp.eye(h, dtype=F32) * d.astype(F32).reshape(g, h)[:, :, None]
    by_lag = jnp.concatenate([kern[1][:, :0:-1], k0[:, None], kern[0][:, 1:]], axis=1)
    wide = by_lag.transpose(0, 3, 1, 2).reshape(g, h, (2 * t - 1) * h).astype(BF16)
    m = jnp.stack([wide[:, :, (t - 1 - ti) * h:(t - 1 - ti) * h + t * h] for ti in range(t)], axis=1)
    m = m.reshape(g, t * h, t * h)

    pf = pw[t - 1 - tt, 0][..., None] * bbar[0][None]
    pb = pw[tt, 1][..., None] * bbar[1][None]

    def _p_mat(x):
        return x.transpose(1, 0, 3, 2).reshape(g, t * h, p)

    p_parts = [_p_mat(pf.real), _p_mat(pf.imag), _p_mat(pb.real), _p_mat(pb.imag)]

    cf = c[0][None] * pw[tt + 1, 0][:, :, None, :]
    cb = c[1][None] * pw[t - tt, 1][:, :, None, :]

    def _q_mat(x):
        return x.transpose(1, 3, 0, 2).reshape(g, p, t * h)

    q_parts = [_q_mat(cf.real), _q_mat(-cf.imag), _q_mat(cb.real), _q_mat(-cb.imag)]

    kg = S5_PACK_GROUPS
    no = g // kg
    gi = jnp.arange(kg)

    def _spread(width):
        c = jnp.arange(width)
        tgt = (gi[:, None] * width + c[None, :])[:, :, None]
        return (tgt == jnp.arange(kg * width)[None, None, :]).astype(BF16)

    def _spread_tok():
        u, hh = jnp.arange(t * h) // h, jnp.arange(t * h) % h
        tgt = (u[None, :] * (kg * h) + gi[:, None] * h + hh[None, :])[:, :, None]
        return (tgt == jnp.arange(S5_PACK)[None, None, :]).astype(BF16)

    def _cols(x, spread):
        x = x.astype(BF16).reshape(no, kg, x.shape[1], x.shape[2])
        return jnp.einsum('ogrc,gcd->ogrd', x, spread, preferred_element_type=BF16)

    def _rows_tok(x):
        c = x.shape[-1]
        return x.reshape(no, kg, t, h, c).transpose(0, 2, 1, 3, 4).reshape(no, S5_PACK, c)

    sp_tok, sp_state = _spread_tok(), _spread(p)
    w_p = jnp.concatenate([_rows_tok(_cols(x, sp_state)) for x in p_parts], axis=2)
    w_mq = jnp.concatenate(
        [_rows_tok(_cols(m, sp_tok))] + [_cols(x, sp_tok).reshape(no, kg * p, S5_PACK) for x in q_parts],
        axis=1)
    at = pw[t]
    a_rows = jnp.stack([at[0].real, at[0].imag, at[1].real, at[1].imag]).reshape(4, g * p)
    return w_p, w_mq, a_rows


def _s5_mm_kernel(x_ref, w_ref, *o_refs):
    res = _dot(x_ref[...], w_ref[0])
    n = res.shape[1] // len(o_refs)
    for k, ref in enumerate(o_refs):
        ref[...] = res[:, n * k:n * (k + 1)].astype(ref.dtype)


def _s5_mm(x2, w, rb, nout, name):
    r = x2.shape[0]
    no, _, c = w.shape
    cw = c // nout
    return pl.pallas_call(
        _s5_mm_kernel,
        out_shape=tuple(jax.ShapeDtypeStruct((r, no * cw), BF16) for _ in range(nout)),
        grid=(no, r // rb),
        in_specs=[pl.BlockSpec((rb, S5_PACK), lambda o, i: (i, o)),
                  pl.BlockSpec((1, S5_PACK, c), lambda o, i: (o, 0, 0))],
        out_specs=tuple(pl.BlockSpec((rb, cw), lambda o, i: (i, o)) for _ in range(nout)),
        compiler_params=_cparams(("parallel", "arbitrary")),
        name=name,
    )(x2, w)


def _s5_scan_kernel(a_ref, pfr_ref, pfi_ref, pbr_ref, pbi_ref,
                    hfr_ref, hfi_ref, hbr_ref, hbi_ref, carry_ref, *, rows):
    @pl.when(pl.program_id(2) == 0)
    def _():
        carry_ref[...] = jnp.zeros_like(carry_ref)

    afr, afi = a_ref[0:1, :], a_ref[1:2, :]
    abr, abi = a_ref[2:3, :], a_ref[3:4, :]

    nt = SCAN_ROWS

    def tile(i, carry):
        hfr, hfi, hbr, hbi = carry
        r0 = pl.multiple_of(i * nt, nt)
        pfr, pfi = pfr_ref[pl.ds(r0, nt), :].astype(F32), pfi_ref[pl.ds(r0, nt), :].astype(F32)
        rb0 = pl.multiple_of(rows - nt - i * nt, nt)
        pbr, pbi = pbr_ref[pl.ds(rb0, nt), :].astype(F32), pbi_ref[pl.ds(rb0, nt), :].astype(F32)
        of_r, of_i, ob_r, ob_i = [], [], [None] * nt, [None] * nt
        for s in range(nt):
            of_r.append(hfr)
            of_i.append(hfi)
            hfr, hfi = (afr * hfr - afi * hfi + pfr[s:s + 1, :],
                        afr * hfi + afi * hfr + pfi[s:s + 1, :])
            sb = nt - 1 - s
            ob_r[sb] = hbr
            ob_i[sb] = hbi
            hbr, hbi = (abr * hbr - abi * hbi + pbr[sb:sb + 1, :],
                        abr * hbi + abi * hbr + pbi[sb:sb + 1, :])
        hfr_ref[pl.ds(r0, nt), :] = jnp.concatenate(of_r, axis=0).astype(BF16)
        hfi_ref[pl.ds(r0, nt), :] = jnp.concatenate(of_i, axis=0).astype(BF16)
        hbr_ref[pl.ds(rb0, nt), :] = jnp.concatenate(ob_r, axis=0).astype(BF16)
        hbi_ref[pl.ds(rb0, nt), :] = jnp.concatenate(ob_i, axis=0).astype(BF16)
        return hfr, hfi, hbr, hbi

    init = tuple(carry_ref[k:k + 1, :] for k in range(4))
    out = lax.fori_loop(0, rows // nt, tile, init)
    for k in range(4):
        carry_ref[k:k + 1, :] = out[k]


def _s5_scan(a_rows, pfr, pfi, pbr, pbi, nseq, cb, lt):
    r, w = pfr.shape
    nblk = r // nseq // cb
    fspec = pl.BlockSpec((cb, lt), lambda b, j, i: (b * nblk + i, j))
    bspec = pl.BlockSpec((cb, lt), lambda b, j, i: (b * nblk + nblk - 1 - i, j))
    st = jax.ShapeDtypeStruct((r, w), BF16)
    return pl.pallas_call(
        functools.partial(_s5_scan_kernel, rows=cb),
        out_shape=(st, st, st, st),
        grid=(nseq, w // lt, nblk),
        in_specs=[pl.BlockSpec((4, lt), lambda b, j, i: (0, j)), fspec, fspec, bspec, bspec],
        out_specs=(fspec, fspec, bspec, bspec),
        scratch_shapes=[pltpu.VMEM((4, lt), F32)],
        compiler_params=_cparams(("parallel", "parallel", "arbitrary")),
        name="s5_scan",
    )(a_rows, pfr, pfi, pbr, pbi)


def _s5_out_kernel(x_ref, hfr_ref, hfi_ref, hbr_ref, hbi_ref, w_ref, y_ref):
    xh = jnp.concatenate([x_ref[...], hfr_ref[...], hfi_ref[...], hbr_ref[...], hbi_ref[...]], axis=1)
    y_ref[...] = _dot(xh, w_ref[0]).astype(BF16)


def _s5_out(x2, hfr, hfi, hbr, hbi, w_mq, rb):
    r = x2.shape[0]
    no, c, _ = w_mq.shape
    hspec = pl.BlockSpec((rb, (c - S5_PACK) // 4), lambda o, i: (i, o))
    yspec = pl.BlockSpec((rb, S5_PACK), lambda o, i: (i, o))
    return pl.pallas_call(
        _s5_out_kernel,
        out_shape=jax.ShapeDtypeStruct(x2.shape, BF16),
        grid=(no, r // rb),
        in_specs=[yspec, hspec, hspec, hspec, hspec,
                  pl.BlockSpec((1, c, S5_PACK), lambda o, i: (o, 0, 0))],
        out_specs=yspec,
        compiler_params=_cparams(("parallel", "arbitrary")),
        name="s5_out",
    )(x2, hfr, hfi, hbr, hbi, w_mq)


def _head_indicator():
    lane = lax.broadcasted_iota(jnp.int32, (D_RWKV, LANES), 0) // RWKV_HEAD
    col = lax.broadcasted_iota(jnp.int32, (D_RWKV, LANES), 1)
    return (lane == col).astype(BF16)


def _head_sum_bcast(x, e, et):
    return _dot(_dot(x.astype(BF16), e).astype(BF16), et)


def _rw_pre_kernel(x_ref, prev_ref, next_ref, mu_ref, wup_ref, aup_ref, w0_ref, a0_ref,
                   kk_ref_p, ka_ref_p, rk_ref_p, e_ref, et_ref,
                   r_ref, kk_ref, v_ref, kdf_ref, kdb_ref, bf_ref, bb_ref, lwf_ref, lwb_ref,
                   bonus_ref, *, tm, seq):
    i = pl.program_id(0)
    xb = x_ref[...]
    first = (i * tm) % seq == 0
    last = ((i + 1) * tm) % seq == 0
    zero = jnp.zeros((HALO, xb.shape[1]), BF16)
    x_ext = jnp.concatenate([jnp.where(first, zero, prev_ref[...]), xb,
                             jnp.where(last, zero, next_ref[...])], axis=0)
    ri = lax.broadcasted_iota(jnp.int32, (tm, tm + 2 * HALO), 0) + HALO
    ci = lax.broadcasted_iota(jnp.int32, (tm, tm + 2 * HALO), 1)
    band = ((ci == ri - 1) | (ci == ri + 1)).astype(BF16)
    x = xb.astype(F32)
    z = x * mu_ref[0:1, :] + _dot(band, x_ext) * mu_ref[1:2, :]

    r = z[:, :D_RWKV]
    k = z[:, D_RWKV:2 * D_RWKV]
    v = z[:, 2 * D_RWKV:3 * D_RWKV]
    xw = z[:, 3 * D_RWKV:3 * D_RWKV + 2 * LORA_RANK]
    xa = z[:, 3 * D_RWKV + 2 * LORA_RANK:]
    e, et = e_ref[...], et_ref[...]

    kk = k * kk_ref_p[...]
    kk = kk * lax.rsqrt(_head_sum_bcast(kk * kk, e, et) + L2_EPS)
    bonus_ref[...] = (_head_sum_bcast(r * k * rk_ref_p[...], e, et) * v).astype(BF16)
    r_ref[...] = r.astype(BF16)
    kk_ref[...] = kk.astype(BF16)
    v_ref[...] = v.astype(BF16)

    wpre = _dot(jnp.tanh(xw).astype(BF16), wup_ref[...]) + w0_ref[...]
    apre = _dot(xa.astype(BF16), aup_ref[...]) + a0_ref[...]
    lw = -math.exp(-0.5) * _sigmoid(wpre)
    a = _sigmoid(apre)
    ka = ka_ref_p[...]
    lwf_ref[...] = lw[:, :D_RWKV]
    lwb_ref[...] = lw[:, D_RWKV:]
    af, ab = a[:, :D_RWKV], a[:, D_RWKV:]
    kdf_ref[...] = (k * (1.0 + (af - 1.0) * ka)).astype(BF16)
    kdb_ref[...] = (k * (1.0 + (ab - 1.0) * ka)).astype(BF16)
    bf_ref[...] = (kk * af).astype(BF16)
    bb_ref[...] = (kk * ab).astype(BF16)


def _rw_pre(rw_in, mu, wup, aup, w0, a0, k_k, k_a, r_k, e, et, tm, seq):
    n = rw_in.shape[0]
    w = D_RW_IN
    nbh = n // HALO
    th = tm // HALO
    row = lambda i: (i, 0)
    const = lambda i: (0, 0)
    ospec = pl.BlockSpec((tm, D_RWKV), row)
    sb = jax.ShapeDtypeStruct((n, D_RWKV), BF16)
    sf = jax.ShapeDtypeStruct((n, D_RWKV), F32)
    return pl.pallas_call(
        functools.partial(_rw_pre_kernel, tm=tm, seq=seq),
        out_shape=(sb,) * 7 + (sf, sf, sb),
        grid=(n // tm,),
        in_specs=[pl.BlockSpec((tm, w), row),
                  pl.BlockSpec((HALO, w), lambda i: (jnp.maximum(i * th - 1, 0), 0)),
                  pl.BlockSpec((HALO, w), lambda i: (jnp.minimum((i + 1) * th, nbh - 1), 0)),
                  pl.BlockSpec((2, w), const),
                  pl.BlockSpec((2 * LORA_RANK, 2 * D_RWKV), const),
                  pl.BlockSpec((2 * LORA_RANK, 2 * D_RWKV), const),
                  pl.BlockSpec((1, 2 * D_RWKV), const),
                  pl.BlockSpec((1, 2 * D_RWKV), const),
                  pl.BlockSpec((1, D_RWKV), const),
                  pl.BlockSpec((1, D_RWKV), const),
                  pl.BlockSpec((1, D_RWKV), const),
                  pl.BlockSpec((D_RWKV, LANES), const),
                  pl.BlockSpec((LANES, D_RWKV), const)],
        out_specs=(ospec,) * 10,
        compiler_params=_cparams(("parallel",)),
        name="rw_pre",
    )(rw_in, rw_in, rw_in, mu, wup, aup, w0, a0, k_k, k_a, r_k, e, et)


def _wkv_kernel(rf_ref, kkf_ref, vf_ref, kdf_ref, bf_ref, lwf_ref,
                rb_ref, kkb_ref, vb_ref, kdb_ref, bb_ref, lwb_ref, yf_ref, yb_ref, s_ref):
    t = WKV_CHUNK
    hd = RWKV_HEAD
    w = HEAD_PAIR

    @pl.when(pl.program_id(1) == 0)
    def _():
        s_ref[...] = jnp.zeros_like(s_ref)

    def iota(shape, dim):
        return lax.broadcasted_iota(jnp.int32, shape, dim)

    rr, cc = iota((t, t), 0), iota((t, t), 1)
    rw_, lane = iota((t, w), 0), iota((t, w), 1)
    col = lane % hd
    m0 = lane < hd
    eye_f = (rw_ == col).astype(F32)
    m0_2t = iota((2 * t, w), 1) < hd
    eye_w = iota((w, w), 0) == iota((w, w), 1)
    bd_mask = (iota((w, w), 0) < hd) == (iota((w, w), 1) < hd)

    def direction(rev, refs, y_ref):
        if rev:
            incl, strict, tri = rw_ <= col, rw_ < col, rr <= cc
        else:
            incl, strict, tri = rw_ >= col, rw_ > col, rr >= cc
        return dict(rev=int(rev), refs=refs, y_ref=y_ref, incl=incl, strict=strict,
                    tri=tri.astype(BF16),
                    mask_2t=jnp.concatenate([strict, incl], axis=0), t_last=0 if rev else t - 1)

    dirs = (direction(False, (rf_ref, kkf_ref, vf_ref, kdf_ref, bf_ref, lwf_ref), yf_ref),
            direction(True, (rb_ref, kkb_ref, vb_ref, kdb_ref, bb_ref, lwb_ref), yb_ref))

    def bdiag(x):
        return jnp.concatenate([jnp.where(m0, x, 0.0), jnp.where(m0, 0.0, x)], axis=0)

    def group(gi, carry):
        streams = [(dm, sq) for dm in dirs for sq in range(rf_ref.shape[0])]
        for k in range(0, len(streams), WKV_STREAMS_PER_BATCH):
            batch(streams[k:k + WKV_STREAMS_PER_BATCH], gi)
        return carry

    def batch(streams, gi):
        hv = []
        for dm, sq, j in [(dm, sq, j) for dm, sq in streams for j in range(WKV_PAIRS_PER_STEP)]:
            hp = gi * WKV_PAIRS_PER_STEP + j
            sl = pl.ds(pl.multiple_of(hp * HEAD_PAIR, HEAD_PAIR), HEAD_PAIR)
            r, kk, v, kd, beta = (ref[sq, :, sl].astype(F32) for ref in dm['refs'][:5])
            lw = dm['refs'][5][sq, :, sl]
            t_last = dm['t_last']
            lcum = lw
            for s in (1, 2, 4, 8, 16, 32):
                if dm['rev']:
                    lcum = lcum + jnp.where(rw_ < t - s, pltpu.roll(lcum, t - s, 0), 0.0)
                else:
                    lcum = lcum + jnp.where(rw_ >= s, pltpu.roll(lcum, s, 0), 0.0)
            lcum_x = lcum - lw
            cref = lcum[t // 2:t // 2 + 1, :]
            ltot = lcum[t_last:t_last + 1, :]
            e1 = jnp.exp(lcum - cref)
            e1x = jnp.exp(lcum_x - cref)
            e2 = jnp.exp(cref - lcum)
            ec = jnp.exp(cref)
            ewt = jnp.exp(ltot - cref)
            wtot = jnp.exp(ltot)
            r_t = r * e1
            a_t = -kk * e1x
            b_t = beta * e2
            k_t = kd * e2
            a_0 = a_t * ec
            r_0 = r_t * ec
            b_h = b_t * ewt
            k_h = k_t * ewt
            hv.append(dict(
                dm=dm, sq=sq, hp=hp, sl=sl, wtot=wtot, a_0=a_0, r_0=r_0,
                lm=jnp.concatenate([a_t, r_t], axis=0),
                rm=jnp.concatenate([b_t, k_t], axis=0).astype(BF16),
                bk=jnp.concatenate([b_h, k_h], axis=0).astype(BF16), v=v))
        for h in hv:
            lm2 = jnp.concatenate([jnp.where(m0_2t, h['lm'], 0.0), jnp.where(m0_2t, 0.0, h['lm'])], axis=0)
            g = _dot_nt(lm2.astype(BF16), h['rm'])
            g0, g1 = g[:2 * t], pltpu.roll(g[2 * t:], hd, 1)
            own = jnp.where(m0_2t, g0, g1)
            oth = jnp.where(m0_2t, g1, g0)
            h['a_ab'] = jnp.where(h['dm']['strict'], own[:t], 0.0)
            h['a_rb'] = jnp.where(h['dm']['incl'], own[t:], 0.0).astype(BF16)
            h['a_k'] = jnp.where(h['dm']['mask_2t'], oth, 0.0).astype(BF16)
        for h in hv:
            v = h['v']
            h['inv'] = eye_f + h['a_ab']
            h['nk'] = _dot(h['a_ab'].astype(BF16), bdiag(h['a_ab']).astype(BF16))
            v_rows = jnp.concatenate([jnp.where(m0, 0.0, v), jnp.where(m0, v, 0.0)], axis=0)
            h['av'] = _dot(h['a_k'], v_rows.astype(BF16))
        for s in range(1, NEUMANN_STEPS + 1):
            for h in hv:
                nk_bd = bdiag(h['nk']).astype(BF16)
                if s < NEUMANN_STEPS:
                    nx = _dot(jnp.concatenate([h['nk'], h['inv']], axis=0).astype(BF16), nk_bd)
                    h['nk'] = nx[:t]
                    h['inv'] = h['inv'] + nx[t:]
                else:
                    h['inv'] = h['inv'] + _dot(h['inv'].astype(BF16), nk_bd)
        for h in hv:
            rhs = jnp.concatenate([bdiag(h['a_0']), bdiag(h['av'][:t])], axis=1)
            h['pp'] = _dot(h['inv'].astype(BF16), rhs.astype(BF16))
        for h in hv:
            pp = h['pp']
            rhs = jnp.concatenate([bdiag(pp[:, :w]), bdiag(pp[:, w:])], axis=1)
            h['qq'] = (_dot(h['a_rb'], rhs.astype(BF16))
                       + jnp.concatenate([h['r_0'], h['av'][t:]], axis=1))
            pv = jnp.concatenate([pp, jnp.concatenate([jnp.zeros_like(h['v']), h['v']], axis=1)], axis=0)
            h['mn'] = _dot_tn(h['bk'], pv.astype(BF16))
        for h in hv:
            sidx = (h['dm']['rev'], h['sq'], h['hp'])
            st = s_ref[sidx]
            lhs = jnp.concatenate([h['qq'][:, :w], jnp.where(bd_mask, h['mn'][:, :w], 0.0)], axis=0)
            res = _dot(lhs.astype(BF16), st.astype(BF16))
            h['dm']['y_ref'][h['sq'], :, h['sl']] = res[:t] + h['qq'][:, w:]
            wcol = jnp.sum(jnp.where(eye_w, jnp.broadcast_to(h['wtot'], (w, w)), 0.0),
                           axis=1, keepdims=True)
            s_ref[sidx] = wcol * st + res[t:] + jnp.where(bd_mask, h['mn'][:, w:], 0.0)

    lax.fori_loop(0, RWKV_HEADS // (2 * WKV_PAIRS_PER_STEP), group, 0)


def _wkv(r, kk, v, kdf, bf, lwf, kdb, bb, lwb, nseq):
    n = r.shape[0]
    t = WKV_CHUNK
    seq = n // nseq
    nc = seq // t
    ns = max(k for k in WKV_SEQS if nseq % k == 0)
    fwd = pl.BlockSpec((ns, t, D_RWKV), lambda b, c: (b, c, 0))
    bwd = pl.BlockSpec((ns, t, D_RWKV), lambda b, c: (b, nc - 1 - c, 0))
    shape = jax.ShapeDtypeStruct((nseq, seq, D_RWKV), F32)
    yf, yb = pl.pallas_call(
        _wkv_kernel,
        out_shape=(shape, shape),
        grid=(nseq // ns, nc),
        in_specs=[fwd] * 6 + [bwd] * 6,
        out_specs=(fwd, bwd),
        scratch_shapes=[pltpu.VMEM((2, ns, RWKV_HEADS // 2, HEAD_PAIR, HEAD_PAIR), F32)],
        compiler_params=_cparams(("parallel", "arbitrary")),
        name="wkv",
    )(*[a.reshape(nseq, seq, D_RWKV) for a in (r, kk, v, kdf, bf, lwf, r, kk, v, kdb, bb, lwb)])
    return yf.reshape(n, D_RWKV), yb.reshape(n, D_RWKV)


def _tail_kernel(ys_ref, sg_ref, ms_ref, yf_ref, yb_ref, bonus_ref, rg_ref, mr_ref, x_ref,
                 wglu_ref, swout_ref, lnw_ref, lnb_ref, e_ref, et_ref, rwout_ref, wo_ref, g_ref,
                 o_ref, ytok_ref):
    f32 = lambda ref: ref[...].astype(F32)
    rows = ys_ref.shape[0]
    slabs = ytok_ref.shape[0]
    low = lax.broadcasted_iota(jnp.int32, (rows, LANES), 1) < S5_HALF
    for o in range(slabs):
        for tp in range(S5_CHUNK // 2):
            lane = 2 * o * S5_PACK + tp * LANES
            pack_a = ys_ref[:, lane:lane + LANES].astype(F32)
            pack_b = ys_ref[:, lane + S5_PACK:lane + S5_PACK + LANES].astype(F32)
            ytok_ref[o, pl.ds(2 * tp, rows, stride=S5_CHUNK), :] = (
                jnp.where(low, pack_a, pltpu.roll(pack_b, S5_HALF, 1)))
            ytok_ref[o, pl.ds(2 * tp + 1, rows, stride=S5_CHUNK), :] = (
                jnp.where(low, pltpu.roll(pack_a, S5_HALF, 1), pack_b))
    e, et = e_ref[...], et_ref[...]
    inv_n = 1.0 / RWKV_HEAD
    y = yf_ref[...] + yb_ref[...]
    mean_h = _dot(y.astype(BF16), e)
    z = _gelu_tanh(jnp.concatenate([ytok_ref[o] for o in range(slabs)], axis=1))
    mean = _dot(mean_h.astype(BF16), et) * inv_n
    glu = _dot(z.astype(BF16), wglu_ref[...])
    yc = y - mean
    var_h = _dot((yc * yc).astype(BF16), e)
    z = z * _sigmoid(glu) * _silu(f32(sg_ref))
    var = _dot(var_h.astype(BF16), et) * inv_n
    h = _sigmoid(f32(ms_ref)) * _dot(z.astype(BF16), swout_ref[...])
    y = yc * lax.rsqrt(var + GN_EPS) * lnw_ref[...] + lnb_ref[...] + f32(bonus_ref)
    y = y * _silu(f32(rg_ref))
    h = h + _sigmoid(f32(mr_ref)) * _dot(y.astype(BF16), rwout_ref[...])

    out = _dot(h.astype(BF16), wo_ref[...])
    ms = jnp.mean(out * out, axis=-1, keepdims=True)
    o_ref[...] = x_ref[...] + out * lax.rsqrt(ms + RMS_EPS) * g_ref[...]


def _tail(y_s5, proj, yf, yb, bonus, x, p, tm):
    n = x.shape[0]
    row = lambda i: (i, 0)
    const = lambda i: (0, 0)
    half = pl.BlockSpec((tm, D_RWKV), row)
    full = pl.BlockSpec((tm, D_MODEL), row)

    def proj_cols(width, blk):
        return pl.BlockSpec((tm, width), lambda i: (i, blk))

    def resident(shape):
        return pl.BlockSpec(shape, const, pipeline_mode=pl.Buffered(1))

    return pl.pallas_call(
        _tail_kernel,
        out_shape=jax.ShapeDtypeStruct((n, D_MODEL), F32),
        grid=(n // tm,),
        in_specs=[pl.BlockSpec((tm // S5_CHUNK, S5_CHUNK * D_S5), row),
                  proj_cols(D_S5, PROJ_SG_BLK), proj_cols(D_MODEL, PROJ_MG_BLK),
                  half, half, half,
                  proj_cols(D_RWKV, PROJ_RG_BLK), proj_cols(D_MODEL, PROJ_MG_BLK + 1),
                  full,
                  resident((D_S5, D_S5)), resident((D_S5, D_MODEL)),
                  resident((1, D_RWKV)), resident((1, D_RWKV)),
                  resident((D_RWKV, LANES)), resident((LANES, D_RWKV)),
                  resident((D_RWKV, D_MODEL)), resident((D_MODEL, D_MODEL)),
                  resident((1, D_MODEL))],
        out_specs=full,
        scratch_shapes=[pltpu.VMEM((D_S5 // LANES, tm, LANES), F32)],
        compiler_params=_cparams(("parallel",)),
        name="tail",
    )(y_s5, proj, proj, yf, yb, bonus, proj, proj, x,
      p['w_glu'], p['s5_w_out'], p['ln_w'], p['ln_b'], p['e'], p['et'], p['rw_w_out'], p['w_o'],
      p['post_g'])


def _pick(n, pref):
    while n % pref:
        pref //= 2
    return pref


def _prepare(pre_norm_g, post_norm_g, w_in, s5_lam_re, s5_lam_im, s5_log_dt, s5_b_re, s5_b_im,
             s5_c_re, s5_c_im, s5_d, s5_w_glu, s5_w_out, rw_mu, rw_w0, rw_w_up, rw_a0, rw_a_up,
             rw_k_k, rw_k_a, rw_r_k, rw_ln_w, rw_ln_b, rw_w_out, w_o):
    p = {}
    p['pre_g'] = pre_norm_g.astype(F32).reshape(1, D_MODEL)
    p['post_g'] = post_norm_g.astype(F32).reshape(1, D_MODEL)
    w = w_in
    rw_end = 2 * D_S5 + D_RW_IN
    p['w_u'] = w[:, :D_S5].astype(BF16)
    p['w_proj'] = jnp.concatenate(
        [w[:, 2 * D_S5:rw_end], jnp.zeros((D_MODEL, PROJ_RW_W - D_RW_IN), w.dtype),
         w[:, D_S5:2 * D_S5], w[:, rw_end:]], axis=1).astype(BF16)
    p['s5_p'], p['s5_mq'], p['s5_a'] = _s5_weights(
        s5_lam_re, s5_lam_im, s5_log_dt, s5_b_re, s5_b_im, s5_c_re, s5_c_im, s5_d)
    p['w_glu'] = s5_w_glu.astype(BF16)
    p['s5_w_out'] = s5_w_out.astype(BF16)
    mu = rw_mu.astype(F32).reshape(1, D_RW_IN)
    p['mu'] = jnp.concatenate([1.0 - mu, 0.5 * mu], axis=0)
    zeros = jnp.zeros((LORA_RANK, D_RWKV), F32)

    def blockdiag(u):
        return jnp.concatenate([jnp.concatenate([u[0], zeros], axis=1),
                                jnp.concatenate([zeros, u[1]], axis=1)], axis=0).astype(BF16)

    p['wup'] = blockdiag(rw_w_up.astype(F32))
    p['aup'] = blockdiag(rw_a_up.astype(F32))
    p['w0'] = rw_w0.astype(F32).reshape(1, 2 * D_RWKV)
    p['a0'] = rw_a0.astype(F32).reshape(1, 2 * D_RWKV)
    p['k_k'] = rw_k_k.astype(F32).reshape(1, D_RWKV)
    p['k_a'] = rw_k_a.astype(F32).reshape(1, D_RWKV)
    p['r_k'] = rw_r_k.astype(F32).reshape(1, D_RWKV)
    p['ln_w'] = rw_ln_w.astype(F32).reshape(1, D_RWKV)
    p['ln_b'] = rw_ln_b.astype(F32).reshape(1, D_RWKV)
    p['rw_w_out'] = rw_w_out.astype(BF16)
    p['w_o'] = w_o.astype(BF16)
    e = _head_indicator()
    p['e'] = e
    p['et'] = e.T
    return p


def _s5_mix(x2, p, nseq):
    r = x2.shape[0]
    rb = _pick(r, 512)
    pfr, pfi, pbr, pbi = _s5_mm(x2, p['s5_p'], rb, 4, "s5_chunk_state")
    cb = _pick(r // nseq, 128)
    hfr, hfi, hbr, hbi = _s5_scan(p['s5_a'], pfr, pfi, pbr, pbi, nseq, cb, 1024)
    return _s5_out(x2, hfr, hfi, hbr, hbi, p['s5_mq'], rb)


def _layer(x, p):
    bsz, seq, _ = x.shape
    n = bsz * seq
    x2 = x.reshape(n, D_MODEL)
    u_rows = _in_proj_u(x2, p['pre_g'], p['w_u'], _pick(n, 512))
    proj = _in_proj(x2, p['pre_g'], p['w_proj'], _pick(n, 1024))

    y_s5 = _s5_mix(u_rows, p, bsz)

    (r, kk, v, kdf, kdb, bf, bb, lwf, lwb, bonus) = _rw_pre(
        proj, p['mu'], p['wup'], p['aup'], p['w0'], p['a0'], p['k_k'], p['k_a'], p['r_k'],
        p['e'], p['et'], _pick(seq, 256), seq)
    yf, yb = _wkv(r, kk, v, kdf, bf, lwf, kdb, bb, lwb, bsz)

    out = _tail(y_s5, proj, yf, yb, bonus, x2, p, _pick(n, 256))
    return out.reshape(bsz, seq, D_MODEL)


def kernel(x_prompt, x_sample, pre_norm_g, post_norm_g, w_in, s5_lam_re, s5_lam_im, s5_log_dt, s5_b_re, s5_b_im, s5_c_re, s5_c_im, s5_d, s5_w_glu, s5_w_out, rw_mu, rw_w0, rw_w_up, rw_a0, rw_a_up, rw_k_k, rw_k_a, rw_r_k, rw_ln_w, rw_ln_b, rw_w_out, w_o):
    params = (pre_norm_g, post_norm_g, w_in, s5_lam_re, s5_lam_im, s5_log_dt, s5_b_re, s5_b_im,
              s5_c_re, s5_c_im, s5_d, s5_w_glu, s5_w_out, rw_mu, rw_w0, rw_w_up, rw_a0, rw_a_up,
              rw_k_k, rw_k_a, rw_r_k, rw_ln_w, rw_ln_b, rw_w_out, w_o)
    y_prompt, y_sample = x_prompt, x_sample
    for layer in range(w_in.shape[0]):
        p = _prepare(*[w[layer] for w in params])
        y_prompt = _layer(y_prompt, p)
        y_sample = _layer(y_sample, p)
    return (y_prompt, y_sample)
```

```python
import functools
import math

import jax
import jax.numpy as jnp
from jax import lax
from jax.experimental import pallas as pl
from jax.experimental.pallas import tpu as pltpu

F32 = jnp.float32
BF16 = jnp.bfloat16

D_MODEL = 2048
D_S5 = 1024
S5_GROUP = 16
S5_GROUPS = 64
S5_STATE = 64
D_RWKV = 1024
RWKV_HEAD = 64
RWKV_HEADS = 16
LORA_RANK = 64
D_RW_IN = 3 * D_RWKV + 4 * LORA_RANK
RMS_EPS = 1e-6
GN_EPS = 64e-5
L2_EPS = 1e-12

LANES = 128
S5_CHUNK = 16
S5_PACK_GROUPS = 4
S5_PACK = S5_PACK_GROUPS * S5_CHUNK * S5_GROUP
S5_HALF = S5_PACK_GROUPS * S5_GROUP
WKV_CHUNK = 64
HEAD_PAIR = 2 * RWKV_HEAD
NEUMANN_STEPS = 5
WKV_PAIRS_PER_STEP = 8
SCAN_ROWS = 16
WKV_STREAMS_PER_BATCH = 2
WKV_SEQS = (1, 2, 4)
HALO = 16

VMEM_LIMIT = 48 * 1024 * 1024


def _cparams(sem):
    return pltpu.CompilerParams(dimension_semantics=sem, vmem_limit_bytes=VMEM_LIMIT)


def _dot(a, b):
    return jnp.dot(a, b, preferred_element_type=F32)


def _dot_nt(a, b):
    return lax.dot_general(a, b, (((1,), (1,)), ((), ())), preferred_element_type=F32)


def _dot_tn(a, b):
    return lax.dot_general(a, b, (((0,), (0,)), ((), ())), preferred_element_type=F32)


def _sigmoid(x):
    return 1.0 / (1.0 + jnp.exp2(x * (-math.log2(math.e))))


def _silu(x):
    return x * _sigmoid(x)


def _gelu_tanh(x):
    c = math.sqrt(2.0 / math.pi)
    return 0.5 * x * (1.0 + jnp.tanh(c * (x + 0.044715 * (x * x * x))))


IN_TN = 1024
PROJ_RW_W = 4096
PROJ_W = PROJ_RW_W + D_S5 + D_RWKV + 2 * D_MODEL
PROJ_SG_BLK = PROJ_RW_W // D_S5
PROJ_RG_BLK = PROJ_SG_BLK + 1
PROJ_MG_BLK = (PROJ_RW_W + D_S5 + D_RWKV) // D_MODEL


def _rms_bf16(x, g):
    ms = jnp.mean(x * x, axis=-1, keepdims=True)
    return (x * lax.rsqrt(ms + RMS_EPS) * g).astype(BF16)


def _in_proj_u_kernel(x_ref, g_ref, w_ref, o_ref, ubuf_ref):
    res = _dot(_rms_bf16(x_ref[...], g_ref[...]), w_ref[...])
    rows = ubuf_ref.shape[1] // S5_CHUNK
    low = lax.broadcasted_iota(jnp.int32, (rows, LANES), 1) < S5_HALF
    for o in range(ubuf_ref.shape[0]):
        ubuf_ref[o] = res[:, o * LANES:(o + 1) * LANES]
        for tp in range(S5_CHUNK // 2):
            te = ubuf_ref[o, pl.ds(2 * tp, rows, stride=S5_CHUNK), :]
            to = ubuf_ref[o, pl.ds(2 * tp + 1, rows, stride=S5_CHUNK), :]
            pack_a = jnp.where(low, te, pltpu.roll(to, S5_HALF, 1))
            pack_b = jnp.where(low, pltpu.roll(te, S5_HALF, 1), to)
            lane = 2 * o * S5_PACK + tp * LANES
            o_ref[:, lane:lane + LANES] = pack_a.astype(BF16)
            o_ref[:, lane + S5_PACK:lane + S5_PACK + LANES] = pack_b.astype(BF16)


def _in_proj_u(x, g, w_u, tm):
    n, d = x.shape
    return pl.pallas_call(
        _in_proj_u_kernel,
        out_shape=jax.ShapeDtypeStruct((n // S5_CHUNK, D_S5 * S5_CHUNK), BF16),
        grid=(n // tm,),
        in_specs=[pl.BlockSpec((tm, d), lambda i: (i, 0)),
                  pl.BlockSpec((1, d), lambda i: (0, 0)),
                  pl.BlockSpec((d, D_S5), lambda i: (0, 0))],
        out_specs=pl.BlockSpec((tm // S5_CHUNK, D_S5 * S5_CHUNK), lambda i: (i, 0)),
        scratch_shapes=[pltpu.VMEM((D_S5 // LANES, tm, LANES), F32)],
        compiler_params=_cparams(("parallel",)),
        name="in_proj_u",
    )(x, g, w_u)


def _in_proj_kernel(x_ref, g_ref, w_ref, o_ref, hn_ref):
    @pl.when(pl.program_id(1) == 0)
    def _():
        hn_ref[...] = _rms_bf16(x_ref[...], g_ref[...])

    o_ref[...] = _dot(hn_ref[...], w_ref[...]).astype(BF16)


def _in_proj(x, g, w, tm):
    n, d = x.shape
    return pl.pallas_call(
        _in_proj_kernel,
        out_shape=jax.ShapeDtypeStruct((n, w.shape[1]), BF16),
        grid=(n // tm, w.shape[1] // IN_TN),
        in_specs=[pl.BlockSpec((tm, d), lambda i, j: (i, 0)),
                  pl.BlockSpec((1, d), lambda i, j: (0, 0)),
                  pl.BlockSpec((d, IN_TN), lambda i, j: (0, j))],
        out_specs=pl.BlockSpec((tm, IN_TN), lambda i, j: (i, j)),
        scratch_shapes=[pltpu.VMEM((tm, d), BF16)],
        compiler_params=_cparams(("parallel", "arbitrary")),
        name="in_proj",
    )(x, g, w)


def _s5_weights(lam_re, lam_im, log_dt, b_re, b_im, c_re, c_im, d):
    t = S5_CHUNK
    g, p, h = S5_GROUPS, S5_STATE, S5_GROUP
    hp = lax.Precision.HIGHEST
    lam = lax.complex(lam_re.astype(F32), lam_im.astype(F32))
    dt = jnp.exp(log_dt.astype(F32))[..., None]
    lam_dt = lam * dt
    lam_bar = jnp.exp(lam_dt)
    bbar = ((lam_bar - 1.0) / lam)[..., None] * lax.complex(b_re.astype(F32), b_im.astype(F32))
    c = lax.complex(c_re.astype(F32), c_im.astype(F32))
    steps = jnp.arange(t + 1, dtype=F32)
    pw = jnp.exp(lam_dt[None] * steps[:, None, None, None])

    kern = jnp.einsum('dghp,ldgp,dgpj->dglhj', c, pw[:t], bbar, precision=hp).real
    tt = jnp.arange(t)
    k0 = kern[0][:, 0] + kern[1][:, 0] + jnp.eye(h, dtype=F32) * d.astype(F32).reshape(g, h)[:, :, None]
    by_lag = jnp.concatenate([kern[1][:, :0:-1], k0[:, None], kern[0][:, 1:]], axis=1)
    wide = by_lag.transpose(0, 3, 1, 2).reshape(g, h, (2 * t - 1) * h).astype(BF16)
    m = jnp.stack([wide[:, :, (t - 1 - ti) * h:(t - 1 - ti) * h + t * h] for ti in range(t)], axis=1)
    m = m.reshape(g, t * h, t * h)

    pf = pw[t - 1 - tt, 0][..., None] * bbar[0][None]
    pb = pw[tt, 1][..., None] * bbar[1][None]

    def _p_mat(x):
        return x.transpose(1, 0, 3, 2).reshape(g, t * h, p)

    p_parts = [_p_mat(pf.real), _p_mat(pf.imag), _p_mat(pb.real), _p_mat(pb.imag)]

    cf = c[0][None] * pw[tt + 1, 0][:, :, None, :]
    cb = c[1][None] * pw[t - tt, 1][:, :, None, :]

    def _q_mat(x):
        return x.transpose(1, 3, 0, 2).reshape(g, p, t * h)

    q_parts = [_q_mat(cf.real), _q_mat(-cf.imag), _q_mat(cb.real), _q_mat(-cb.imag)]

    kg = S5_PACK_GROUPS
    no = g // kg
    gi = jnp.arange(kg)

    def _spread(width):
        c = jnp.arange(width)
        tgt = (gi[:, None] * width + c[None, :])[:, :, None]
        return (tgt == jnp.arange(kg * width)[None, None, :]).astype(BF16)

    def _spread_tok():
        u, hh = jnp.arange(t * h) // h, jnp.arange(t * h) % h
        tgt = (u[None, :] * (kg * h) + gi[:, None] * h + hh[None, :])[:, :, None]
        return (tgt == jnp.arange(S5_PACK)[None, None, :]).astype(BF16)

    def _cols(x, spread):
        x = x.astype(BF16).reshape(no, kg, x.shape[1], x.shape[2])
        return jnp.einsum('ogrc,gcd->ogrd', x, spread, preferred_element_type=BF16)

    def _rows_tok(x):
        c = x.shape[-1]
        return x.reshape(no, kg, t, h, c).transpose(0, 2, 1, 3, 4).reshape(no, S5_PACK, c)

    sp_tok, sp_state = _spread_tok(), _spread(p)
    w_p = jnp.concatenate([_rows_tok(_cols(x, sp_state)) for x in p_parts], axis=2)
    w_mq = jnp.concatenate(
        [_rows_tok(_cols(m, sp_tok))] + [_cols(x, sp_tok).reshape(no, kg * p, S5_PACK) for x in q_parts],
        axis=1)
    at = pw[t]
    a_rows = jnp.stack([at[0].real, at[0].imag, at[1].real, at[1].imag]).reshape(4, g * p)
    return w_p, w_mq, a_rows


def _s5_mm_kernel(x_ref, w_ref, *o_refs):
    res = _dot(x_ref[...], w_ref[0])
    n = res.shape[1] // len(o_refs)
    for k, ref in enumerate(o_refs):
        ref[...] = res[:, n * k:n * (k + 1)].astype(ref.dtype)


def _s5_mm(x2, w, rb, nout, name):
    r = x2.shape[0]
    no, _, c = w.shape
    cw = c // nout
    return pl.pallas_call(
        _s5_mm_kernel,
        out_shape=tuple(jax.ShapeDtypeStruct((r, no * cw), BF16) for _ in range(nout)),
        grid=(no, r // rb),
        in_specs=[pl.BlockSpec((rb, S5_PACK), lambda o, i: (i, o)),
                  pl.BlockSpec((1, S5_PACK, c), lambda o, i: (o, 0, 0))],
        out_specs=tuple(pl.BlockSpec((rb, cw), lambda o, i: (i, o)) for _ in range(nout)),
        compiler_params=_cparams(("parallel", "arbitrary")),
        name=name,
    )(x2, w)


def _s5_scan_kernel(a_ref, pfr_ref, pfi_ref, pbr_ref, pbi_ref,
                    hfr_ref, hfi_ref, hbr_ref, hbi_ref, carry_ref, *, rows):
    @pl.when(pl.program_id(2) == 0)
    def _():
        carry_ref[...] = jnp.zeros_like(carry_ref)

    afr, afi = a_ref[0:1, :], a_ref[1:2, :]
    abr, abi = a_ref[2:3, :], a_ref[3:4, :]

    nt = SCAN_ROWS

    def tile(i, carry):
        hfr, hfi, hbr, hbi = carry
        r0 = pl.multiple_of(i * nt, nt)
        pfr, pfi = pfr_ref[pl.ds(r0, nt), :].astype(F32), pfi_ref[pl.ds(r0, nt), :].astype(F32)
        rb0 = pl.multiple_of(rows - nt - i * nt, nt)
        pbr, pbi = pbr_ref[pl.ds(rb0, nt), :].astype(F32), pbi_ref[pl.ds(rb0, nt), :].astype(F32)
        of_r, of_i, ob_r, ob_i = [], [], [None] * nt, [None] * nt
        for s in range(nt):
            of_r.append(hfr)
            of_i.append(hfi)
            hfr, hfi = (afr * hfr - afi * hfi + pfr[s:s + 1, :],
                        afr * hfi + afi * hfr + pfi[s:s + 1, :])
            sb = nt - 1 - s
            ob_r[sb] = hbr
            ob_i[sb] = hbi
            hbr, hbi = (abr * hbr - abi * hbi + pbr[sb:sb + 1, :],
                        abr * hbi + abi * hbr + pbi[sb:sb + 1, :])
        hfr_ref[pl.ds(r0, nt), :] = jnp.concatenate(of_r, axis=0).astype(BF16)
        hfi_ref[pl.ds(r0, nt), :] = jnp.concatenate(of_i, axis=0).astype(BF16)
        hbr_ref[pl.ds(rb0, nt), :] = jnp.concatenate(ob_r, axis=0).astype(BF16)
        hbi_ref[pl.ds(rb0, nt), :] = jnp.concatenate(ob_i, axis=0).astype(BF16)
        return hfr, hfi, hbr, hbi

    init = tuple(carry_ref[k:k + 1, :] for k in range(4))
    out = lax.fori_loop(0, rows // nt, tile, init)
    for k in range(4):
        carry_ref[k:k + 1, :] = out[k]


def _s5_scan(a_rows, pfr, pfi, pbr, pbi, nseq, cb, lt):
    r, w = pfr.shape
    nblk = r // nseq // cb
    fspec = pl.BlockSpec((cb, lt), lambda b, j, i: (b * nblk + i, j))
    bspec = pl.BlockSpec((cb, lt), lambda b, j, i: (b * nblk + nblk - 1 - i, j))
    st = jax.ShapeDtypeStruct((r, w), BF16)
    return pl.pallas_call(
        functools.partial(_s5_scan_kernel, rows=cb),
        out_shape=(st, st, st, st),
        grid=(nseq, w // lt, nblk),
        in_specs=[pl.BlockSpec((4, lt), lambda b, j, i: (0, j)), fspec, fspec, bspec, bspec],
        out_specs=(fspec, fspec, bspec, bspec),
        scratch_shapes=[pltpu.VMEM((4, lt), F32)],
        compiler_params=_cparams(("parallel", "parallel", "arbitrary")),
        name="s5_scan",
    )(a_rows, pfr, pfi, pbr, pbi)


def _s5_out_kernel(x_ref, hfr_ref, hfi_ref, hbr_ref, hbi_ref, w_ref, y_ref):
    xh = jnp.concatenate([x_ref[...], hfr_ref[...], hfi_ref[...], hbr_ref[...], hbi_ref[...]], axis=1)
    y_ref[...] = _dot(xh, w_ref[0]).astype(BF16)


def _s5_out(x2, hfr, hfi, hbr, hbi, w_mq, rb):
    r = x2.shape[0]
    no, c, _ = w_mq.shape
    hspec = pl.BlockSpec((rb, (c - S5_PACK) // 4), lambda o, i: (i, o))
    yspec = pl.BlockSpec((rb, S5_PACK), lambda o, i: (i, o))
    return pl.pallas_call(
        _s5_out_kernel,
        out_shape=jax.ShapeDtypeStruct(x2.shape, BF16),
        grid=(no, r // rb),
        in_specs=[yspec, hspec, hspec, hspec, hspec,
                  pl.BlockSpec((1, c, S5_PACK), lambda o, i: (o, 0, 0))],
        out_specs=yspec,
        compiler_params=_cparams(("parallel", "arbitrary")),
        name="s5_out",
    )(x2, hfr, hfi, hbr, hbi, w_mq)


def _head_indicator():
    lane = lax.broadcasted_iota(jnp.int32, (D_RWKV, LANES), 0) // RWKV_HEAD
    col = lax.broadcasted_iota(jnp.int32, (D_RWKV, LANES), 1)
    return (lane == col).astype(BF16)


def _head_sum_bcast(x, e, et):
    return _dot(_dot(x.astype(BF16), e).astype(BF16), et)


def _rw_pre_kernel(x_ref, prev_ref, next_ref, mu_ref, wup_ref, aup_ref, w0_ref, a0_ref,
                   kk_ref_p, ka_ref_p, rk_ref_p, e_ref, et_ref,
                   r_ref, kk_ref, v_ref, kdf_ref, kdb_ref, bf_ref, bb_ref, lwf_ref, lwb_ref,
                   bonus_ref, *, tm, seq):
    i = pl.program_id(0)
    xb = x_ref[...]
    first = (i * tm) % seq == 0
    last = ((i + 1) * tm) % seq == 0
    zero = jnp.zeros((HALO, xb.shape[1]), BF16)
    x_ext = jnp.concatenate([jnp.where(first, zero, prev_ref[...]), xb,
                             jnp.where(last, zero, next_ref[...])], axis=0)
    ri = lax.broadcasted_iota(jnp.int32, (tm, tm + 2 * HALO), 0) + HALO
    ci = lax.broadcasted_iota(jnp.int32, (tm, tm + 2 * HALO), 1)
    band = ((ci == ri - 1) | (ci == ri + 1)).astype(BF16)
    x = xb.astype(F32)
    z = x * mu_ref[0:1, :] + _dot(band, x_ext) * mu_ref[1:2, :]

    r = z[:, :D_RWKV]
    k = z[:, D_RWKV:2 * D_RWKV]
    v = z[:, 2 * D_RWKV:3 * D_RWKV]
    xw = z[:, 3 * D_RWKV:3 * D_RWKV + 2 * LORA_RANK]
    xa = z[:, 3 * D_RWKV + 2 * LORA_RANK:]
    e, et = e_ref[...], et_ref[...]

    kk = k * kk_ref_p[...]
    kk = kk * lax.rsqrt(_head_sum_bcast(kk * kk, e, et) + L2_EPS)
    bonus_ref[...] = (_head_sum_bcast(r * k * rk_ref_p[...], e, et) * v).astype(BF16)
    r_ref[...] = r.astype(BF16)
    kk_ref[...] = kk.astype(BF16)
    v_ref[...] = v.astype(BF16)

    wpre = _dot(jnp.tanh(xw).astype(BF16), wup_ref[...]) + w0_ref[...]
    apre = _dot(xa.astype(BF16), aup_ref[...]) + a0_ref[...]
    lw = -math.exp(-0.5) * _sigmoid(wpre)
    a = _sigmoid(apre)
    ka = ka_ref_p[...]
    lwf_ref[...] = lw[:, :D_RWKV]
    lwb_ref[...] = lw[:, D_RWKV:]
    af, ab = a[:, :D_RWKV], a[:, D_RWKV:]
    kdf_ref[...] = (k * (1.0 + (af - 1.0) * ka)).astype(BF16)
    kdb_ref[...] = (k * (1.0 + (ab - 1.0) * ka)).astype(BF16)
    bf_ref[...] = (kk * af).astype(BF16)
    bb_ref[...] = (kk * ab).astype(BF16)


def _rw_pre(rw_in, mu, wup, aup, w0, a0, k_k, k_a, r_k, e, et, tm, seq):
    n = rw_in.shape[0]
    w = D_RW_IN
    nbh = n // HALO
    th = tm // HALO
    row = lambda i: (i, 0)
    const = lambda i: (0, 0)
    ospec = pl.BlockSpec((tm, D_RWKV), row)
    sb = jax.ShapeDtypeStruct((n, D_RWKV), BF16)
    sf = jax.ShapeDtypeStruct((n, D_RWKV), F32)
    return pl.pallas_call(
        functools.partial(_rw_pre_kernel, tm=tm, seq=seq),
        out_shape=(sb,) * 7 + (sf, sf, sb),
        grid=(n // tm,),
        in_specs=[pl.BlockSpec((tm, w), row),
                  pl.BlockSpec((HALO, w), lambda i: (jnp.maximum(i * th - 1, 0), 0)),
                  pl.BlockSpec((HALO, w), lambda i: (jnp.minimum((i + 1) * th, nbh - 1), 0)),
                  pl.BlockSpec((2, w), const),
                  pl.BlockSpec((2 * LORA_RANK, 2 * D_RWKV), const),
                  pl.BlockSpec((2 * LORA_RANK, 2 * D_RWKV), const),
                  pl.BlockSpec((1, 2 * D_RWKV), const),
                  pl.BlockSpec((1, 2 * D_RWKV), const),
                  pl.BlockSpec((1, D_RWKV), const),
                  pl.BlockSpec((1, D_RWKV), const),
                  pl.BlockSpec((1, D_RWKV), const),
                  pl.BlockSpec((D_RWKV, LANES), const),
                  pl.BlockSpec((LANES, D_RWKV), const)],
        out_specs=(ospec,) * 10,
        compiler_params=_cparams(("parallel",)),
        name="rw_pre",
    )(rw_in, rw_in, rw_in, mu, wup, aup, w0, a0, k_k, k_a, r_k, e, et)


def _wkv_kernel(rf_ref, kkf_ref, vf_ref, kdf_ref, bf_ref, lwf_ref,
                rb_ref, kkb_ref, vb_ref, kdb_ref, bb_ref, lwb_ref, yf_ref, yb_ref, s_ref):
    t = WKV_CHUNK
    hd = RWKV_HEAD
    w = HEAD_PAIR

    @pl.when(pl.program_id(1) == 0)
    def _():
        s_ref[...] = jnp.zeros_like(s_ref)

    def iota(shape, dim):
        return lax.broadcasted_iota(jnp.int32, shape, dim)

    rr, cc = iota((t, t), 0), iota((t, t), 1)
    rw_, lane = iota((t, w), 0), iota((t, w), 1)
    col = lane % hd
    m0 = lane < hd
    eye_f = (rw_ == col).astype(F32)
    m0_2t = iota((2 * t, w), 1) < hd
    eye_w = iota((w, w), 0) == iota((w, w), 1)
    bd_mask = (iota((w, w), 0) < hd) == (iota((w, w), 1) < hd)

    def direction(rev, refs, y_ref):
        if rev:
            incl, strict, tri = rw_ <= col, rw_ < col, rr <= cc
        else:
            incl, strict, tri = rw_ >= col, rw_ > col, rr >= cc
        return dict(rev=int(rev), refs=refs, y_ref=y_ref, incl=incl, strict=strict,
                    tri=tri.astype(BF16),
                    mask_2t=jnp.concatenate([strict, incl], axis=0), t_last=0 if rev else t - 1)

    dirs = (direction(False, (rf_ref, kkf_ref, vf_ref, kdf_ref, bf_ref, lwf_ref), yf_ref),
            direction(True, (rb_ref, kkb_ref, vb_ref, kdb_ref, bb_ref, lwb_ref), yb_ref))

    def bdiag(x):
        return jnp.concatenate([jnp.where(m0, x, 0.0), jnp.where(m0, 0.0, x)], axis=0)

    def group(gi, carry):
        streams = [(dm, sq) for dm in dirs for sq in range(rf_ref.shape[0])]
        for k in range(0, len(streams), WKV_STREAMS_PER_BATCH):
            batch(streams[k:k + WKV_STREAMS_PER_BATCH], gi)
        return carry

    def batch(streams, gi):
        hv = []
        for dm, sq, j in [(dm, sq, j) for dm, sq in streams for j in range(WKV_PAIRS_PER_STEP)]:
            hp = gi * WKV_PAIRS_PER_STEP + j
            sl = pl.ds(pl.multiple_of(hp * HEAD_PAIR, HEAD_PAIR), HEAD_PAIR)
            r, kk, v, kd, beta = (ref[sq, :, sl].astype(F32) for ref in dm['refs'][:5])
            lw = dm['refs'][5][sq, :, sl]
            t_last = dm['t_last']
            lcum = lw
            for s in (1, 2, 4, 8, 16, 32):
                if dm['rev']:
                    lcum = lcum + jnp.where(rw_ < t - s, pltpu.roll(lcum, t - s, 0), 0.0)
                else:
                    lcum = lcum + jnp.where(rw_ >= s, pltpu.roll(lcum, s, 0), 0.0)
            lcum_x = lcum - lw
            cref = lcum[t // 2:t // 2 + 1, :]
            ltot = lcum[t_last:t_last + 1, :]
            e1 = jnp.exp(lcum - cref)
            e1x = jnp.exp(lcum_x - cref)
            e2 = jnp.exp(cref - lcum)
            ec = jnp.exp(cref)
            ewt = jnp.exp(ltot - cref)
            wtot = jnp.exp(ltot)
            r_t = r * e1
            a_t = -kk * e1x
            b_t = beta * e2
            k_t = kd * e2
            a_0 = a_t * ec
            r_0 = r_t * ec
            b_h = b_t * ewt
            k_h = k_t * ewt
            hv.append(dict(
                dm=dm, sq=sq, hp=hp, sl=sl, wtot=wtot, a_0=a_0, r_0=r_0,
                lm=jnp.concatenate([a_t, r_t], axis=0),
                rm=jnp.concatenate([b_t, k_t], axis=0).astype(BF16),
                bk=jnp.concatenate([b_h, k_h], axis=0).astype(BF16), v=v))
        for h in hv:
            lm2 = jnp.concatenate([jnp.where(m0_2t, h['lm'], 0.0), jnp.where(m0_2t, 0.0, h['lm'])], axis=0)
            g = _dot_nt(lm2.astype(BF16), h['rm'])
            g0, g1 = g[:2 * t], pltpu.roll(g[2 * t:], hd, 1)
            own = jnp.where(m0_2t, g0, g1)
            oth = jnp.where(m0_2t, g1, g0)
            h['a_ab'] = jnp.where(h['dm']['strict'], own[:t], 0.0)
            h['a_rb'] = jnp.where(h['dm']['incl'], own[t:], 0.0).astype(BF16)
            h['a_k'] = jnp.where(h['dm']['mask_2t'], oth, 0.0).astype(BF16)
        for h in hv:
            v = h['v']
            h['inv'] = eye_f + h['a_ab']
            h['nk'] = _dot(h['a_ab'].astype(BF16), bdiag(h['a_ab']).astype(BF16))
            v_rows = jnp.concatenate([jnp.where(m0, 0.0, v), jnp.where(m0, v, 0.0)], axis=0)
            h['av'] = _dot(h['a_k'], v_rows.astype(BF16))
        for s in range(1, NEUMANN_STEPS + 1):
            for h in hv:
                nk_bd = bdiag(h['nk']).astype(BF16)
                if s < NEUMANN_STEPS:
                    nx = _dot(jnp.concatenate([h['nk'], h['inv']], axis=0).astype(BF16), nk_bd)
                    h['nk'] = nx[:t]
                    h['inv'] = h['inv'] + nx[t:]
                else:
                    h['inv'] = h['inv'] + _dot(h['inv'].astype(BF16), nk_bd)
        for h in hv:
            rhs = jnp.concatenate([bdiag(h['a_0']), bdiag(h['av'][:t])], axis=1)
            h['pp'] = _dot(h['inv'].astype(BF16), rhs.astype(BF16))
        for h in hv:
            pp = h['pp']
            rhs = jnp.concatenate([bdiag(pp[:, :w]), bdiag(pp[:, w:])], axis=1)
            h['qq'] = (_dot(h['a_rb'], rhs.astype(BF16))
                       + jnp.concatenate([h['r_0'], h['av'][t:]], axis=1))
            pv = jnp.concatenate([pp, jnp.concatenate([jnp.zeros_like(h['v']), h['v']], axis=1)], axis=0)
            h['mn'] = _dot_tn(h['bk'], pv.astype(BF16))
        for h in hv:
            sidx = (h['dm']['rev'], h['sq'], h['hp'])
            st = s_ref[sidx]
            lhs = jnp.concatenate([h['qq'][:, :w], jnp.where(bd_mask, h['mn'][:, :w], 0.0)], axis=0)
            res = _dot(lhs.astype(BF16), st.astype(BF16))
            h['dm']['y_ref'][h['sq'], :, h['sl']] = res[:t] + h['qq'][:, w:]
            wcol = jnp.sum(jnp.where(eye_w, jnp.broadcast_to(h['wtot'], (w, w)), 0.0),
                           axis=1, keepdims=True)
            s_ref[sidx] = wcol * st + res[t:] + jnp.where(bd_mask, h['mn'][:, w:], 0.0)

    lax.fori_loop(0, RWKV_HEADS // (2 * WKV_PAIRS_PER_STEP), group, 0)


def _wkv(r, kk, v, kdf, bf, lwf, kdb, bb, lwb, nseq):
    n = r.shape[0]
    t = WKV_CHUNK
    seq = n // nseq
    nc = seq // t
    ns = max(k for k in WKV_SEQS if nseq % k == 0)
    fwd = pl.BlockSpec((ns, t, D_RWKV), lambda b, c: (b, c, 0))
    bwd = pl.BlockSpec((ns, t, D_RWKV), lambda b, c: (b, nc - 1 - c, 0))
    shape = jax.ShapeDtypeStruct((nseq, seq, D_RWKV), F32)
    yf, yb = pl.pallas_call(
        _wkv_kernel,
        out_shape=(shape, shape),
        grid=(nseq // ns, nc),
        in_specs=[fwd] * 6 + [bwd] * 6,
        out_specs=(fwd, bwd),
        scratch_shapes=[pltpu.VMEM((2, ns, RWKV_HEADS // 2, HEAD_PAIR, HEAD_PAIR), F32)],
        compiler_params=_cparams(("parallel", "arbitrary")),
        name="wkv",
    )(*[a.reshape(nseq, seq, D_RWKV) for a in (r, kk, v, kdf, bf, lwf, r, kk, v, kdb, bb, lwb)])
    return yf.reshape(n, D_RWKV), yb.reshape(n, D_RWKV)


def _tail_kernel(ys_ref, sg_ref, ms_ref, yf_ref, yb_ref, bonus_ref, rg_ref, mr_ref, x_ref,
                 wglu_ref, swout_ref, lnw_ref, lnb_ref, e_ref, et_ref, rwout_ref, wo_ref, g_ref,
                 o_ref, ytok_ref):
    f32 = lambda ref: ref[...].astype(F32)
    rows = ys_ref.shape[0]
    slabs = ytok_ref.shape[0]
    low = lax.broadcasted_iota(jnp.int32, (rows, LANES), 1) < S5_HALF
    for o in range(slabs):
        for tp in range(S5_CHUNK // 2):
            lane = 2 * o * S5_PACK + tp * LANES
            pack_a = ys_ref[:, lane:lane + LANES].astype(F32)
            pack_b = ys_ref[:, lane + S5_PACK:lane + S5_PACK + LANES].astype(F32)
            ytok_ref[o, pl.ds(2 * tp, rows, stride=S5_CHUNK), :] = (
                jnp.where(low, pack_a, pltpu.roll(pack_b, S5_HALF, 1)))
            ytok_ref[o, pl.ds(2 * tp + 1, rows, stride=S5_CHUNK), :] = (
                jnp.where(low, pltpu.roll(pack_a, S5_HALF, 1), pack_b))
    e, et = e_ref[...], et_ref[...]
    inv_n = 1.0 / RWKV_HEAD
    y = yf_ref[...] + yb_ref[...]
    mean_h = _dot(y.astype(BF16), e)
    z = _gelu_tanh(jnp.concatenate([ytok_ref[o] for o in range(slabs)], axis=1))
    mean = _dot(mean_h.astype(BF16), et) * inv_n
    glu = _dot(z.astype(BF16), wglu_ref[...])
    yc = y - mean
    var_h = _dot((yc * yc).astype(BF16), e)
    z = z * _sigmoid(glu) * _silu(f32(sg_ref))
    var = _dot(var_h.astype(BF16), et) * inv_n
    h = _sigmoid(f32(ms_ref)) * _dot(z.astype(BF16), swout_ref[...])
    y = yc * lax.rsqrt(var + GN_EPS) * lnw_ref[...] + lnb_ref[...] + f32(bonus_ref)
    y = y * _silu(f32(rg_ref))
    h = h + _sigmoid(f32(mr_ref)) * _dot(y.astype(BF16), rwout_ref[...])

    out = _dot(h.astype(BF16), wo_ref[...])
    ms = jnp.mean(out * out, axis=-1, keepdims=True)
    o_ref[...] = x_ref[...] + out * lax.rsqrt(ms + RMS_EPS) * g_ref[...]


def _tail(y_s5, proj, yf, yb, bonus, x, p, tm):
    n = x.shape[0]
    row = lambda i: (i, 0)
    const = lambda i: (0, 0)
    half = pl.BlockSpec((tm, D_RWKV), row)
    full = pl.BlockSpec((tm, D_MODEL), row)

    def proj_cols(width, blk):
        return pl.BlockSpec((tm, width), lambda i: (i, blk))

    def resident(shape):
        return pl.BlockSpec(shape, const, pipeline_mode=pl.Buffered(1))

    return pl.pallas_call(
        _tail_kernel,
        out_shape=jax.ShapeDtypeStruct((n, D_MODEL), F32),
        grid=(n // tm,),
        in_specs=[pl.BlockSpec((tm // S5_CHUNK, S5_CHUNK * D_S5), row),
                  proj_cols(D_S5, PROJ_SG_BLK), proj_cols(D_MODEL, PROJ_MG_BLK),
                  half, half, half,
                  proj_cols(D_RWKV, PROJ_RG_BLK), proj_cols(D_MODEL, PROJ_MG_BLK + 1),
                  full,
                  resident((D_S5, D_S5)), resident((D_S5, D_MODEL)),
                  resident((1, D_RWKV)), resident((1, D_RWKV)),
                  resident((D_RWKV, LANES)), resident((LANES, D_RWKV)),
                  resident((D_RWKV, D_MODEL)), resident((D_MODEL, D_MODEL)),
                  resident((1, D_MODEL))],
        out_specs=full,
        scratch_shapes=[pltpu.VMEM((D_S5 // LANES, tm, LANES), F32)],
        compiler_params=_cparams(("parallel",)),
        name="tail",
    )(y_s5, proj, proj, yf, yb, bonus, proj, proj, x,
      p['w_glu'], p['s5_w_out'], p['ln_w'], p['ln_b'], p['e'], p['et'], p['rw_w_out'], p['w_o'],
      p['post_g'])


def _pick(n, pref):
    while n % pref:
        pref //= 2
    return pref


def _prepare(pre_norm_g, post_norm_g, w_in, s5_lam_re, s5_lam_im, s5_log_dt, s5_b_re, s5_b_im,
             s5_c_re, s5_c_im, s5_d, s5_w_glu, s5_w_out, rw_mu, rw_w0, rw_w_up, rw_a0, rw_a_up,
             rw_k_k, rw_k_a, rw_r_k, rw_ln_w, rw_ln_b, rw_w_out, w_o):
    p = {}
    p['pre_g'] = pre_norm_g.astype(F32).reshape(1, D_MODEL)
    p['post_g'] = post_norm_g.astype(F32).reshape(1, D_MODEL)
    w = w_in
    rw_end = 2 * D_S5 + D_RW_IN
    p['w_u'] = w[:, :D_S5].astype(BF16)
    p['w_proj'] = jnp.concatenate(
        [w[:, 2 * D_S5:rw_end], jnp.zeros((D_MODEL, PROJ_RW_W - D_RW_IN), w.dtype),
         w[:, D_S5:2 * D_S5], w[:, rw_end:]], axis=1).astype(BF16)
    p['s5_p'], p['s5_mq'], p['s5_a'] = _s5_weights(
        s5_lam_re, s5_lam_im, s5_log_dt, s5_b_re, s5_b_im, s5_c_re, s5_c_im, s5_d)
    p['w_glu'] = s5_w_glu.astype(BF16)
    p['s5_w_out'] = s5_w_out.astype(BF16)
    mu = rw_mu.astype(F32).reshape(1, D_RW_IN)
    p['mu'] = jnp.concatenate([1.0 - mu, 0.5 * mu], axis=0)
    zeros = jnp.zeros((LORA_RANK, D_RWKV), F32)

    def blockdiag(u):
        return jnp.concatenate([jnp.concatenate([u[0], zeros], axis=1),
                                jnp.concatenate([zeros, u[1]], axis=1)], axis=0).astype(BF16)

    p['wup'] = blockdiag(rw_w_up.astype(F32))
    p['aup'] = blockdiag(rw_a_up.astype(F32))
    p['w0'] = rw_w0.astype(F32).reshape(1, 2 * D_RWKV)
    p['a0'] = rw_a0.astype(F32).reshape(1, 2 * D_RWKV)
    p['k_k'] = rw_k_k.astype(F32).reshape(1, D_RWKV)
    p['k_a'] = rw_k_a.astype(F32).reshape(1, D_RWKV)
    p['r_k'] = rw_r_k.astype(F32).reshape(1, D_RWKV)
    p['ln_w'] = rw_ln_w.astype(F32).reshape(1, D_RWKV)
    p['ln_b'] = rw_ln_b.astype(F32).reshape(1, D_RWKV)
    p['rw_w_out'] = rw_w_out.astype(BF16)
    p['w_o'] = w_o.astype(BF16)
    e = _head_indicator()
    p['e'] = e
    p['et'] = e.T
    return p


def _s5_mix(x2, p, nseq):
    r = x2.shape[0]
    rb = _pick(r, 512)
    pfr, pfi, pbr, pbi = _s5_mm(x2, p['s5_p'], rb, 4, "s5_chunk_state")
    cb = _pick(r // nseq, 128)
    hfr, hfi, hbr, hbi = _s5_scan(p['s5_a'], pfr, pfi, pbr, pbi, nseq, cb, 2048)
    return _s5_out(x2, hfr, hfi, hbr, hbi, p['s5_mq'], rb)


def _layer(x, p):
    bsz, seq, _ = x.shape
    n = bsz * seq
    x2 = x.reshape(n, D_MODEL)
    u_rows = _in_proj_u(x2, p['pre_g'], p['w_u'], _pick(n, 512))
    proj = _in_proj(x2, p['pre_g'], p['w_proj'], _pick(n, 1024))

    y_s5 = _s5_mix(u_rows, p, bsz)

    (r, kk, v, kdf, kdb, bf, bb, lwf, lwb, bonus) = _rw_pre(
        proj, p['mu'], p['wup'], p['aup'], p['w0'], p['a0'], p['k_k'], p['k_a'], p['r_k'],
        p['e'], p['et'], _pick(seq, 256), seq)
    yf, yb = _wkv(r, kk, v, kdf, bf, lwf, kdb, bb, lwb, bsz)

    out = _tail(y_s5, proj, yf, yb, bonus, x2, p, _pick(n, 256))
    return out.reshape(bsz, seq, D_MODEL)


def kernel(x_prompt, x_sample, pre_norm_g, post_norm_g, w_in, s5_lam_re, s5_lam_im, s5_log_dt, s5_b_re, s5_b_im, s5_c_re, s5_c_im, s5_d, s5_w_glu, s5_w_out, rw_mu, rw_w0, rw_w_up, rw_a0, rw_a_up, rw_k_k, rw_k_a, rw_r_k, rw_ln_w, rw_ln_b, rw_w_out, w_o):
    params = (pre_norm_g, post_norm_g, w_in, s5_lam_re, s5_lam_im, s5_log_dt, s5_b_re, s5_b_im,
              s5_c_re, s5_c_im, s5_d, s5_w_glu, s5_w_out, rw_mu, rw_w0, rw_w_up, rw_a0, rw_a_up,
              rw_k_k, rw_k_a, rw_r_k, rw_ln_w, rw_ln_b, rw_w_out, w_o)
    y_prompt, y_sample = x_prompt, x_sample
    for layer in range(w_in.shape[0]):
        p = _prepare(*[w[layer] for w in params])
        y_prompt = _layer(y_prompt, p)
        y_sample = _layer(y_sample, p)
    return (y_prompt, y_sample)
```

```python
import functools
import math

import jax
import jax.numpy as jnp
from jax import lax
from jax.experimental import pallas as pl
from jax.experimental.pallas import tpu as pltpu

F32 = jnp.float32
BF16 = jnp.bfloat16

D_MODEL = 2048
D_S5 = 1024
S5_GROUP = 16
S5_GROUPS = 64
S5_STATE = 64
D_RWKV = 1024
RWKV_HEAD = 64
RWKV_HEADS = 16
LORA_RANK = 64
D_RW_IN = 3 * D_RWKV + 4 * LORA_RANK
RMS_EPS = 1e-6
GN_EPS = 64e-5
L2_EPS = 1e-12

LANES = 128
S5_CHUNK = 16
S5_PACK_GROUPS = 4
S5_PACK = S5_PACK_GROUPS * S5_CHUNK * S5_GROUP
S5_HALF = S5_PACK_GROUPS * S5_GROUP
WKV_CHUNK = 64
HEAD_PAIR = 2 * RWKV_HEAD
NEUMANN_STEPS = 5
WKV_PAIRS_PER_STEP = 8
SCAN_ROWS = 16
WKV_STREAMS_PER_BATCH = 2
WKV_SEQS = (1, 2, 4)
HALO = 16

VMEM_LIMIT = 48 * 1024 * 1024


def _cparams(sem):
    return pltpu.CompilerParams(dimension_semantics=sem, vmem_limit_bytes=VMEM_LIMIT)


def _dot(a, b):
    return jnp.dot(a, b, preferred_element_type=F32)


def _dot_nt(a, b):
    return lax.dot_general(a, b, (((1,), (1,)), ((), ())), preferred_element_type=F32)


def _dot_tn(a, b):
    return lax.dot_general(a, b, (((0,), (0,)), ((), ())), preferred_element_type=F32)


def _sigmoid(x):
    return 1.0 / (1.0 + jnp.exp2(x * (-math.log2(math.e))))


def _silu(x):
    return x * _sigmoid(x)


def _gelu_tanh(x):
    c = math.sqrt(2.0 / math.pi)
    return 0.5 * x * (1.0 + jnp.tanh(c * (x + 0.044715 * (x * x * x))))


IN_TN = 1024
PROJ_RW_W = 4096
PROJ_W = PROJ_RW_W + D_S5 + D_RWKV + 2 * D_MODEL
PROJ_SG_BLK = PROJ_RW_W // D_S5
PROJ_RG_BLK = PROJ_SG_BLK + 1
PROJ_MG_BLK = (PROJ_RW_W + D_S5 + D_RWKV) // D_MODEL


def _rms_bf16(x, g):
    ms = jnp.mean(x * x, axis=-1, keepdims=True)
    return (x * lax.rsqrt(ms + RMS_EPS) * g).astype(BF16)


def _in_proj_u_kernel(x_ref, g_ref, w_ref, o_ref, ubuf_ref):
    res = _dot(_rms_bf16(x_ref[...], g_ref[...]), w_ref[...])
    rows = ubuf_ref.shape[1] // S5_CHUNK
    low = lax.broadcasted_iota(jnp.int32, (rows, LANES), 1) < S5_HALF
    for o in range(ubuf_ref.shape[0]):
        ubuf_ref[o] = res[:, o * LANES:(o + 1) * LANES]
        for tp in range(S5_CHUNK // 2):
            te = ubuf_ref[o, pl.ds(2 * tp, rows, stride=S5_CHUNK), :]
            to = ubuf_ref[o, pl.ds(2 * tp + 1, rows, stride=S5_CHUNK), :]
            pack_a = jnp.where(low, te, pltpu.roll(to, S5_HALF, 1))
            pack_b = jnp.where(low, pltpu.roll(te, S5_HALF, 1), to)
            lane = 2 * o * S5_PACK + tp * LANES
            o_ref[:, lane:lane + LANES] = pack_a.astype(BF16)
            o_ref[:, lane + S5_PACK:lane + S5_PACK + LANES] = pack_b.astype(BF16)


def _in_proj_u(x, g, w_u, tm):
    n, d = x.shape
    return pl.pallas_call(
        _in_proj_u_kernel,
        out_shape=jax.ShapeDtypeStruct((n // S5_CHUNK, D_S5 * S5_CHUNK), BF16),
        grid=(n // tm,),
        in_specs=[pl.BlockSpec((tm, d), lambda i: (i, 0)),
                  pl.BlockSpec((1, d), lambda i: (0, 0)),
                  pl.BlockSpec((d, D_S5), lambda i: (0, 0))],
        out_specs=pl.BlockSpec((tm // S5_CHUNK, D_S5 * S5_CHUNK), lambda i: (i, 0)),
        scratch_shapes=[pltpu.VMEM((D_S5 // LANES, tm, LANES), F32)],
        compiler_params=_cparams(("parallel",)),
        name="in_proj_u",
    )(x, g, w_u)


def _in_proj_kernel(x_ref, g_ref, w_ref, o_ref, hn_ref):
    @pl.when(pl.program_id(1) == 0)
    def _():
        hn_ref[...] = _rms_bf16(x_ref[...], g_ref[...])

    o_ref[...] = _dot(hn_ref[...], w_ref[...]).astype(BF16)


def _in_proj(x, g, w, tm):
    n, d = x.shape
    return pl.pallas_call(
        _in_proj_kernel,
        out_shape=jax.ShapeDtypeStruct((n, w.shape[1]), BF16),
        grid=(n // tm, w.shape[1] // IN_TN),
        in_specs=[pl.BlockSpec((tm, d), lambda i, j: (i, 0)),
                  pl.BlockSpec((1, d), lambda i, j: (0, 0)),
                  pl.BlockSpec((d, IN_TN), lambda i, j: (0, j))],
        out_specs=pl.BlockSpec((tm, IN_TN), lambda i, j: (i, j)),
        scratch_shapes=[pltpu.VMEM((tm, d), BF16)],
        compiler_params=_cparams(("parallel", "arbitrary")),
        name="in_proj",
    )(x, g, w)


def _s5_weights(lam_re, lam_im, log_dt, b_re, b_im, c_re, c_im, d):
    t = S5_CHUNK
    g, p, h = S5_GROUPS, S5_STATE, S5_GROUP
    hp = lax.Precision.HIGHEST
    lam = lax.complex(lam_re.astype(F32), lam_im.astype(F32))
    dt = jnp.exp(log_dt.astype(F32))[..., None]
    lam_dt = lam * dt
    lam_bar = jnp.exp(lam_dt)
    bbar = ((lam_bar - 1.0) / lam)[..., None] * lax.complex(b_re.astype(F32), b_im.astype(F32))
    c = lax.complex(c_re.astype(F32), c_im.astype(F32))
    steps = jnp.arange(t + 1, dtype=F32)
    pw = jnp.exp(lam_dt[None] * steps[:, None, None, None])

    kern = jnp.einsum('dghp,ldgp,dgpj->dglhj', c, pw[:t], bbar, precision=hp).real
    tt = jnp.arange(t)
    k0 = kern[0][:, 0] + kern[1][:, 0] + jnp.eye(h, dtype=F32) * d.astype(F32).reshape(g, h)[:, :, None]
    by_lag = jnp.concatenate([kern[1][:, :0:-1], k0[:, None], kern[0][:, 1:]], axis=1)
    wide = by_lag.transpose(0, 3, 1, 2).reshape(g, h, (2 * t - 1) * h).astype(BF16)
    m = jnp.stack([wide[:, :, (t - 1 - ti) * h:(t - 1 - ti) * h + t * h] for ti in range(t)], axis=1)
    m = m.reshape(g, t * h, t * h)

    pf = pw[t - 1 - tt, 0][..., None] * bbar[0][None]
    pb = pw[tt, 1][..., None] * bbar[1][None]

    def _p_mat(x):
        return x.transpose(1, 0, 3, 2).reshape(g, t * h, p)

    p_parts = [_p_mat(pf.real), _p_mat(pf.imag), _p_mat(pb.real), _p_mat(pb.imag)]

    cf = c[0][None] * pw[tt + 1, 0][:, :, None, :]
    cb = c[1][None] * pw[t - tt, 1][:, :, None, :]

    def _q_mat(x):
        return x.transpose(1, 3, 0, 2).reshape(g, p, t * h)

    q_parts = [_q_mat(cf.real), _q_mat(-cf.imag), _q_mat(cb.real), _q_mat(-cb.imag)]

    kg = S5_PACK_GROUPS
    no = g // kg
    gi = jnp.arange(kg)

    def _spread(width):
        c = jnp.arange(width)
        tgt = (gi[:, None] * width + c[None, :])[:, :, None]
        return (tgt == jnp.arange(kg * width)[None, None, :]).astype(BF16)

    def _spread_tok():
        u, hh = jnp.arange(t * h) // h, jnp.arange(t * h) % h
        tgt = (u[None, :] * (kg * h) + gi[:, None] * h + hh[None, :])[:, :, None]
        return (tgt == jnp.arange(S5_PACK)[None, None, :]).astype(BF16)

    def _cols(x, spread):
        x = x.astype(BF16).reshape(no, kg, x.shape[1], x.shape[2])
        return jnp.einsum('ogrc,gcd->ogrd', x, spread, preferred_element_type=BF16)

    def _rows_tok(x):
        c = x.shape[-1]
        return x.reshape(no, kg, t, h, c).transpose(0, 2, 1, 3, 4).reshape(no, S5_PACK, c)

    sp_tok, sp_state = _spread_tok(), _spread(p)
    w_p = jnp.concatenate([_rows_tok(_cols(x, sp_state)) for x in p_parts], axis=2)
    w_mq = jnp.concatenate(
        [_rows_tok(_cols(m, sp_tok))] + [_cols(x, sp_tok).reshape(no, kg * p, S5_PACK) for x in q_parts],
        axis=1)
    at = pw[t]
    a_rows = jnp.stack([at[0].real, at[0].imag, at[1].real, at[1].imag]).reshape(4, g * p)
    return w_p, w_mq, a_rows


def _s5_mm_kernel(x_ref, w_ref, *o_refs):
    res = _dot(x_ref[...], w_ref[0])
    n = res.shape[1] // len(o_refs)
    for k, ref in enumerate(o_refs):
        ref[...] = res[:, n * k:n * (k + 1)].astype(ref.dtype)


def _s5_mm(x2, w, rb, nout, name):
    r = x2.shape[0]
    no, _, c = w.shape
    cw = c // nout
    return pl.pallas_call(
        _s5_mm_kernel,
        out_shape=tuple(jax.ShapeDtypeStruct((r, no * cw), BF16) for _ in range(nout)),
        grid=(no, r // rb),
        in_specs=[pl.BlockSpec((rb, S5_PACK), lambda o, i: (i, o)),
                  pl.BlockSpec((1, S5_PACK, c), lambda o, i: (o, 0, 0))],
        out_specs=tuple(pl.BlockSpec((rb, cw), lambda o, i: (i, o)) for _ in range(nout)),
        compiler_params=_cparams(("parallel", "arbitrary")),
        name=name,
    )(x2, w)


def _s5_scan_kernel(a_ref, pfr_ref, pfi_ref, pbr_ref, pbi_ref,
                    hfr_ref, hfi_ref, hbr_ref, hbi_ref, carry_ref, *, rows):
    @pl.when(pl.program_id(2) == 0)
    def _():
        carry_ref[...] = jnp.zeros_like(carry_ref)

    afr, afi = a_ref[0:1, :], a_ref[1:2, :]
    abr, abi = a_ref[2:3, :], a_ref[3:4, :]

    nt = SCAN_ROWS

    def tile(i, carry):
        hfr, hfi, hbr, hbi = carry
        r0 = pl.multiple_of(i * nt, nt)
        pfr, pfi = pfr_ref[pl.ds(r0, nt), :].astype(F32), pfi_ref[pl.ds(r0, nt), :].astype(F32)
        rb0 = pl.multiple_of(rows - nt - i * nt, nt)
        pbr, pbi = pbr_ref[pl.ds(rb0, nt), :].astype(F32), pbi_ref[pl.ds(rb0, nt), :].astype(F32)
        of_r, of_i, ob_r, ob_i = [], [], [None] * nt, [None] * nt
        for s in range(nt):
            of_r.append(hfr)
            of_i.append(hfi)
            hfr, hfi = (afr * hfr - afi * hfi + pfr[s:s + 1, :],
                        afr * hfi + afi * hfr + pfi[s:s + 1, :])
            sb = nt - 1 - s
            ob_r[sb] = hbr
            ob_i[sb] = hbi
            hbr, hbi = (abr * hbr - abi * hbi + pbr[sb:sb + 1, :],
                        abr * hbi + abi * hbr + pbi[sb:sb + 1, :])
        hfr_ref[pl.ds(r0, nt), :] = jnp.concatenate(of_r, axis=0).astype(BF16)
        hfi_ref[pl.ds(r0, nt), :] = jnp.concatenate(of_i, axis=0).astype(BF16)
        hbr_ref[pl.ds(rb0, nt), :] = jnp.concatenate(ob_r, axis=0).astype(BF16)
        hbi_ref[pl.ds(rb0, nt), :] = jnp.concatenate(ob_i, axis=0).astype(BF16)
        return hfr, hfi, hbr, hbi

    init = tuple(carry_ref[k:k + 1, :] for k in range(4))
    out = lax.fori_loop(0, rows // nt, tile, init)
    for k in range(4):
        carry_ref[k:k + 1, :] = out[k]


def _s5_scan(a_rows, pfr, pfi, pbr, pbi, nseq, cb, lt):
    r, w = pfr.shape
    nblk = r // nseq // cb
    fspec = pl.BlockSpec((cb, lt), lambda b, j, i: (b * nblk + i, j))
    bspec = pl.BlockSpec((cb, lt), lambda b, j, i: (b * nblk + nblk - 1 - i, j))
    st = jax.ShapeDtypeStruct((r, w), BF16)
    return pl.pallas_call(
        functools.partial(_s5_scan_kernel, rows=cb),
        out_shape=(st, st, st, st),
        grid=(nseq, w // lt, nblk),
        in_specs=[pl.BlockSpec((4, lt), lambda b, j, i: (0, j)), fspec, fspec, bspec, bspec],
        out_specs=(fspec, fspec, bspec, bspec),
        scratch_shapes=[pltpu.VMEM((4, lt), F32)],
        compiler_params=_cparams(("parallel", "parallel", "arbitrary")),
        name="s5_scan",
    )(a_rows, pfr, pfi, pbr, pbi)


def _s5_out_kernel(x_ref, hfr_ref, hfi_ref, hbr_ref, hbi_ref, w_ref, y_ref):
    xh = jnp.concatenate([x_ref[...], hfr_ref[...], hfi_ref[...], hbr_ref[...], hbi_ref[...]], axis=1)
    y_ref[...] = _dot(xh, w_ref[0]).astype(BF16)


def _s5_out(x2, hfr, hfi, hbr, hbi, w_mq, rb):
    r = x2.shape[0]
    no, c, _ = w_mq.shape
    hspec = pl.BlockSpec((rb, (c - S5_PACK) // 4), lambda o, i: (i, o))
    yspec = pl.BlockSpec((rb, S5_PACK), lambda o, i: (i, o))
    return pl.pallas_call(
        _s5_out_kernel,
        out_shape=jax.ShapeDtypeStruct(x2.shape, BF16),
        grid=(no, r // rb),
        in_specs=[yspec, hspec, hspec, hspec, hspec,
                  pl.BlockSpec((1, c, S5_PACK), lambda o, i: (o, 0, 0))],
        out_specs=yspec,
        compiler_params=_cparams(("parallel", "arbitrary")),
        name="s5_out",
    )(x2, hfr, hfi, hbr, hbi, w_mq)


def _head_indicator():
    lane = lax.broadcasted_iota(jnp.int32, (D_RWKV, LANES), 0) // RWKV_HEAD
    col = lax.broadcasted_iota(jnp.int32, (D_RWKV, LANES), 1)
    return (lane == col).astype(BF16)


def _head_sum_bcast(x, e, et):
    return _dot(_dot(x.astype(BF16), e).astype(BF16), et)


def _rw_pre_kernel(x_ref, prev_ref, next_ref, mu_ref, wup_ref, aup_ref, w0_ref, a0_ref,
                   kk_ref_p, ka_ref_p, rk_ref_p, e_ref, et_ref,
                   r_ref, kk_ref, v_ref, kdf_ref, kdb_ref, bf_ref, bb_ref, lwf_ref, lwb_ref,
                   bonus_ref, *, tm, seq):
    i = pl.program_id(0)
    xb = x_ref[...]
    first = (i * tm) % seq == 0
    last = ((i + 1) * tm) % seq == 0
    zero = jnp.zeros((HALO, xb.shape[1]), BF16)
    x_ext = jnp.concatenate([jnp.where(first, zero, prev_ref[...]), xb,
                             jnp.where(last, zero, next_ref[...])], axis=0)
    ri = lax.broadcasted_iota(jnp.int32, (tm, tm + 2 * HALO), 0) + HALO
    ci = lax.broadcasted_iota(jnp.int32, (tm, tm + 2 * HALO), 1)
    band = ((ci == ri - 1) | (ci == ri + 1)).astype(BF16)
    x = xb.astype(F32)
    z = x * mu_ref[0:1, :] + _dot(band, x_ext) * mu_ref[1:2, :]

    r = z[:, :D_RWKV]
    k = z[:, D_RWKV:2 * D_RWKV]
    v = z[:, 2 * D_RWKV:3 * D_RWKV]
    xw = z[:, 3 * D_RWKV:3 * D_RWKV + 2 * LORA_RANK]
    xa = z[:, 3 * D_RWKV + 2 * LORA_RANK:]
    e, et = e_ref[...], et_ref[...]

    kk = k * kk_ref_p[...]
    kk = kk * lax.rsqrt(_head_sum_bcast(kk * kk, e, et) + L2_EPS)
    bonus_ref[...] = (_head_sum_bcast(r * k * rk_ref_p[...], e, et) * v).astype(BF16)
    r_ref[...] = r.astype(BF16)
    kk_ref[...] = kk.astype(BF16)
    v_ref[...] = v.astype(BF16)

    wpre = _dot(jnp.tanh(xw).astype(BF16), wup_ref[...]) + w0_ref[...]
    apre = _dot(xa.astype(BF16), aup_ref[...]) + a0_ref[...]
    lw = -math.exp(-0.5) * _sigmoid(wpre)
    a = _sigmoid(apre)
    ka = ka_ref_p[...]
    lwf_ref[...] = lw[:, :D_RWKV]
    lwb_ref[...] = lw[:, D_RWKV:]
    af, ab = a[:, :D_RWKV], a[:, D_RWKV:]
    kdf_ref[...] = (k * (1.0 + (af - 1.0) * ka)).astype(BF16)
    kdb_ref[...] = (k * (1.0 + (ab - 1.0) * ka)).astype(BF16)
    bf_ref[...] = (kk * af).astype(BF16)
    bb_ref[...] = (kk * ab).astype(BF16)


def _rw_pre(rw_in, mu, wup, aup, w0, a0, k_k, k_a, r_k, e, et, tm, seq):
    n = rw_in.shape[0]
    w = D_RW_IN
    nbh = n // HALO
    th = tm // HALO
    row = lambda i: (i, 0)
    const = lambda i: (0, 0)
    ospec = pl.BlockSpec((tm, D_RWKV), row)
    sb = jax.ShapeDtypeStruct((n, D_RWKV), BF16)
    sf = jax.ShapeDtypeStruct((n, D_RWKV), F32)
    return pl.pallas_call(
        functools.partial(_rw_pre_kernel, tm=tm, seq=seq),
        out_shape=(sb,) * 7 + (sf, sf, sb),
        grid=(n // tm,),
        in_specs=[pl.BlockSpec((tm, w), row),
                  pl.BlockSpec((HALO, w), lambda i: (jnp.maximum(i * th - 1, 0), 0)),
                  pl.BlockSpec((HALO, w), lambda i: (jnp.minimum((i + 1) * th, nbh - 1), 0)),
                  pl.BlockSpec((2, w), const),
                  pl.BlockSpec((2 * LORA_RANK, 2 * D_RWKV), const),
                  pl.BlockSpec((2 * LORA_RANK, 2 * D_RWKV), const),
                  pl.BlockSpec((1, 2 * D_RWKV), const),
                  pl.BlockSpec((1, 2 * D_RWKV), const),
                  pl.BlockSpec((1, D_RWKV), const),
                  pl.BlockSpec((1, D_RWKV), const),
                  pl.BlockSpec((1, D_RWKV), const),
                  pl.BlockSpec((D_RWKV, LANES), const),
                  pl.BlockSpec((LANES, D_RWKV), const)],
        out_specs=(ospec,) * 10,
        compiler_params=_cparams(("parallel",)),
        name="rw_pre",
    )(rw_in, rw_in, rw_in, mu, wup, aup, w0, a0, k_k, k_a, r_k, e, et)


def _wkv_kernel(rf_ref, kkf_ref, vf_ref, kdf_ref, bf_ref, lwf_ref,
                rb_ref, kkb_ref, vb_ref, kdb_ref, bb_ref, lwb_ref, yf_ref, yb_ref, s_ref):
    t = WKV_CHUNK
    hd = RWKV_HEAD
    w = HEAD_PAIR

    @pl.when(pl.program_id(1) == 0)
    def _():
        s_ref[...] = jnp.zeros_like(s_ref)

    def iota(shape, dim):
        return lax.broadcasted_iota(jnp.int32, shape, dim)

    rr, cc = iota((t, t), 0), iota((t, t), 1)
    rw_, lane = iota((t, w), 0), iota((t, w), 1)
    col = lane % hd
    m0 = lane < hd
    eye_f = (rw_ == col).astype(F32)
    m0_2t = iota((2 * t, w), 1) < hd
    eye_w = iota((w, w), 0) == iota((w, w), 1)
    bd_mask = (iota((w, w), 0) < hd) == (iota((w, w), 1) < hd)

    def direction(rev, refs, y_ref):
        if rev:
            incl, strict, tri = rw_ <= col, rw_ < col, rr <= cc
        else:
            incl, strict, tri = rw_ >= col, rw_ > col, rr >= cc
        return dict(rev=int(rev), refs=refs, y_ref=y_ref, incl=incl, strict=strict,
                    tri=tri.astype(BF16),
                    mask_2t=jnp.concatenate([strict, incl], axis=0), t_last=0 if rev else t - 1)

    dirs = (direction(False, (rf_ref, kkf_ref, vf_ref, kdf_ref, bf_ref, lwf_ref), yf_ref),
            direction(True, (rb_ref, kkb_ref, vb_ref, kdb_ref, bb_ref, lwb_ref), yb_ref))

    def bdiag(x):
        return jnp.concatenate([jnp.where(m0, x, 0.0), jnp.where(m0, 0.0, x)], axis=0)

    def group(gi, carry):
        streams = [(dm, sq) for dm in dirs for sq in range(rf_ref.shape[0])]
        for k in range(0, len(streams), WKV_STREAMS_PER_BATCH):
            batch(streams[k:k + WKV_STREAMS_PER_BATCH], gi)
        return carry

    def batch(streams, gi):
        hv = []
        for dm, sq, j in [(dm, sq, j) for dm, sq in streams for j in range(WKV_PAIRS_PER_STEP)]:
            hp = gi * WKV_PAIRS_PER_STEP + j
            sl = pl.ds(pl.multiple_of(hp * HEAD_PAIR, HEAD_PAIR), HEAD_PAIR)
            r, kk, v, kd, beta = (ref[sq, :, sl].astype(F32) for ref in dm['refs'][:5])
            lw = dm['refs'][5][sq, :, sl]
            t_last = dm['t_last']
            lcum = lw
            for s in (1, 2, 4, 8, 16, 32):
                if dm['rev']:
                    lcum = lcum + jnp.where(rw_ < t - s, pltpu.roll(lcum, t - s, 0), 0.0)
                else:
                    lcum = lcum + jnp.where(rw_ >= s, pltpu.roll(lcum, s, 0), 0.0)
            lcum_x = lcum - lw
            cref = lcum[t // 2:t // 2 + 1, :]
            ltot = lcum[t_last:t_last + 1, :]
            e1 = jnp.exp(lcum - cref)
            e1x = jnp.exp(lcum_x - cref)
            e2 = jnp.exp(cref - lcum)
            ec = jnp.exp(cref)
            ewt = jnp.exp(ltot - cref)
            wtot = jnp.exp(ltot)
            r_t = r * e1
            a_t = -kk * e1x
            b_t = beta * e2
            k_t = kd * e2
            a_0 = a_t * ec
            r_0 = r_t * ec
            b_h = b_t * ewt
            k_h = k_t * ewt
            hv.append(dict(
                dm=dm, sq=sq, hp=hp, sl=sl, wtot=wtot, a_0=a_0, r_0=r_0,
                lm=jnp.concatenate([a_t, r_t], axis=0),
                rm=jnp.concatenate([b_t, k_t], axis=0).astype(BF16),
                bk=jnp.concatenate([b_h, k_h], axis=0).astype(BF16), v=v))
        for h in hv:
            lm2 = jnp.concatenate([jnp.where(m0_2t, h['lm'], 0.0), jnp.where(m0_2t, 0.0, h['lm'])], axis=0)
            g = _dot_nt(lm2.astype(BF16), h['rm'])
            g0, g1 = g[:2 * t], pltpu.roll(g[2 * t:], hd, 1)
            own = jnp.where(m0_2t, g0, g1)
            oth = jnp.where(m0_2t, g1, g0)
            h['a_ab'] = jnp.where(h['dm']['strict'], own[:t], 0.0)
            h['a_rb'] = jnp.where(h['dm']['incl'], own[t:], 0.0).astype(BF16)
            h['a_k'] = jnp.where(h['dm']['mask_2t'], oth, 0.0).astype(BF16)
        for h in hv:
            v = h['v']
            h['inv'] = eye_f + h['a_ab']
            h['nk'] = _dot(h['a_ab'].astype(BF16), bdiag(h['a_ab']).astype(BF16))
            v_rows = jnp.concatenate([jnp.where(m0, 0.0, v), jnp.where(m0, v, 0.0)], axis=0)
            h['av'] = _dot(h['a_k'], v_rows.astype(BF16))
        for s in range(1, NEUMANN_STEPS + 1):
            for h in hv:
                nk_bd = bdiag(h['nk']).astype(BF16)
                if s < NEUMANN_STEPS:
                    nx = _dot(jnp.concatenate([h['nk'], h['inv']], axis=0).astype(BF16), nk_bd)
                    h['nk'] = nx[:t]
                    h['inv'] = h['inv'] + nx[t:]
                else:
                    h['inv'] = h['inv'] + _dot(h['inv'].astype(BF16), nk_bd)
        for h in hv:
            rhs = jnp.concatenate([bdiag(h['a_0']), bdiag(h['av'][:t])], axis=1)
            h['pp'] = _dot(h['inv'].astype(BF16), rhs.astype(BF16))
        for h in hv:
            pp = h['pp']
            rhs = jnp.concatenate([bdiag(pp[:, :w]), bdiag(pp[:, w:])], axis=1)
            h['qq'] = (_dot(h['a_rb'], rhs.astype(BF16))
                       + jnp.concatenate([h['r_0'], h['av'][t:]], axis=1))
            pv = jnp.concatenate([pp, jnp.concatenate([jnp.zeros_like(h['v']), h['v']], axis=1)], axis=0)
            h['mn'] = _dot_tn(h['bk'], pv.astype(BF16))
        for h in hv:
            sidx = (h['dm']['rev'], h['sq'], h['hp'])
            st = s_ref[sidx]
            lhs = jnp.concatenate([h['qq'][:, :w], jnp.where(bd_mask, h['mn'][:, :w], 0.0)], axis=0)
            res = _dot(lhs.astype(BF16), st.astype(BF16))
            h['dm']['y_ref'][h['sq'], :, h['sl']] = res[:t] + h['qq'][:, w:]
            wcol = jnp.sum(jnp.where(eye_w, jnp.broadcast_to(h['wtot'], (w, w)), 0.0),
                           axis=1, keepdims=True)
            s_ref[sidx] = wcol * st + res[t:] + jnp.where(bd_mask, h['mn'][:, w:], 0.0)

    lax.fori_loop(0, RWKV_HEADS // (2 * WKV_PAIRS_PER_STEP), group, 0)


def _wkv(r, kk, v, kdf, bf, lwf, kdb, bb, lwb, nseq):
    n = r.shape[0]
    t = WKV_CHUNK
    seq = n // nseq
    nc = seq // t
    ns = max(k for k in WKV_SEQS if nseq % k == 0)
    fwd = pl.BlockSpec((ns, t, D_RWKV), lambda b, c: (b, c, 0))
    bwd = pl.BlockSpec((ns, t, D_RWKV), lambda b, c: (b, nc - 1 - c, 0))
    shape = jax.ShapeDtypeStruct((nseq, seq, D_RWKV), F32)
    yf, yb = pl.pallas_call(
        _wkv_kernel,
        out_shape=(shape, shape),
        grid=(nseq // ns, nc),
        in_specs=[fwd] * 6 + [bwd] * 6,
        out_specs=(fwd, bwd),
        scratch_shapes=[pltpu.VMEM((2, ns, RWKV_HEADS // 2, HEAD_PAIR, HEAD_PAIR), F32)],
        compiler_params=_cparams(("parallel", "arbitrary")),
        name="wkv",
    )(*[a.reshape(nseq, seq, D_RWKV) for a in (r, kk, v, kdf, bf, lwf, r, kk, v, kdb, bb, lwb)])
    return yf.reshape(n, D_RWKV), yb.reshape(n, D_RWKV)


def _tail_kernel(ys_ref, sg_ref, ms_ref, yf_ref, yb_ref, bonus_ref, rg_ref, mr_ref, x_ref,
                 wglu_ref, swout_ref, lnw_ref, lnb_ref, e_ref, et_ref, rwout_ref, wo_ref, g_ref,
                 o_ref, ytok_ref):
    f32 = lambda ref: ref[...].astype(F32)
    rows = ys_ref.shape[0]
    slabs = ytok_ref.shape[0]
    low = lax.broadcasted_iota(jnp.int32, (rows, LANES), 1) < S5_HALF
    for o in range(slabs):
        for tp in range(S5_CHUNK // 2):
            lane = 2 * o * S5_PACK + tp * LANES
            pack_a = ys_ref[:, lane:lane + LANES].astype(F32)
            pack_b = ys_ref[:, lane + S5_PACK:lane + S5_PACK + LANES].astype(F32)
            ytok_ref[o, pl.ds(2 * tp, rows, stride=S5_CHUNK), :] = (
                jnp.where(low, pack_a, pltpu.roll(pack_b, S5_HALF, 1)))
            ytok_ref[o, pl.ds(2 * tp + 1, rows, stride=S5_CHUNK), :] = (
                jnp.where(low, pltpu.roll(pack_a, S5_HALF, 1), pack_b))
    e, et = e_ref[...], et_ref[...]
    inv_n = 1.0 / RWKV_HEAD
    y = yf_ref[...] + yb_ref[...]
    mean_h = _dot(y.astype(BF16), e)
    z = _gelu_tanh(jnp.concatenate([ytok_ref[o] for o in range(slabs)], axis=1))
    mean = _dot(mean_h.astype(BF16), et) * inv_n
    glu = _dot(z.astype(BF16), wglu_ref[...])
    yc = y - mean
    var_h = _dot((yc * yc).astype(BF16), e)
    z = z * _sigmoid(glu) * _silu(f32(sg_ref))
    var = _dot(var_h.astype(BF16), et) * inv_n
    h = _sigmoid(f32(ms_ref)) * _dot(z.astype(BF16), swout_ref[...])
    y = yc * lax.rsqrt(var + GN_EPS) * lnw_ref[...] + lnb_ref[...] + f32(bonus_ref)
    y = y * _silu(f32(rg_ref))
    h = h + _sigmoid(f32(mr_ref)) * _dot(y.astype(BF16), rwout_ref[...])

    out = _dot(h.astype(BF16), wo_ref[...])
    ms = jnp.mean(out * out, axis=-1, keepdims=True)
    o_ref[...] = x_ref[...] + out * lax.rsqrt(ms + RMS_EPS) * g_ref[...]


def _tail(y_s5, proj, yf, yb, bonus, x, p, tm):
    n = x.shape[0]
    row = lambda i: (i, 0)
    const = lambda i: (0, 0)
    half = pl.BlockSpec((tm, D_RWKV), row)
    full = pl.BlockSpec((tm, D_MODEL), row)

    def proj_cols(width, blk):
        return pl.BlockSpec((tm, width), lambda i: (i, blk))

    def resident(shape):
        return pl.BlockSpec(shape, const, pipeline_mode=pl.Buffered(1))

    return pl.pallas_call(
        _tail_kernel,
        out_shape=jax.ShapeDtypeStruct((n, D_MODEL), F32),
        grid=(n // tm,),
        in_specs=[pl.BlockSpec((tm // S5_CHUNK, S5_CHUNK * D_S5), row),
                  proj_cols(D_S5, PROJ_SG_BLK), proj_cols(D_MODEL, PROJ_MG_BLK),
                  half, half, half,
                  proj_cols(D_RWKV, PROJ_RG_BLK), proj_cols(D_MODEL, PROJ_MG_BLK + 1),
                  full,
                  resident((D_S5, D_S5)), resident((D_S5, D_MODEL)),
                  resident((1, D_RWKV)), resident((1, D_RWKV)),
                  resident((D_RWKV, LANES)), resident((LANES, D_RWKV)),
                  resident((D_RWKV, D_MODEL)), resident((D_MODEL, D_MODEL)),
                  resident((1, D_MODEL))],
        out_specs=full,
        scratch_shapes=[pltpu.VMEM((D_S5 // LANES, tm, LANES), F32)],
        compiler_params=_cparams(("parallel",)),
        name="tail",
    )(y_s5, proj, proj, yf, yb, bonus, proj, proj, x,
      p['w_glu'], p['s5_w_out'], p['ln_w'], p['ln_b'], p['e'], p['et'], p['rw_w_out'], p['w_o'],
      p['post_g'])


def _pick(n, pref):
    while n % pref:
        pref //= 2
    return pref


def _prepare(pre_norm_g, post_norm_g, w_in, s5_lam_re, s5_lam_im, s5_log_dt, s5_b_re, s5_b_im,
             s5_c_re, s5_c_im, s5_d, s5_w_glu, s5_w_out, rw_mu, rw_w0, rw_w_up, rw_a0, rw_a_up,
             rw_k_k, rw_k_a, rw_r_k, rw_ln_w, rw_ln_b, rw_w_out, w_o):
    p = {}
    p['pre_g'] = pre_norm_g.astype(F32).reshape(1, D_MODEL)
    p['post_g'] = post_norm_g.astype(F32).reshape(1, D_MODEL)
    w = w_in
    rw_end = 2 * D_S5 + D_RW_IN
    p['w_u'] = w[:, :D_S5].astype(BF16)
    p['w_proj'] = jnp.concatenate(
        [w[:, 2 * D_S5:rw_end], jnp.zeros((D_MODEL, PROJ_RW_W - D_RW_IN), w.dtype),
         w[:, D_S5:2 * D_S5], w[:, rw_end:]], axis=1).astype(BF16)
    p['s5_p'], p['s5_mq'], p['s5_a'] = _s5_weights(
        s5_lam_re, s5_lam_im, s5_log_dt, s5_b_re, s5_b_im, s5_c_re, s5_c_im, s5_d)
    p['w_glu'] = s5_w_glu.astype(BF16)
    p['s5_w_out'] = s5_w_out.astype(BF16)
    mu = rw_mu.astype(F32).reshape(1, D_RW_IN)
    p['mu'] = jnp.concatenate([1.0 - mu, 0.5 * mu], axis=0)
    zeros = jnp.zeros((LORA_RANK, D_RWKV), F32)

    def blockdiag(u):
        return jnp.concatenate([jnp.concatenate([u[0], zeros], axis=1),
                                jnp.concatenate([zeros, u[1]], axis=1)], axis=0).astype(BF16)

    p['wup'] = blockdiag(rw_w_up.astype(F32))
    p['aup'] = blockdiag(rw_a_up.astype(F32))
    p['w0'] = rw_w0.astype(F32).reshape(1, 2 * D_RWKV)
    p['a0'] = rw_a0.astype(F32).reshape(1, 2 * D_RWKV)
    p['k_k'] = rw_k_k.astype(F32).reshape(1, D_RWKV)
    p['k_a'] = rw_k_a.astype(F32).reshape(1, D_RWKV)
    p['r_k'] = rw_r_k.astype(F32).reshape(1, D_RWKV)
    p['ln_w'] = rw_ln_w.astype(F32).reshape(1, D_RWKV)
    p['ln_b'] = rw_ln_b.astype(F32).reshape(1, D_RWKV)
    p['rw_w_out'] = rw_w_out.astype(BF16)
    p['w_o'] = w_o.astype(BF16)
    e = _head_indicator()
    p['e'] = e
    p['et'] = e.T
    return p


def _s5_mix(x2, p, nseq):
    r = x2.shape[0]
    rb = _pick(r, 512)
    pfr, pfi, pbr, pbi = _s5_mm(x2, p['s5_p'], rb, 4, "s5_chunk_state")
    cb = _pick(r // nseq, 128)
    hfr, hfi, hbr, hbi = _s5_scan(p['s5_a'], pfr, pfi, pbr, pbi, nseq, cb, 4096)
    return _s5_out(x2, hfr, hfi, hbr, hbi, p['s5_mq'], rb)


def _layer(x, p):
    bsz, seq, _ = x.shape
    n = bsz * seq
    x2 = x.reshape(n, D_MODEL)
    u_rows = _in_proj_u(x2, p['pre_g'], p['w_u'], _pick(n, 512))
    proj = _in_proj(x2, p['pre_g'], p['w_proj'], _pick(n, 1024))

    y_s5 = _s5_mix(u_rows, p, bsz)

    (r, kk, v, kdf, kdb, bf, bb, lwf, lwb, bonus) = _rw_pre(
        proj, p['mu'], p['wup'], p['aup'], p['w0'], p['a0'], p['k_k'], p['k_a'], p['r_k'],
        p['e'], p['et'], _pick(seq, 256), seq)
    yf, yb = _wkv(r, kk, v, kdf, bf, lwf, kdb, bb, lwb, bsz)

    out = _tail(y_s5, proj, yf, yb, bonus, x2, p, _pick(n, 256))
    return out.reshape(bsz, seq, D_MODEL)


def kernel(x_prompt, x_sample, pre_norm_g, post_norm_g, w_in, s5_lam_re, s5_lam_im, s5_log_dt, s5_b_re, s5_b_im, s5_c_re, s5_c_im, s5_d, s5_w_glu, s5_w_out, rw_mu, rw_w0, rw_w_up, rw_a0, rw_a_up, rw_k_k, rw_k_a, rw_r_k, rw_ln_w, rw_ln_b, rw_w_out, w_o):
    params = (pre_norm_g, post_norm_g, w_in, s5_lam_re, s5_lam_im, s5_log_dt, s5_b_re, s5_b_im,
              s5_c_re, s5_c_im, s5_d, s5_w_glu, s5_w_out, rw_mu, rw_w0, rw_w_up, rw_a0, rw_a_up,
              rw_k_k, rw_k_a, rw_r_k, rw_ln_w, rw_ln_b, rw_w_out, w_o)
    y_prompt, y_sample = x_prompt, x_sample
    for layer in range(w_in.shape[0]):
        p = _prepare(*[w[layer] for w in params])
        y_prompt = _layer(y_prompt, p)
        y_sample = _layer(y_sample, p)
    return (y_prompt, y_sample)
```
